```python
import jax
import jax.numpy as jnp
from jax import lax
import numpy as np

D_MODEL = 2048
BATCH = 4
SEQ = 4096
DEPTH = 2

F32 = jnp.float32
GRID_W = 64
CTX_LEN = 256
N_EVEN = (DEPTH + 1) // 2
N_ODD = DEPTH // 2

M_HEADS = 4
M_DIM = D_MODEL // 2
M_HEAD_DIM = M_DIM // M_HEADS
M_CHUNK = 128
M_CONV = 3
P_DIM = D_MODEL - M_DIM
P_GROUPS = 4
P_GROUP_DIM = P_DIM // P_GROUPS
POOL_WINDOWS = (2, 4, 8, 16)
IN_AB = 4 * M_DIM + 4 * M_HEADS + P_DIM
A_HEADS = 16
Q_LORA = 1536
KV_LORA = 512
NOPE = 128
ROPE = 64
V_DIM = 128
IN_C = Q_LORA + KV_LORA + ROPE
ROPE_THETA = 10000.0
Q_BLOCK = 128
N_EXPERTS = 16
EC_FACTOR = 2
F_EXPERT = D_MODEL
LN_EPS = 1e-5
RMS_EPS = 1e-6
ALPHA = (2 * DEPTH) ** 0.25
BETA = (8 * DEPTH) ** -0.25

kernel_name = 'hybrid_mlstm_pool_mla_ec_dit'


def layer_norm(x, g, b):
    xf = x.astype(F32)
    mu = jnp.mean(xf, -1, keepdims=True)
    var = jnp.mean(jnp.square(xf - mu), -1, keepdims=True)
    return ((xf - mu) * lax.rsqrt(var + LN_EPS) * g + b).astype(x.dtype)


def rms_norm(x, g):
    xf = x.astype(F32)
    return (xf * lax.rsqrt(jnp.mean(jnp.square(xf), -1, keepdims=True) + RMS_EPS) * g).astype(x.dtype)


def adaln(cond, w, b):
    m = jax.nn.silu(cond) @ w + b
    m = jnp.expand_dims(m.reshape(cond.shape[:-1] + (6, D_MODEL)), -3)
    return [m[..., k, :] for k in range(6)]


def modulate(x, shift, scale):
    return x * (1.0 + scale) + shift


def post_norm(x, y, gate, g, b):
    return layer_norm(ALPHA * x + (1.0 + gate) * y, g, b)


def depthwise_conv_centred(u, w, b):
    out = lax.conv_general_dilated(u, w[:, None, :].astype(u.dtype), window_strides=(1,),
                                   padding=[(M_CONV // 2, M_CONV // 2)],
                                   dimension_numbers=('NWC', 'WIO', 'NWC'),
                                   feature_group_count=u.shape[-1])
    return out + b


def ab_project(h, w_in, conv_w, conv_b, gate_b):
    p = h @ w_in
    qk = jax.nn.silu(depthwise_conv_centred(p[..., :2 * M_DIM], conv_w, conv_b))
    q = qk[..., :M_DIM]
    k = qk[..., M_DIM:] * (M_HEAD_DIM ** -0.5)
    v = p[..., 2 * M_DIM:3 * M_DIM]
    o = p[..., 3 * M_DIM:4 * M_DIM]
    gates = (p[..., 4 * M_DIM:4 * M_DIM + 4 * M_HEADS].astype(F32) + gate_b).reshape(p.shape[:2] + (2, 2, M_HEADS))
    u = p[..., 4 * M_DIM + 4 * M_HEADS:]
    return q, k, v, o, gates, u


def mlstm_directions(q, k, v, gates):
    def heads(a):
        return a.astype(F32).reshape(a.shape[:2] + (M_HEADS, M_HEAD_DIM)).transpose(0, 2, 1, 3)

    def stack(fwd, bwd):
        return jnp.concatenate([fwd, jnp.flip(bwd, 2)], 0)

    qh, kh, vh = heads(q), heads(k), heads(v)
    li = gates[..., 0, :].transpose(2, 0, 3, 1)
    lf = jax.nn.log_sigmoid(gates[..., 1, :]).transpose(2, 0, 3, 1)
    return stack(qh, qh), stack(kh, kh), stack(vh, vh), stack(li[0], li[1]), stack(lf[0], lf[1])


def mlstm_chunk_scan(q, k, v, li, lf, state):
    N, H, T, _ = q.shape
    nc = T // M_CHUNK

    def chunks(a):
        return jnp.moveaxis(a.reshape(a.shape[:2] + (nc, M_CHUNK) + a.shape[3:]), 2, 0)

    tril = jnp.tril(jnp.ones((M_CHUNK, M_CHUNK), bool))

    def step(carry, xs):
        C, n, m = carry
        qc, kc, vc, lic, lfc = xs
        b = jnp.cumsum(lfc, -1)
        dmat = jnp.where(tril, b[..., :, None] - b[..., None, :] + lic[..., None, :], -jnp.inf)
        inter = b + m[..., None]
        m_t = jnp.maximum(inter, jnp.max(dmat, -1))
        w_intra = jnp.exp(dmat - m_t[..., None])
        w_inter = jnp.exp(inter - m_t)
        s = jnp.einsum('nhtd,nhsd->nhts', qc, kc) * w_intra
        num = w_inter[..., None] * jnp.einsum('nhtd,nhde->nhte', qc, C) + jnp.einsum('nhts,nhse->nhte', s, vc)
        den = w_inter * jnp.einsum('nhtd,nhd->nht', qc, n) + jnp.sum(s, -1)
        h = num / jnp.maximum(jnp.abs(den), jnp.exp(-m_t))[..., None]
        bl = b[..., -1]
        g = bl[..., None] - b + lic
        m_new = jnp.maximum(bl + m, jnp.max(g, -1))
        wg = jnp.exp(g - m_new[..., None])
        decay = jnp.exp(bl + m - m_new)
        C_new = decay[..., None, None] * C + jnp.einsum('nhsd,nhse->nhde', kc * wg[..., None], vc)
        n_new = decay[..., None] * n + jnp.einsum('nhs,nhsd->nhd', wg, kc)
        return (C_new, n_new, m_new), h

    state, hs = lax.scan(step, state, (chunks(q), chunks(k), chunks(v), chunks(li), chunks(lf)))
    h = jnp.moveaxis(hs, 0, 2).reshape(N, H, T, -1)
    return h, state


def mlstm_readout(h2, o, head_g):
    nb = h2.shape[0] // 2
    h = (h2[:nb] + jnp.flip(h2[nb:], 2)).transpose(0, 2, 1, 3)
    mu = jnp.mean(h, -1, keepdims=True)
    var = jnp.mean(jnp.square(h - mu), -1, keepdims=True)
    h = (h - mu) * lax.rsqrt(var + LN_EPS) * head_g.reshape(M_HEADS, M_HEAD_DIM)
    return (h.reshape(h.shape[:2] + (M_DIM,)) * jax.nn.sigmoid(o.astype(F32))).astype(o.dtype)


def pool_mixer(u, pool_w, pool_s):
    nb, T, _ = u.shape
    ug = u.astype(F32).reshape(nb, T, P_GROUPS, P_GROUP_DIM)
    cs = jnp.concatenate([jnp.zeros((nb, 1, P_GROUPS, P_GROUP_DIM), F32), jnp.cumsum(ug, axis=1)], 1)
    t = jnp.arange(T)
    outs = []
    for g, w in enumerate(POOL_WINDOWS):
        lo = jnp.clip(t - w // 2, 0, T)
        hi = jnp.clip(t + w // 2, 0, T)
        csg = cs[:, :, g]
        mean = (csg[:, hi] - csg[:, lo]) / (hi - lo).astype(F32)[:, None]
        outs.append(mean - ug[:, :, g])
    pooled = jnp.stack(outs, 2).astype(u.dtype)
    y = jnp.einsum('btgc,gcd->btgd', pooled, pool_w)
    return y.reshape(u.shape) * pool_s


def mixer_mlstm_pool(h_lat, h_ctx, w_in, conv_w, conv_b, gate_b, head_g, pool_w, pool_s, w_out, with_ctx_out):
    n2 = 2 * h_lat.shape[0]
    state0 = (jnp.zeros((n2, M_HEADS, M_HEAD_DIM, M_HEAD_DIM), F32),
              jnp.zeros((n2, M_HEADS, M_HEAD_DIM), F32),
              jnp.zeros((n2, M_HEADS), F32))
    qc, kc, vc, oc, gc, uc = ab_project(h_ctx, w_in, conv_w, conv_b, gate_b)
    h2_ctx, ctx_state = mlstm_chunk_scan(*mlstm_directions(qc, kc, vc, gc), state0)
    ql, kl, vl, ol, gl, ul = ab_project(h_lat, w_in, conv_w, conv_b, gate_b)
    h2_lat, _ = mlstm_chunk_scan(*mlstm_directions(ql, kl, vl, gl), ctx_state)
    y_lat = jnp.concatenate([mlstm_readout(h2_lat, ol, head_g), pool_mixer(ul, pool_w, pool_s)], -1) @ w_out
    if not with_ctx_out:
        return y_lat, None
    y_ctx = jnp.concatenate([mlstm_readout(h2_ctx, oc, head_g), pool_mixer(uc, pool_w, pool_s)], -1) @ w_out
    return y_lat, y_ctx


def axial_rope_tables(n_tokens):
    rows = n_tokens // GRID_W
    row = jnp.repeat(jnp.arange(rows), GRID_W)
    col = jnp.tile(jnp.arange(GRID_W), rows)
    half = ROPE // 2
    inv = ROPE_THETA ** (-jnp.arange(0, half, 2, dtype=F32) / half)
    ang = jnp.stack([row[:, None] * inv, col[:, None] * inv], 1)
    return jnp.cos(ang), jnp.sin(ang)


def apply_axial_rope(x, cos, sin):
    xs = x.reshape(x.shape[:-1] + (2, 2, ROPE // 4))
    x1, x2 = xs[..., 0, :], xs[..., 1, :]
    out = jnp.stack([x1 * cos - x2 * sin, x2 * cos + x1 * sin], -2)
    return out.reshape(x.shape).astype(x.dtype)


def mla_q(pq, q_norm_g, w_uq):
    q = (rms_norm(pq, q_norm_g) @ w_uq).reshape(pq.shape[:2] + (A_HEADS, NOPE + ROPE))
    return q[..., :NOPE], q[..., NOPE:]


def mla_kv(pkv, kv_norm_g, w_ukv):
    kv = (rms_norm(pkv[..., :KV_LORA], kv_norm_g) @ w_ukv).reshape(pkv.shape[:2] + (A_HEADS, NOPE + V_DIM))
    return kv[..., :NOPE], pkv[..., KV_LORA:], kv[..., NOPE:]


def attend_blocks(q_nope, q_pe, k_nope, k_pe, v):
    nb, tq = q_nope.shape[:2]
    nblk = tq // Q_BLOCK
    scale = (NOPE + ROPE) ** -0.5

    def blk(a):
        return jnp.moveaxis(a.reshape((nb, nblk, Q_BLOCK) + a.shape[2:]), 1, 0)

    def one(args):
        qn, qp = args
        s = (jnp.einsum('bqhd,bkhd->bhqk', qn, k_nope, preferred_element_type=F32)
             + jnp.einsum('bqhr,bkr->bhqk', qp, k_pe, preferred_element_type=F32)) * scale
        p = jax.nn.softmax(s, axis=-1).astype(v.dtype)
        return jnp.einsum('bhqk,bkhd->bqhd', p, v)

    out = lax.map(one, (blk(q_nope), blk(q_pe)))
    return jnp.moveaxis(out, 0, 1).reshape(nb, tq, A_HEADS * V_DIM)


def mixer_mla(h_lat, h_ctx, w_in, q_norm_g, kv_norm_g, w_uq, w_ukv, w_o, with_ctx_out):
    cos, sin = axial_rope_tables(h_lat.shape[1])
    p_lat = h_lat @ w_in
    qn_l, qp_l = mla_q(p_lat[..., :Q_LORA], q_norm_g, w_uq)
    qp_l = apply_axial_rope(qp_l, cos[:, None], sin[:, None])
    kn_l, kp_l, v_l = mla_kv(p_lat[..., Q_LORA:], kv_norm_g, w_ukv)
    kp_l = apply_axial_rope(kp_l, cos, sin)
    p_ctx = h_ctx @ (w_in if with_ctx_out else w_in[:, Q_LORA:])
    kn_c, kp_c, v_c = mla_kv(p_ctx[..., -(KV_LORA + ROPE):], kv_norm_g, w_ukv)
    k_nope = jnp.concatenate([kn_c, kn_l], 1)
    k_pe = jnp.concatenate([kp_c, kp_l], 1)
    v = jnp.concatenate([v_c, v_l], 1)
    y_lat = attend_blocks(qn_l, qp_l, k_nope, k_pe, v) @ w_o
    if not with_ctx_out:
        return y_lat, None
    qn_c, qp_c = mla_q(p_ctx[..., :Q_LORA], q_norm_g, w_uq)
    y_ctx = attend_blocks(qn_c, qp_c, kn_c, kp_c, v_c) @ w_o
    return y_lat, y_ctx


def expert_choice_ffn(h, w_router, w1, w3, w2):
    nb, T, _ = h.shape
    cap = EC_FACTOR * T // N_EXPERTS
    aff = jax.nn.softmax(jnp.einsum('btd,de->bte', h, w_router, preferred_element_type=F32), axis=-1)
    gate, idx = lax.top_k(jnp.swapaxes(aff, 1, 2), cap)
    bidx = jnp.arange(nb)[:, None, None]
    xe = h[bidx, idx]
    a = jnp.einsum('becd,edf->becf', xe, w1)
    g = jnp.einsum('becd,edf->becf', xe, w3)
    y = jnp.einsum('becf,efd->becd', jax.nn.silu(a) * g, w2)
    y = y * gate[..., None].astype(y.dtype)
    return jnp.zeros_like(h).at[bidx, idx].add(y)


def setup_inputs(seed: int = 0) -> dict:
    ks = iter(jax.random.split(jax.random.key(seed), 32))

    def nrm(shape, scale):
        return scale * jax.random.normal(next(ks), shape, F32)

    D = D_MODEL
    ne, no = N_EVEN, N_ODD
    i_b = nrm((ne, 2, 1, M_HEADS), 0.1)
    f_b = jnp.linspace(3.0, 6.0, M_HEADS, dtype=F32) + nrm((ne, 2, 1, M_HEADS), 0.1)
    return {
        'x': nrm((BATCH, SEQ, D), 1.0),
        'c': nrm((BATCH, D), 1.0),
        'ctx': nrm((BATCH, CTX_LEN, D), 1.0),
        'c_ctx': nrm((D,), 1.0),
        'ada_w': nrm((DEPTH, D, 6 * D), 0.5 * D ** -0.5),
        'ada_b': nrm((DEPTH, 6 * D), 0.02),
        'ln_g': 1.0 + nrm((DEPTH, 2, D), 0.05),
        'ln_b': nrm((DEPTH, 2, D), 0.02),
        'ab_w_in': nrm((ne, D, IN_AB), D ** -0.5),
        'ab_conv_w': nrm((ne, M_CONV, 2 * M_DIM), M_CONV ** -0.5),
        'ab_conv_b': nrm((ne, 2 * M_DIM), 0.02),
        'ab_gate_b': jnp.concatenate([i_b, f_b], 2).reshape(ne, 4 * M_HEADS),
        'ab_head_g': 1.0 + nrm((ne, M_DIM), 0.05),
        'ab_pool_w': nrm((ne, P_GROUPS, P_GROUP_DIM, P_GROUP_DIM), P_GROUP_DIM ** -0.5),
        'ab_pool_s': 1.0 + nrm((ne, P_DIM), 0.1),
        'ab_w_out': nrm((ne, M_DIM + P_DIM, D), BETA * (M_DIM + P_DIM) ** -0.5),
        'mla_w_in': nrm((no, D, IN_C), D ** -0.5),
        'mla_q_norm_g': 1.0 + nrm((no, Q_LORA), 0.05),
        'mla_kv_norm_g': 1.0 + nrm((no, KV_LORA), 0.05),
        'mla_w_uq': nrm((no, Q_LORA, A_HEADS * (NOPE + ROPE)), Q_LORA ** -0.5),
        'mla_w_ukv': nrm((no, KV_LORA, A_HEADS * (NOPE + V_DIM)), KV_LORA ** -0.5),
        'mla_w_o': nrm((no, A_HEADS * V_DIM, D), BETA * (A_HEADS * V_DIM) ** -0.5),
        'moe_router': nrm((DEPTH, D, N_EXPERTS), D ** -0.5),
        'moe_w1': nrm((DEPTH, N_EXPERTS, D, F_EXPERT), D ** -0.5),
        'moe_w3': nrm((DEPTH, N_EXPERTS, D, F_EXPERT), D ** -0.5),
        'moe_w2': nrm((DEPTH, N_EXPERTS, F_EXPERT, D), BETA * F_EXPERT ** -0.5),
    }


def reference(x, c, ctx, c_ctx, ada_w, ada_b, ln_g, ln_b, ab_w_in, ab_conv_w, ab_conv_b, ab_gate_b,
              ab_head_g, ab_pool_w, ab_pool_s, ab_w_out, mla_w_in, mla_q_norm_g, mla_kv_norm_g,
              mla_w_uq, mla_w_ukv, mla_w_o, moe_router, moe_w1, moe_w3, moe_w2):
    x_lat, x_ctx = x, ctx
    for i in range(DEPTH):
        with_ctx = i < DEPTH - 1
        j = i // 2
        sh_l, sc_l, ga_l, shf_l, scf_l, gaf_l = adaln(c, ada_w[i], ada_b[i])
        sh_c, sc_c, ga_c, shf_c, scf_c, gaf_c = adaln(c_ctx, ada_w[i], ada_b[i])
        h_lat = modulate(x_lat, sh_l, sc_l)
        h_ctx = modulate(x_ctx, sh_c, sc_c)
        if i % 2 == 0:
            y_lat, y_ctx = mixer_mlstm_pool(h_lat, h_ctx, ab_w_in[j], ab_conv_w[j], ab_conv_b[j], ab_gate_b[j],
                                            ab_head_g[j], ab_pool_w[j], ab_pool_s[j], ab_w_out[j], with_ctx)
        else:
            y_lat, y_ctx = mixer_mla(h_lat, h_ctx, mla_w_in[j], mla_q_norm_g[j], mla_kv_norm_g[j],
                                     mla_w_uq[j], mla_w_ukv[j], mla_w_o[j], with_ctx)
        x_lat = post_norm(x_lat, y_lat, ga_l, ln_g[i, 0], ln_b[i, 0])
        f_lat = expert_choice_ffn(modulate(x_lat, shf_l, scf_l), moe_router[i], moe_w1[i], moe_w3[i], moe_w2[i])
        x_lat = post_norm(x_lat, f_lat, gaf_l, ln_g[i, 1], ln_b[i, 1])
        if with_ctx:
            x_ctx = post_norm(x_ctx, y_ctx, ga_c, ln_g[i, 0], ln_b[i, 0])
            f_ctx = expert_choice_ffn(modulate(x_ctx, shf_c, scf_c), moe_router[i], moe_w1[i], moe_w3[i], moe_w2[i])
            x_ctx = post_norm(x_ctx, f_ctx, gaf_c, ln_g[i, 1], ln_b[i, 1])
    return x_lat
```

```python
import functools

import jax
import jax.numpy as jnp
from jax import lax
from jax.experimental import pallas as pl
from jax.experimental.pallas import tpu as pltpu

D_MODEL = 2048
DEPTH = 2
F32 = jnp.float32
BF16 = jnp.bfloat16
GRID_W = 64

M_HEADS = 4
M_DIM = D_MODEL // 2
M_HEAD_DIM = M_DIM // M_HEADS
M_CHUNK = 128
M_CONV = 3
P_DIM = D_MODEL - M_DIM
P_GROUPS = 4
P_GROUP_DIM = P_DIM // P_GROUPS
POOL_WINDOWS = (2, 4, 8, 16)
A_HEADS = 16
Q_LORA = 1536
KV_LORA = 512
NOPE = 128
ROPE = 64
V_DIM = 128
ROPE_THETA = 10000.0
Q_BLOCK = 128
N_EXPERTS = 16
EC_FACTOR = 2
LN_EPS = 1e-5
RMS_EPS = 1e-6
ALPHA = (2 * DEPTH) ** 0.25

VMEM_LIMIT_BYTES = 56 * 1024 * 1024


def _mm_kernel(x_ref, w_ref, o_ref):
    o_ref[...] = jnp.dot(x_ref[...].astype(BF16), w_ref[...].astype(BF16),
                         preferred_element_type=F32).astype(o_ref.dtype)


def _pick_tile(n, prefs):
    for t in prefs:
        if n % t == 0:
            return t
    return n


def matmul(x, w, out_dtype=F32):
    if x.ndim == 2:
        return _matmul_grouped(x[None], w[None], out_dtype)[0]
    return _matmul_grouped(x, w, out_dtype)


def _matmul_grouped(x, w, out_dtype):
    g, m, k = x.shape
    n = w.shape[-1]
    tm = _pick_tile(m, (1024, 512, 256, 128))
    tn = _pick_tile(n, (512, 384, 256, 128))
    return pl.pallas_call(
        _mm_kernel,
        grid=(g, n // tn, m // tm),
        in_specs=[pl.BlockSpec((None, tm, k), lambda e, j, i: (e, i, 0)),
                  pl.BlockSpec((None, k, tn), lambda e, j, i: (e, 0, j))],
        out_specs=pl.BlockSpec((None, tm, tn), lambda e, j, i: (e, i, j)),
        out_shape=jax.ShapeDtypeStruct((g, m, n), out_dtype),
        compiler_params=pltpu.CompilerParams(
            dimension_semantics=("arbitrary", "arbitrary", "arbitrary"),
            vmem_limit_bytes=VMEM_LIMIT_BYTES),
        name="matmul",
    )(x, w)


def dense(x, w, splits=None):
    b, t, k = x.shape
    x2 = x.reshape(b * t, k)
    if splits is None:
        return matmul(x2, w).reshape(b, t, w.shape[-1])
    outs, lo = [], 0
    for width in splits:
        outs.append(matmul(x2, w[:, lo:lo + width]).reshape(b, t, width))
        lo += width
    return outs


def layer_norm(x, g, b):
    mu = jnp.mean(x, -1, keepdims=True)
    var = jnp.mean(jnp.square(x - mu), -1, keepdims=True)
    return (x - mu) * lax.rsqrt(var + LN_EPS) * g + b


def rms_norm(x, g):
    return x * lax.rsqrt(jnp.mean(jnp.square(x), -1, keepdims=True) + RMS_EPS) * g


def adaln(cond, w, b):
    m = jnp.dot(jax.nn.silu(cond), w, precision=lax.Precision.HIGHEST) + b
    m = jnp.expand_dims(m.reshape(cond.shape[:-1] + (6, D_MODEL)), -3)
    return [m[..., k, :] for k in range(6)]


def modulate(x, shift, scale):
    return x * (1.0 + scale) + shift


def post_norm(x, y, gate, g, b):
    return layer_norm(ALPHA * x + (1.0 + gate) * y, g, b)


def depthwise_conv_centred(u, w, b):
    out = lax.conv_general_dilated(u, w[:, None, :], window_strides=(1,),
                                   padding=[(M_CONV // 2, M_CONV // 2)],
                                   dimension_numbers=('NWC', 'WIO', 'NWC'),
                                   feature_group_count=u.shape[-1],
                                   precision=lax.Precision.HIGHEST)
    return out + b


def ab_project(h, w_in, conv_w, conv_b, gate_b):
    p, pg, u = dense(h, w_in, splits=(4 * M_DIM, 4 * M_HEADS, P_DIM))
    qk = jax.nn.silu(depthwise_conv_centred(p[..., :2 * M_DIM], conv_w, conv_b))
    q = qk[..., :M_DIM]
    k = qk[..., M_DIM:] * (M_HEAD_DIM ** -0.5)
    v = p[..., 2 * M_DIM:3 * M_DIM]
    o = p[..., 3 * M_DIM:4 * M_DIM]
    gates = (pg + gate_b).reshape(p.shape[:2] + (2, 2, M_HEADS))
    return q, k, v, o, gates, u


def mlstm_directions(q, k, v, gates):
    def heads(a):
        return a.reshape(a.shape[:2] + (M_HEADS, M_HEAD_DIM)).transpose(0, 2, 1, 3)

    def stack(fwd, bwd):
        return jnp.concatenate([fwd, jnp.flip(bwd, 2)], 0)

    qh, kh, vh = heads(q), heads(k), heads(v)
    li = gates[..., 0, :].transpose(2, 0, 3, 1)
    lf = jax.nn.log_sigmoid(gates[..., 1, :]).transpose(2, 0, 3, 1)
    return stack(qh, qh), stack(kh, kh), stack(vh, vh), stack(li[0], li[1]), stack(lf[0], lf[1])


def mlstm_chunk_scan(q, k, v, li, lf, state):
    N, H, T, _ = q.shape
    nc = T // M_CHUNK

    def chunks(a):
        return jnp.moveaxis(a.reshape(a.shape[:2] + (nc, M_CHUNK) + a.shape[3:]), 2, 0)

    tril = jnp.tril(jnp.ones((M_CHUNK, M_CHUNK), bool))

    def step(carry, xs):
        C, n, m = carry
        qc, kc, vc, lic, lfc = xs
        b = jnp.cumsum(lfc, -1)
        dmat = jnp.where(tril, b[..., :, None] - b[..., None, :] + lic[..., None, :], -jnp.inf)
        inter = b + m[..., None]
        m_t = jnp.maximum(inter, jnp.max(dmat, -1))
        w_intra = jnp.exp(dmat - m_t[..., None])
        w_inter = jnp.exp(inter - m_t)
        s = jnp.einsum('nhtd,nhsd->nhts', qc, kc) * w_intra
        num = w_inter[..., None] * jnp.einsum('nhtd,nhde->nhte', qc, C) + jnp.einsum('nhts,nhse->nhte', s, vc)
        den = w_inter * jnp.einsum('nhtd,nhd->nht', qc, n) + jnp.sum(s, -1)
        h = num / jnp.maximum(jnp.abs(den), jnp.exp(-m_t))[..., None]
        bl = b[..., -1]
        g = bl[..., None] - b + lic
        m_new = jnp.maximum(bl + m, jnp.max(g, -1))
        wg = jnp.exp(g - m_new[..., None])
        decay = jnp.exp(bl + m - m_new)
        C_new = decay[..., None, None] * C + jnp.einsum('nhsd,nhse->nhde', kc * wg[..., None], vc)
        n_new = decay[..., None] * n + jnp.einsum('nhs,nhsd->nhd', wg, kc)
        return (C_new, n_new, m_new), h

    state, hs = lax.scan(step, state, (chunks(q), chunks(k), chunks(v), chunks(li), chunks(lf)))
    h = jnp.moveaxis(hs, 0, 2).reshape(N, H, T, -1)
    return h, state


def mlstm_readout(h2, o, head_g):
    nb = h2.shape[0] // 2
    h = (h2[:nb] + jnp.flip(h2[nb:], 2)).transpose(0, 2, 1, 3)
    mu = jnp.mean(h, -1, keepdims=True)
    var = jnp.mean(jnp.square(h - mu), -1, keepdims=True)
    h = (h - mu) * lax.rsqrt(var + LN_EPS) * head_g.reshape(M_HEADS, M_HEAD_DIM)
    return h.reshape(h.shape[:2] + (M_DIM,)) * jax.nn.sigmoid(o)


def pool_mixer(u, pool_w, pool_s):
    nb, T, _ = u.shape
    ug = u.reshape(nb, T, P_GROUPS, P_GROUP_DIM)
    cs = jnp.concatenate([jnp.zeros((nb, 1, P_GROUPS, P_GROUP_DIM), F32), jnp.cumsum(ug, axis=1)], 1)
    t = jnp.arange(T)
    outs = []
    for g, w in enumerate(POOL_WINDOWS):
        lo = jnp.clip(t - w // 2, 0, T)
        hi = jnp.clip(t + w // 2, 0, T)
        csg = cs[:, :, g]
        mean = (csg[:, hi] - csg[:, lo]) / (hi - lo).astype(F32)[:, None]
        outs.append(mean - ug[:, :, g])
    pooled = jnp.stack(outs, 2)
    pg = pooled.transpose(2, 0, 1, 3).reshape(P_GROUPS, nb * T, P_GROUP_DIM)
    y = matmul(pg, pool_w).reshape(P_GROUPS, nb, T, P_GROUP_DIM).transpose(1, 2, 0, 3)
    return y.reshape(u.shape) * pool_s


def mixer_mlstm_pool(h_lat, h_ctx, w_in, conv_w, conv_b, gate_b, head_g, pool_w, pool_s, w_out, with_ctx_out):
    n2 = 2 * h_lat.shape[0]
    state0 = (jnp.zeros((n2, M_HEADS, M_HEAD_DIM, M_HEAD_DIM), F32),
              jnp.zeros((n2, M_HEADS, M_HEAD_DIM), F32),
              jnp.zeros((n2, M_HEADS), F32))
    qc, kc, vc, oc, gc, uc = ab_project(h_ctx, w_in, conv_w, conv_b, gate_b)
    h2_ctx, ctx_state = mlstm_chunk_scan(*mlstm_directions(qc, kc, vc, gc), state0)
    ql, kl, vl, ol, gl, ul = ab_project(h_lat, w_in, conv_w, conv_b, gate_b)
    h2_lat, _ = mlstm_chunk_scan(*mlstm_directions(ql, kl, vl, gl), ctx_state)
    y_lat = dense(jnp.concatenate([mlstm_readout(h2_lat, ol, head_g), pool_mixer(ul, pool_w, pool_s)], -1), w_out)
    if not with_ctx_out:
        return y_lat, None
    y_ctx = dense(jnp.concatenate([mlstm_readout(h2_ctx, oc, head_g), pool_mixer(uc, pool_w, pool_s)], -1), w_out)
    return y_lat, y_ctx


def axial_rope_tables(n_tokens):
    rows = n_tokens // GRID_W
    row = jnp.repeat(jnp.arange(rows), GRID_W)
    col = jnp.tile(jnp.arange(GRID_W), rows)
    half = ROPE // 2
    inv = ROPE_THETA ** (-jnp.arange(0, half, 2, dtype=F32) / half)
    ang = jnp.stack([row[:, None] * inv, col[:, None] * inv], 1)
    return jnp.cos(ang), jnp.sin(ang)


def apply_axial_rope(x, cos, sin):
    xs = x.reshape(x.shape[:-1] + (2, 2, ROPE // 4))
    x1, x2 = xs[..., 0, :], xs[..., 1, :]
    out = jnp.stack([x1 * cos - x2 * sin, x2 * cos + x1 * sin], -2)
    return out.reshape(x.shape)


def mla_q(pq, q_norm_g, w_uq):
    q = dense(rms_norm(pq, q_norm_g), w_uq).reshape(pq.shape[:2] + (A_HEADS, NOPE + ROPE))
    return q[..., :NOPE], q[..., NOPE:]


def mla_kv(pkv, kv_norm_g, w_ukv):
    kv = dense(rms_norm(pkv[..., :KV_LORA], kv_norm_g), w_ukv).reshape(pkv.shape[:2] + (A_HEADS, NOPE + V_DIM))
    return kv[..., :NOPE], pkv[..., KV_LORA:], kv[..., NOPE:]


def attend_blocks(q_nope, q_pe, k_nope, k_pe, v):
    nb, tq = q_nope.shape[:2]
    nblk = tq // Q_BLOCK
    scale = (NOPE + ROPE) ** -0.5

    def blk(a):
        return jnp.moveaxis(a.reshape((nb, nblk, Q_BLOCK) + a.shape[2:]), 1, 0)

    def one(args):
        qn, qp = args
        s = (jnp.einsum('bqhd,bkhd->bhqk', qn, k_nope, preferred_element_type=F32)
             + jnp.einsum('bqhr,bkr->bhqk', qp, k_pe, preferred_element_type=F32)) * scale
        p = jax.nn.softmax(s, axis=-1)
        return jnp.einsum('bhqk,bkhd->bqhd', p, v)

    out = lax.map(one, (blk(q_nope), blk(q_pe)))
    return jnp.moveaxis(out, 0, 1).reshape(nb, tq, A_HEADS * V_DIM)


def mixer_mla(h_lat, h_ctx, w_in, q_norm_g, kv_norm_g, w_uq, w_ukv, w_o, with_ctx_out):
    cos, sin = axial_rope_tables(h_lat.shape[1])
    pq_l, pkv_l, pr_l = dense(h_lat, w_in, splits=(Q_LORA, KV_LORA, ROPE))
    qn_l, qp_l = mla_q(pq_l, q_norm_g, w_uq)
    qp_l = apply_axial_rope(qp_l, cos[:, None], sin[:, None])
    kn_l, kp_l, v_l = mla_kv(jnp.concatenate([pkv_l, pr_l], -1), kv_norm_g, w_ukv)
    kp_l = apply_axial_rope(kp_l, cos, sin)
    pq_c, pkv_c, pr_c = dense(h_ctx, w_in, splits=(Q_LORA, KV_LORA, ROPE))
    p_ctx = jnp.concatenate([pq_c, pkv_c, pr_c], -1)
    kn_c, kp_c, v_c = mla_kv(p_ctx[..., -(KV_LORA + ROPE):], kv_norm_g, w_ukv)
    k_nope = jnp.concatenate([kn_c, kn_l], 1)
    k_pe = jnp.concatenate([kp_c, kp_l], 1)
    v = jnp.concatenate([v_c, v_l], 1)
    y_lat = dense(attend_blocks(qn_l, qp_l, k_nope, k_pe, v), w_o)
    if not with_ctx_out:
        return y_lat, None
    qn_c, qp_c = mla_q(p_ctx[..., :Q_LORA], q_norm_g, w_uq)
    y_ctx = dense(attend_blocks(qn_c, qp_c, kn_c, kp_c, v_c), w_o)
    return y_lat, y_ctx


def expert_choice_ffn(h, w_router, w1, w3, w2):
    nb, T, d = h.shape
    cap = EC_FACTOR * T // N_EXPERTS
    aff = jax.nn.softmax(jnp.einsum('btd,de->bte', h, w_router, preferred_element_type=F32,
                                    precision=lax.Precision.HIGHEST), axis=-1)
    gate, idx = lax.top_k(jnp.swapaxes(aff, 1, 2), cap)
    bidx = jnp.arange(nb)[:, None, None]
    xe = h[bidx, idx].astype(BF16)
    xe = xe.transpose(1, 0, 2, 3).reshape(N_EXPERTS, nb * cap, d)
    a = matmul(xe, w1)
    g = matmul(xe, w3)
    y = matmul((jax.nn.silu(a) * g).astype(BF16), w2)
    y = y.reshape(N_EXPERTS, nb, cap, d).transpose(1, 0, 2, 3) * gate[..., None]
    return jnp.zeros_like(h).at[bidx, idx].add(y)


def kernel(x, c, ctx, c_ctx, ada_w, ada_b, ln_g, ln_b, ab_w_in, ab_conv_w, ab_conv_b, ab_gate_b,
           ab_head_g, ab_pool_w, ab_pool_s, ab_w_out, mla_w_in, mla_q_norm_g, mla_kv_norm_g,
           mla_w_uq, mla_w_ukv, mla_w_o, moe_router, moe_w1, moe_w3, moe_w2):
    x_lat, x_ctx = x, ctx
    for i in range(DEPTH):
        with_ctx = i < DEPTH - 1
        j = i // 2
        sh_l, sc_l, ga_l, shf_l, scf_l, gaf_l = adaln(c, ada_w[i], ada_b[i])
        sh_c, sc_c, ga_c, shf_c, scf_c, gaf_c = adaln(c_ctx, ada_w[i], ada_b[i])
        h_lat = modulate(x_lat, sh_l, sc_l)
        h_ctx = modulate(x_ctx, sh_c, sc_c)
        if i % 2 == 0:
            y_lat, y_ctx = mixer_mlstm_pool(h_lat, h_ctx, ab_w_in[j], ab_conv_w[j], ab_conv_b[j], ab_gate_b[j],
                                            ab_head_g[j], ab_pool_w[j], ab_pool_s[j], ab_w_out[j], with_ctx)
        else:
            y_lat, y_ctx = mixer_mla(h_lat, h_ctx, mla_w_in[j], mla_q_norm_g[j], mla_kv_norm_g[j],
                                     mla_w_uq[j], mla_w_ukv[j], mla_w_o[j], with_ctx)
        x_lat = post_norm(x_lat, y_lat, ga_l, ln_g[i, 0], ln_b[i, 0])
        f_lat = expert_choice_ffn(modulate(x_lat, shf_l, scf_l), moe_router[i], moe_w1[i], moe_w3[i], moe_w2[i])
        x_lat = post_norm(x_lat, f_lat, gaf_l, ln_g[i, 1], ln_b[i, 1])
        if with_ctx:
            x_ctx = post_norm(x_ctx, y_ctx, ga_c, ln_g[i, 0], ln_b[i, 0])
            f_ctx = expert_choice_ffn(modulate(x_ctx, shf_c, scf_c), moe_router[i], moe_w1[i], moe_w3[i], moe_w2[i])
            x_ctx = post_norm(x_ctx, f_ctx, gaf_c, ln_g[i, 1], ln_b[i, 1])
    return x_lat
```

```python
import jax
import jax.numpy as jnp
from jax import lax
from jax.experimental import pallas as pl
from jax.experimental.pallas import tpu as pltpu

D_MODEL = 2048
DEPTH = 2
F32 = jnp.float32
BF16 = jnp.bfloat16
GRID_W = 64

M_HEADS = 4
M_DIM = D_MODEL // 2
M_HEAD_DIM = M_DIM // M_HEADS
M_CHUNK = 128
M_CONV = 3
P_DIM = D_MODEL - M_DIM
P_GROUPS = 4
P_GROUP_DIM = P_DIM // P_GROUPS
POOL_WINDOWS = (2, 4, 8, 16)
A_HEADS = 16
Q_LORA = 1536
KV_LORA = 512
NOPE = 128
ROPE = 64
V_DIM = 128
ROPE_THETA = 10000.0
N_EXPERTS = 16
EC_FACTOR = 2
LN_EPS = 1e-5
RMS_EPS = 1e-6
ALPHA = (2 * DEPTH) ** 0.25

VMEM_LIMIT_BYTES = 56 * 1024 * 1024


def _params(n_axes):
    return pltpu.CompilerParams(dimension_semantics=("arbitrary",) * n_axes,
                                vmem_limit_bytes=VMEM_LIMIT_BYTES)


def _mm_kernel(x_ref, w_ref, o_ref):
    o_ref[...] = jnp.dot(x_ref[...].astype(BF16), w_ref[...].astype(BF16),
                         preferred_element_type=F32).astype(o_ref.dtype)


def _pick_tile(n, prefs):
    for t in prefs:
        if n % t == 0:
            return t
    return n


def matmul(x, w, out_dtype=F32):
    if x.ndim == 2:
        return _matmul_grouped(x[None], w[None], out_dtype)[0]
    return _matmul_grouped(x, w, out_dtype)


def _matmul_grouped(x, w, out_dtype):
    g, m, k = x.shape
    n = w.shape[-1]
    tm = _pick_tile(m, (1024, 512, 256, 128))
    tn = _pick_tile(n, (512, 384, 256, 128))
    return pl.pallas_call(
        _mm_kernel,
        grid=(g, n // tn, m // tm),
        in_specs=[pl.BlockSpec((None, tm, k), lambda e, j, i: (e, i, 0)),
                  pl.BlockSpec((None, k, tn), lambda e, j, i: (e, 0, j))],
        out_specs=pl.BlockSpec((None, tm, tn), lambda e, j, i: (e, i, j)),
        out_shape=jax.ShapeDtypeStruct((g, m, n), out_dtype),
        compiler_params=_params(3),
        name="matmul",
    )(x, w)


def dense(x, w, splits=None):
    b, t, k = x.shape
    x2 = x.reshape(b * t, k)
    if splits is None:
        return matmul(x2, w).reshape(b, t, w.shape[-1])
    outs, lo = [], 0
    for width in splits:
        outs.append(matmul(x2, w[:, lo:lo + width]).reshape(b, t, width))
        lo += width
    return outs


def layer_norm(x, g, b):
    mu = jnp.mean(x, -1, keepdims=True)
    var = jnp.mean(jnp.square(x - mu), -1, keepdims=True)
    return (x - mu) * lax.rsqrt(var + LN_EPS) * g + b


def rms_norm(x, g):
    return x * lax.rsqrt(jnp.mean(jnp.square(x), -1, keepdims=True) + RMS_EPS) * g


def adaln(cond, w, b):
    m = jnp.dot(jax.nn.silu(cond), w, precision=lax.Precision.HIGHEST) + b
    m = jnp.expand_dims(m.reshape(cond.shape[:-1] + (6, D_MODEL)), -3)
    return [m[..., k, :] for k in range(6)]


def modulate(x, shift, scale):
    return x * (1.0 + scale) + shift


def post_norm(x, y, gate, g, b):
    return layer_norm(ALPHA * x + (1.0 + gate) * y, g, b)


def depthwise_conv_centred(u, w, b):
    up = jnp.pad(u, ((0, 0), (M_CONV // 2, M_CONV // 2), (0, 0)))
    t = u.shape[1]
    out = sum(up[:, j:j + t] * w[j] for j in range(M_CONV))
    return out + b


def ab_project(h, w_in, conv_w, conv_b, gate_b):
    p, pg, u = dense(h, w_in, splits=(4 * M_DIM, 4 * M_HEADS, P_DIM))
    qk = jax.nn.silu(depthwise_conv_centred(p[..., :2 * M_DIM], conv_w, conv_b))
    kscale = jnp.concatenate([jnp.ones((M_DIM,), F32), jnp.full((M_DIM,), M_HEAD_DIM ** -0.5, F32)])
    qk = (qk * kscale).astype(BF16)
    v = p[..., 2 * M_DIM:3 * M_DIM].astype(BF16)
    o = p[..., 3 * M_DIM:4 * M_DIM]
    gates = (pg + gate_b).reshape(p.shape[:2] + (2, 2, M_HEADS))
    li = gates[..., 0, :]
    lf = jax.nn.log_sigmoid(gates[..., 1, :])
    grow = jnp.stack([li, lf], 3).reshape(p.shape[:2] + (4 * M_HEADS,))
    return qk, v, o, grow, u


def _mlstm_chain(q, k, v, li_row, lf_row, c_ref, n_ref, m_ref, reverse):
    L = M_CHUNK
    ti = lax.broadcasted_iota(jnp.int32, (L, L), 0)
    si = lax.broadcasted_iota(jnp.int32, (L, L), 1)
    mask = (si >= ti) if reverse else (si <= ti)
    mask_t = (ti >= si) if reverse else (ti <= si)
    lf_b = jnp.broadcast_to(lf_row, (L, L))
    lf_bt = lf_b.T
    li_col = jnp.broadcast_to(li_row, (L, L)).T[:, 0:1]
    b_col = jnp.sum(jnp.where(mask, lf_b, 0.0), axis=1, keepdims=True)
    b_row = jnp.sum(jnp.where(mask_t, lf_bt, 0.0), axis=0, keepdims=True)
    a_row = li_row - b_row
    dmat = jnp.where(mask, b_col + a_row, -jnp.inf)
    m_prev = m_ref[...][:, 0:1]
    inter = b_col + m_prev
    m_t = jnp.maximum(inter, jnp.max(dmat, axis=1, keepdims=True))
    w_intra = jnp.exp(dmat - m_t)
    w_inter = jnp.exp(inter - m_t)
    s = lax.dot_general(q, k, (((1,), (1,)), ((), ())), preferred_element_type=F32) * w_intra
    c_prev = c_ref[...]
    n_prev = n_ref[...]
    num = (w_inter * jnp.dot(q, c_prev.astype(BF16), preferred_element_type=F32)
           + jnp.dot(s.astype(BF16), v, preferred_element_type=F32))
    den = (w_inter * jnp.sum(q.astype(F32) * n_prev, axis=1, keepdims=True)
           + jnp.sum(s, axis=1, keepdims=True))
    h = num / jnp.maximum(jnp.abs(den), jnp.exp(-m_t))
    bl = jnp.sum(lf_row, axis=1, keepdims=True)
    m_new = jnp.maximum(bl + m_prev, jnp.max(bl + a_row, axis=1, keepdims=True))
    decay = jnp.exp(bl + m_prev - m_new)
    wg_col = jnp.exp(bl - b_col + li_col - m_new)
    kw = k.astype(F32) * wg_col
    c_ref[...] = decay * c_prev + jnp.dot(kw.T.astype(BF16), v, preferred_element_type=F32)
    n_ref[...] = decay * n_prev + jnp.sum(kw, axis=0, keepdims=True)
    m_ref[...] = jnp.broadcast_to(m_new, m_ref.shape)
    return h


def _mlstm_kernel(qf_ref, kf_ref, vf_ref, gf_ref, qb_ref, kb_ref, vb_ref, gb_ref,
                  hf_ref, hb_ref, c_scr, n_scr, m_scr):
    @pl.when(pl.program_id(1) == 0)
    def _():
        c_scr[...] = jnp.zeros_like(c_scr)
        n_scr[...] = jnp.zeros_like(n_scr)
        m_scr[...] = jnp.zeros_like(m_scr)

    dirs = ((qf_ref, kf_ref, vf_ref, gf_ref, hf_ref), (qb_ref, kb_ref, vb_ref, gb_ref, hb_ref))
    for d, (q_ref, k_ref, v_ref, g_ref, h_ref) in enumerate(dirs):
        for hd in range(M_HEADS):
            cols = slice(hd * M_HEAD_DIM, (hd + 1) * M_HEAD_DIM)
            row_i = d * 2 * M_HEADS + hd
            row_f = row_i + M_HEADS
            h = _mlstm_chain(q_ref[:, cols], k_ref[:, cols], v_ref[:, cols],
                             g_ref[row_i:row_i + 1, :], g_ref[row_f:row_f + 1, :],
                             c_scr.at[d, hd], n_scr.at[d, hd], m_scr.at[d, hd], reverse=(d == 1))
            h_ref[:, cols] = h.astype(h_ref.dtype)


def mlstm_scan(qk, v, grow, n_ctx_chunks):
    nb, tt, _ = v.shape
    nc = tt // M_CHUNK
    g = grow.reshape(nb, nc, M_CHUNK, 4 * M_HEADS).transpose(0, 1, 3, 2)

    def fwd(c):
        return c

    def bwd(c):
        return jnp.where(c < n_ctx_chunks, n_ctx_chunks - 1 - c, nc + n_ctx_chunks - 1 - c)

    def specs(cm):
        return [pl.BlockSpec((None, M_CHUNK, M_DIM), lambda b, c: (b, cm(c), 0)),
                pl.BlockSpec((None, M_CHUNK, M_DIM), lambda b, c: (b, cm(c), 1)),
                pl.BlockSpec((None, M_CHUNK, M_DIM), lambda b, c: (b, cm(c), 0)),
                pl.BlockSpec((None, None, 4 * M_HEADS, M_CHUNK), lambda b, c: (b, cm(c), 0, 0))]

    out_sds = jax.ShapeDtypeStruct((nb, tt, M_DIM), F32)
    return pl.pallas_call(
        _mlstm_kernel,
        grid=(nb, nc),
        in_specs=specs(fwd) + specs(bwd),
        out_specs=[pl.BlockSpec((None, M_CHUNK, M_DIM), lambda b, c: (b, fwd(c), 0)),
                   pl.BlockSpec((None, M_CHUNK, M_DIM), lambda b, c: (b, bwd(c), 0))],
        out_shape=[out_sds, out_sds],
        scratch_shapes=[pltpu.VMEM((2, M_HEADS, M_HEAD_DIM, M_HEAD_DIM), F32),
                        pltpu.VMEM((2, M_HEADS, 1, M_HEAD_DIM), F32),
                        pltpu.VMEM((2, M_HEADS, 1, 128), F32)],
        compiler_params=_params(2),
        name="mlstm_scan",
    )(qk, qk, v, g, qk, qk, v, g)


def mlstm_readout(h_sum, o, head_g):
    h = h_sum.reshape(h_sum.shape[:2] + (M_HEADS, M_HEAD_DIM))
    mu = jnp.mean(h, -1, keepdims=True)
    var = jnp.mean(jnp.square(h - mu), -1, keepdims=True)
    h = (h - mu) * lax.rsqrt(var + LN_EPS) * head_g.reshape(M_HEADS, M_HEAD_DIM)
    return h.reshape(h.shape[:2] + (M_DIM,)) * jax.nn.sigmoid(o)


def pool_mixer(u, pool_w, pool_s):
    nb, T, _ = u.shape
    ug = u.reshape(nb, T, P_GROUPS, P_GROUP_DIM)
    cs = jnp.concatenate([jnp.zeros((nb, 1, P_GROUPS, P_GROUP_DIM), F32), jnp.cumsum(ug, axis=1)], 1)
    t = jnp.arange(T)
    outs = []
    for g, w in enumerate(POOL_WINDOWS):
        lo = jnp.clip(t - w // 2, 0, T)
        hi = jnp.clip(t + w // 2, 0, T)
        csg = cs[:, :, g]
        mean = (csg[:, hi] - csg[:, lo]) / (hi - lo).astype(F32)[:, None]
        outs.append(mean - ug[:, :, g])
    pooled = jnp.stack(outs, 2)
    pg = pooled.transpose(2, 0, 1, 3).reshape(P_GROUPS, nb * T, P_GROUP_DIM)
    y = matmul(pg, pool_w).reshape(P_GROUPS, nb, T, P_GROUP_DIM).transpose(1, 2, 0, 3)
    return y.reshape(u.shape) * pool_s


def mixer_mlstm_pool(h_lat, h_ctx, w_in, conv_w, conv_b, gate_b, head_g, pool_w, pool_s, w_out, with_ctx_out):
    t_ctx = h_ctx.shape[1]
    qk_c, v_c, oc, g_c, uc = ab_project(h_ctx, w_in, conv_w, conv_b, gate_b)
    qk_l, v_l, ol, g_l, ul = ab_project(h_lat, w_in, conv_w, conv_b, gate_b)
    h_f, h_b = mlstm_scan(jnp.concatenate([qk_c, qk_l], 1), jnp.concatenate([v_c, v_l], 1),
                          jnp.concatenate([g_c, g_l], 1), t_ctx // M_CHUNK)
    h_sum = h_f + h_b
    y_lat = dense(jnp.concatenate([mlstm_readout(h_sum[:, t_ctx:], ol, head_g),
                                   pool_mixer(ul, pool_w, pool_s)], -1), w_out)
    if not with_ctx_out:
        return y_lat, None
    y_ctx = dense(jnp.concatenate([mlstm_readout(h_sum[:, :t_ctx], oc, head_g),
                                   pool_mixer(uc, pool_w, pool_s)], -1), w_out)
    return y_lat, y_ctx


def axial_rope_tables(n_tokens):
    rows = n_tokens // GRID_W
    row = jnp.repeat(jnp.arange(rows), GRID_W)
    col = jnp.tile(jnp.arange(GRID_W), rows)
    half = ROPE // 2
    inv = ROPE_THETA ** (-jnp.arange(0, half, 2, dtype=F32) / half)
    ang = jnp.stack([row[:, None] * inv, col[:, None] * inv], 1)
    return jnp.cos(ang), jnp.sin(ang)


def apply_axial_rope(x, cos, sin):
    xs = x.reshape(x.shape[:-1] + (2, 2, ROPE // 4))
    x1, x2 = xs[..., 0, :], xs[..., 1, :]
    out = jnp.stack([x1 * cos - x2 * sin, x2 * cos + x1 * sin], -2)
    return out.reshape(x.shape)


def mla_q(pq, q_norm_g, w_uq):
    q = dense(rms_norm(pq, q_norm_g), w_uq).reshape(pq.shape[:2] + (A_HEADS, NOPE + ROPE))
    return q[..., :NOPE], q[..., NOPE:]


def mla_kv(pkv, kv_norm_g, w_ukv):
    kv = dense(rms_norm(pkv[..., :KV_LORA], kv_norm_g), w_ukv).reshape(pkv.shape[:2] + (A_HEADS, NOPE + V_DIM))
    return kv[..., :NOPE], pkv[..., KV_LORA:], kv[..., NOPE:]


def _attn_kernel(q_ref, k_ref, v_ref, o_ref):
    s = lax.dot_general(q_ref[...], k_ref[...], (((1,), (1,)), ((), ())), preferred_element_type=F32)
    m = jnp.max(s, axis=-1, keepdims=True)
    p = jnp.exp2(s - m)
    o = jnp.dot(p.astype(BF16), v_ref[...], preferred_element_type=F32)
    o_ref[...] = (o[:, :V_DIM] / o[:, V_DIM:V_DIM + 1]).astype(o_ref.dtype)


LOG2_E = 1.4426950408889634


def attend_blocks(q_nope, q_pe, k_nope, k_pe, v):
    nb, tq = q_nope.shape[:2]
    tk = k_nope.shape[1]
    scale = (NOPE + ROPE) ** -0.5 * LOG2_E
    q = (jnp.concatenate([q_nope, q_pe], -1) * scale).astype(BF16).transpose(0, 2, 1, 3)
    k_pe_h = jnp.broadcast_to(k_pe[:, :, None, :], k_pe.shape[:2] + (A_HEADS, ROPE))
    k = jnp.concatenate([k_nope, k_pe_h], -1).astype(BF16).transpose(0, 2, 1, 3)
    ones = jnp.concatenate([jnp.ones(v.shape[:3] + (1,), F32), jnp.zeros(v.shape[:3] + (V_DIM - 1,), F32)], -1)
    vh = jnp.concatenate([v, ones], -1).astype(BF16).transpose(0, 2, 1, 3)
    bq = _pick_tile(tq, (512, 256, 128))
    return pl.pallas_call(
        _attn_kernel,
        grid=(nb, A_HEADS, tq // bq),
        in_specs=[pl.BlockSpec((None, None, bq, NOPE + ROPE), lambda b, h, i: (b, h, i, 0)),
                  pl.BlockSpec((None, None, tk, NOPE + ROPE), lambda b, h, i: (b, h, 0, 0)),
                  pl.BlockSpec((None, None, tk, 2 * V_DIM), lambda b, h, i: (b, h, 0, 0))],
        out_specs=pl.BlockSpec((None, bq, V_DIM), lambda b, h, i: (b, i, h)),
        out_shape=jax.ShapeDtypeStruct((nb, tq, A_HEADS * V_DIM), BF16),
        compiler_params=_params(3),
        name="mla_attention",
    )(q, k, vh)


def mixer_mla(h_lat, h_ctx, w_in, q_norm_g, kv_norm_g, w_uq, w_ukv, w_o, with_ctx_out):
    cos, sin = axial_rope_tables(h_lat.shape[1])
    pq_l, pkv_l, pr_l = dense(h_lat, w_in, splits=(Q_LORA, KV_LORA, ROPE))
    qn_l, qp_l = mla_q(pq_l, q_norm_g, w_uq)
    qp_l = apply_axial_rope(qp_l, cos[:, None], sin[:, None])
    kn_l, kp_l, v_l = mla_kv(jnp.concatenate([pkv_l, pr_l], -1), kv_norm_g, w_ukv)
    kp_l = apply_axial_rope(kp_l, cos, sin)
    pq_c, pkv_c, pr_c = dense(h_ctx, w_in, splits=(Q_LORA, KV_LORA, ROPE))
    p_ctx = jnp.concatenate([pq_c, pkv_c, pr_c], -1)
    kn_c, kp_c, v_c = mla_kv(p_ctx[..., -(KV_LORA + ROPE):], kv_norm_g, w_ukv)
    k_nope = jnp.concatenate([kn_c, kn_l], 1)
    k_pe = jnp.concatenate([kp_c, kp_l], 1)
    v = jnp.concatenate([v_c, v_l], 1)
    y_lat = dense(attend_blocks(qn_l, qp_l, k_nope, k_pe, v), w_o)
    if not with_ctx_out:
        return y_lat, None
    qn_c, qp_c = mla_q(p_ctx[..., :Q_LORA], q_norm_g, w_uq)
    y_ctx = dense(attend_blocks(qn_c, qp_c, kn_c, kp_c, v_c), w_o)
    return y_lat, y_ctx


MOE_TF = 256
MOE_TN = 256
MOE_NF = D_MODEL // MOE_TF
MOE_NN = D_MODEL // MOE_TN
HALF_D = D_MODEL // 2


def pack_rows(h):
    hb = lax.bitcast_convert_type(h.astype(BF16), jnp.uint16).astype(jnp.uint32)
    return hb[:, :HALF_D] | (hb[:, HALF_D:] << 16)


def _moe_kernel(rows_ref, h_hbm, w1_ref, w3_ref, w2_ref, y_ref, xg, xb, hmid, sem):
    e = pl.program_id(0)
    j = pl.program_id(1)
    m = xg.shape[0]
    per_step = m // MOE_NF

    def issue(expert, lo, n):
        def body(i, carry):
            r = rows_ref[expert * m + lo + i]
            pltpu.make_async_copy(h_hbm.at[pl.ds(r, 1), :], xg.at[pl.ds(lo + i, 1), :], sem.at[0]).start()
            return carry
        lax.fori_loop(0, n, body, 0, unroll=8)

    @pl.when((e == 0) & (j == 0))
    def _():
        issue(0, 0, m)

    @pl.when(j == 0)
    def _():
        pltpu.make_async_copy(xg, xg, sem.at[0]).wait()
        w = xg[...]
        xb[:, :HALF_D] = pltpu.bitcast(w << 16, F32).astype(BF16)
        xb[:, HALF_D:] = pltpu.bitcast(w & jnp.uint32(0xFFFF0000), F32).astype(BF16)

    @pl.when((j >= 1) & (j <= MOE_NF) & (e + 1 < pl.num_programs(0)))
    def _():
        issue(e + 1, (j - 1) * per_step, per_step)

    @pl.when(j < MOE_NF)
    def _():
        x = xb[...]
        a = jnp.dot(x, w1_ref[...].astype(BF16), preferred_element_type=F32)
        g = jnp.dot(x, w3_ref[...].astype(BF16), preferred_element_type=F32)
        hmid[j] = (jax.nn.silu(a) * g).astype(BF16)

    @pl.when(j >= MOE_NF)
    def _():
        acc = jnp.dot(hmid[0], w2_ref[0:MOE_TF, :].astype(BF16), preferred_element_type=F32)
        for f in range(1, MOE_NF):
            acc += jnp.dot(hmid[f], w2_ref[f * MOE_TF:(f + 1) * MOE_TF, :].astype(BF16),
                           preferred_element_type=F32)
        y_ref[...] = acc


def expert_ffn_rows(h_packed, rows, w1, w3, w2):
    n_e, m = rows.shape
    return pl.pallas_call(
        _moe_kernel,
        grid_spec=pltpu.PrefetchScalarGridSpec(
            num_scalar_prefetch=1,
            grid=(n_e, MOE_NF + MOE_NN),
            in_specs=[pl.BlockSpec(memory_space=pl.ANY),
                      pl.BlockSpec((None, D_MODEL, MOE_TF), lambda e, j, r: (e, 0, jnp.minimum(j, MOE_NF - 1))),
                      pl.BlockSpec((None, D_MODEL, MOE_TF), lambda e, j, r: (e, 0, jnp.minimum(j, MOE_NF - 1))),
                      pl.BlockSpec((None, D_MODEL, MOE_TN), lambda e, j, r: (e, 0, jnp.maximum(j - MOE_NF, 0)))],
            out_specs=pl.BlockSpec((None, m, MOE_TN), lambda e, j, r: (e, 0, jnp.maximum(j - MOE_NF, 0))),
            scratch_shapes=[pltpu.VMEM((m, HALF_D), jnp.uint32),
                            pltpu.VMEM((m, D_MODEL), BF16),
                            pltpu.VMEM((MOE_NF, m, MOE_TF), BF16),
                            pltpu.SemaphoreType.DMA((1,))]),
        out_shape=jax.ShapeDtypeStruct((n_e, m, D_MODEL), F32),
        compiler_params=_params(2),
        name="expert_ffn",
    )(rows.reshape(-1), h_packed, w1, w3, w2)


def route(h, w_router):
    cap = EC_FACTOR * h.shape[1] // N_EXPERTS
    aff = jax.nn.softmax(jnp.einsum('btd,de->bte', h, w_router, preferred_element_type=F32,
                                    precision=lax.Precision.HIGHEST), axis=-1)
    return lax.top_k(jnp.swapaxes(aff, 1, 2), cap)


def expert_choice_ffn(h_sets, w_router, w1, w3, w2):
    nb, d = h_sets[0].shape[0], h_sets[0].shape[-1]
    routed, rows, base = [], [], 0
    for h in h_sets:
        gate, idx = route(h, w_router)
        routed.append((gate, idx))
        t = h.shape[1]
        rid = base + jnp.arange(nb, dtype=jnp.int32)[:, None, None] * t + idx
        rows.append(rid.transpose(1, 0, 2).reshape(N_EXPERTS, -1))
        base += nb * t
    h_packed = pack_rows(jnp.concatenate([h.reshape(-1, d) for h in h_sets], 0))
    y = expert_ffn_rows(h_packed, jnp.concatenate(rows, 1), w1, w3, w2)
    outs, lo = [], 0
    bidx = jnp.arange(nb)[:, None, None]
    for h, (gate, idx) in zip(h_sets, routed):
        cap = idx.shape[-1]
        ys = y[:, lo:lo + nb * cap].reshape(N_EXPERTS, nb, cap, d).transpose(1, 0, 2, 3) * gate[..., None]
        outs.append(jnp.zeros_like(h).at[bidx, idx].add(ys))
        lo += nb * cap
    return outs


def kernel(x, c, ctx, c_ctx, ada_w, ada_b, ln_g, ln_b, ab_w_in, ab_conv_w, ab_conv_b, ab_gate_b,
           ab_head_g, ab_pool_w, ab_pool_s, ab_w_out, mla_w_in, mla_q_norm_g, mla_kv_norm_g,
           mla_w_uq, mla_w_ukv, mla_w_o, moe_router, moe_w1, moe_w3, moe_w2):
    x_lat, x_ctx = x, ctx
    for i in range(DEPTH):
        with_ctx = i < DEPTH - 1
        j = i // 2
        sh_l, sc_l, ga_l, shf_l, scf_l, gaf_l = adaln(c, ada_w[i], ada_b[i])
        sh_c, sc_c, ga_c, shf_c, scf_c, gaf_c = adaln(c_ctx, ada_w[i], ada_b[i])
        h_lat = modulate(x_lat, sh_l, sc_l)
        h_ctx = modulate(x_ctx, sh_c, sc_c)
        if i % 2 == 0:
            y_lat, y_ctx = mixer_mlstm_pool(h_lat, h_ctx, ab_w_in[j], ab_conv_w[j], ab_conv_b[j], ab_gate_b[j],
                                            ab_head_g[j], ab_pool_w[j], ab_pool_s[j], ab_w_out[j], with_ctx)
        else:
            y_lat, y_ctx = mixer_mla(h_lat, h_ctx, mla_w_in[j], mla_q_norm_g[j], mla_kv_norm_g[j],
                                     mla_w_uq[j], mla_w_ukv[j], mla_w_o[j], with_ctx)
        x_lat = post_norm(x_lat, y_lat, ga_l, ln_g[i, 0], ln_b[i, 0])
        h_sets = [modulate(x_lat, shf_l, scf_l)]
        if with_ctx:
            x_ctx = post_norm(x_ctx, y_ctx, ga_c, ln_g[i, 0], ln_b[i, 0])
            h_sets.append(modulate(x_ctx, shf_c, scf_c))
        f_sets = expert_choice_ffn(h_sets, moe_router[i], moe_w1[i], moe_w3[i], moe_w2[i])
        x_lat = post_norm(x_lat, f_sets[0], gaf_l, ln_g[i, 1], ln_b[i, 1])
        if with_ctx:
            x_ctx = post_norm(x_ctx, f_sets[1], gaf_c, ln_g[i, 1], ln_b[i, 1])
    return x_lat
```

```python
import jax
import jax.numpy as jnp
from jax import lax
from jax.experimental import pallas as pl
from jax.experimental.pallas import tpu as pltpu

D_MODEL = 2048
DEPTH = 2
F32 = jnp.float32
BF16 = jnp.bfloat16
GRID_W = 64

M_HEADS = 4
M_DIM = D_MODEL // 2
M_HEAD_DIM = M_DIM // M_HEADS
M_CHUNK = 128
M_CONV = 3
P_DIM = D_MODEL - M_DIM
P_GROUPS = 4
P_GROUP_DIM = P_DIM // P_GROUPS
POOL_WINDOWS = (2, 4, 8, 16)
A_HEADS = 16
Q_LORA = 1536
KV_LORA = 512
NOPE = 128
ROPE = 64
V_DIM = 128
ROPE_THETA = 10000.0
N_EXPERTS = 16
EC_FACTOR = 2
LN_EPS = 1e-5
RMS_EPS = 1e-6
ALPHA = (2 * DEPTH) ** 0.25

VMEM_LIMIT_BYTES = 56 * 1024 * 1024


def _params(n_axes):
    return pltpu.CompilerParams(dimension_semantics=("arbitrary",) * n_axes,
                                vmem_limit_bytes=VMEM_LIMIT_BYTES)


def _mm_kernel(x_ref, w_ref, o_ref):
    o_ref[...] = jnp.dot(x_ref[...].astype(BF16), w_ref[...].astype(BF16),
                         preferred_element_type=F32).astype(o_ref.dtype)


def _pick_tile(n, prefs):
    for t in prefs:
        if n % t == 0:
            return t
    return n


def matmul(x, w, out_dtype=F32):
    if x.ndim == 2:
        return _matmul_grouped(x[None], w[None], out_dtype)[0]
    return _matmul_grouped(x, w, out_dtype)


def _matmul_grouped(x, w, out_dtype):
    g, m, k = x.shape
    n = w.shape[-1]
    tm = _pick_tile(m, (1024, 512, 256, 128))
    tn = _pick_tile(n, (512, 384, 256, 128))
    return pl.pallas_call(
        _mm_kernel,
        grid=(g, n // tn, m // tm),
        in_specs=[pl.BlockSpec((None, tm, k), lambda e, j, i: (e, i, 0)),
                  pl.BlockSpec((None, k, tn), lambda e, j, i: (e, 0, j))],
        out_specs=pl.BlockSpec((None, tm, tn), lambda e, j, i: (e, i, j)),
        out_shape=jax.ShapeDtypeStruct((g, m, n), out_dtype),
        compiler_params=_params(3),
        name="matmul",
    )(x, w)


def dense(x, w, splits=None):
    b, t, k = x.shape
    x2 = x.reshape(b * t, k)
    if splits is None:
        return matmul(x2, w).reshape(b, t, w.shape[-1])
    outs, lo = [], 0
    for width in splits:
        outs.append(matmul(x2, w[:, lo:lo + width]).reshape(b, t, width))
        lo += width
    return outs


def layer_norm(x, g, b):
    mu = jnp.mean(x, -1, keepdims=True)
    var = jnp.mean(jnp.square(x - mu), -1, keepdims=True)
    return (x - mu) * lax.rsqrt(var + LN_EPS) * g + b


def rms_norm(x, g):
    return x * lax.rsqrt(jnp.mean(jnp.square(x), -1, keepdims=True) + RMS_EPS) * g


def adaln(cond, w, b):
    m = jnp.dot(jax.nn.silu(cond), w, precision=lax.Precision.HIGHEST) + b
    m = jnp.expand_dims(m.reshape(cond.shape[:-1] + (6, D_MODEL)), -3)
    return [m[..., k, :] for k in range(6)]


def modulate(x, shift, scale):
    return x * (1.0 + scale) + shift


def post_norm(x, y, gate, g, b):
    return layer_norm(ALPHA * x + (1.0 + gate) * y, g, b)


def depthwise_conv_centred(u, w, b):
    up = jnp.pad(u, ((0, 0), (M_CONV // 2, M_CONV // 2), (0, 0)))
    t = u.shape[1]
    out = sum(up[:, j:j + t] * w[j] for j in range(M_CONV))
    return out + b


def ab_project(h, w_in, conv_w, conv_b, gate_b):
    p, pg, u = dense(h, w_in, splits=(4 * M_DIM, 4 * M_HEADS, P_DIM))
    qk = jax.nn.silu(depthwise_conv_centred(p[..., :2 * M_DIM], conv_w, conv_b))
    kscale = jnp.concatenate([jnp.ones((M_DIM,), F32), jnp.full((M_DIM,), M_HEAD_DIM ** -0.5, F32)])
    qk = (qk * kscale).astype(BF16)
    v = p[..., 2 * M_DIM:3 * M_DIM].astype(BF16)
    o = p[..., 3 * M_DIM:4 * M_DIM]
    gates = (pg + gate_b).reshape(p.shape[:2] + (2, 2, M_HEADS))
    li = gates[..., 0, :]
    lf = jax.nn.log_sigmoid(gates[..., 1, :])
    grow = jnp.stack([li, lf], 3).reshape(p.shape[:2] + (4 * M_HEADS,))
    return qk, v, o, grow, u


def _mlstm_chain(q, k, v, li_row, lf_row, c_ref, n_ref, m_ref, reverse):
    L = M_CHUNK
    ti = lax.broadcasted_iota(jnp.int32, (L, L), 0)
    si = lax.broadcasted_iota(jnp.int32, (L, L), 1)
    mask = (si >= ti) if reverse else (si <= ti)
    mask_t = (ti >= si) if reverse else (ti <= si)
    lf_b = jnp.broadcast_to(lf_row, (L, L))
    lf_bt = lf_b.T
    li_col = jnp.broadcast_to(li_row, (L, L)).T[:, 0:1]
    b_col = jnp.sum(jnp.where(mask, lf_b, 0.0), axis=1, keepdims=True)
    b_row = jnp.sum(jnp.where(mask_t, lf_bt, 0.0), axis=0, keepdims=True)
    a_row = li_row - b_row
    dmat = jnp.where(mask, b_col + a_row, -jnp.inf)
    m_prev = m_ref[...][:, 0:1]
    inter = b_col + m_prev
    m_t = jnp.maximum(inter, jnp.max(dmat, axis=1, keepdims=True))
    w_intra = jnp.exp(dmat - m_t)
    w_inter = jnp.exp(inter - m_t)
    s = lax.dot_general(q, k, (((1,), (1,)), ((), ())), preferred_element_type=F32) * w_intra
    c_prev = c_ref[...]
    n_prev = n_ref[...]
    num = (w_inter * jnp.dot(q, c_prev.astype(BF16), preferred_element_type=F32)
           + jnp.dot(s.astype(BF16), v, preferred_element_type=F32))
    den = (w_inter * jnp.sum(q.astype(F32) * n_prev, axis=1, keepdims=True)
           + jnp.sum(s, axis=1, keepdims=True))
    h = num / jnp.maximum(jnp.abs(den), jnp.exp(-m_t))
    bl = jnp.sum(lf_row, axis=1, keepdims=True)
    m_new = jnp.maximum(bl + m_prev, jnp.max(bl + a_row, axis=1, keepdims=True))
    decay = jnp.exp(bl + m_prev - m_new)
    wg_col = jnp.exp(bl - b_col + li_col - m_new)
    kw = k.astype(F32) * wg_col
    c_ref[...] = decay * c_prev + jnp.dot(kw.T.astype(BF16), v, preferred_element_type=F32)
    n_ref[...] = decay * n_prev + jnp.sum(kw, axis=0, keepdims=True)
    m_ref[...] = jnp.broadcast_to(m_new, m_ref.shape)
    return h


def _mlstm_kernel(qf_ref, kf_ref, vf_ref, gf_ref, qb_ref, kb_ref, vb_ref, gb_ref,
                  hf_ref, hb_ref, c_scr, n_scr, m_scr):
    @pl.when(pl.program_id(1) == 0)
    def _():
        c_scr[...] = jnp.zeros_like(c_scr)
        n_scr[...] = jnp.zeros_like(n_scr)
        m_scr[...] = jnp.zeros_like(m_scr)

    dirs = ((qf_ref, kf_ref, vf_ref, gf_ref, hf_ref), (qb_ref, kb_ref, vb_ref, gb_ref, hb_ref))
    for d, (q_ref, k_ref, v_ref, g_ref, h_ref) in enumerate(dirs):
        for hd in range(M_HEADS):
            cols = slice(hd * M_HEAD_DIM, (hd + 1) * M_HEAD_DIM)
            row_i = d * 2 * M_HEADS + hd
            row_f = row_i + M_HEADS
            h = _mlstm_chain(q_ref[:, cols], k_ref[:, cols], v_ref[:, cols],
                             g_ref[row_i:row_i + 1, :], g_ref[row_f:row_f + 1, :],
                             c_scr.at[d, hd], n_scr.at[d, hd], m_scr.at[d, hd], reverse=(d == 1))
            h_ref[:, cols] = h.astype(h_ref.dtype)


def mlstm_scan(qk, v, grow, n_ctx_chunks):
    nb, tt, _ = v.shape
    nc = tt // M_CHUNK
    g = grow.reshape(nb, nc, M_CHUNK, 4 * M_HEADS).transpose(0, 1, 3, 2)

    def fwd(c):
        return c

    def bwd(c):
        return jnp.where(c < n_ctx_chunks, n_ctx_chunks - 1 - c, nc + n_ctx_chunks - 1 - c)

    def specs(cm):
        return [pl.BlockSpec((None, M_CHUNK, M_DIM), lambda b, c: (b, cm(c), 0)),
                pl.BlockSpec((None, M_CHUNK, M_DIM), lambda b, c: (b, cm(c), 1)),
                pl.BlockSpec((None, M_CHUNK, M_DIM), lambda b, c: (b, cm(c), 0)),
                pl.BlockSpec((None, None, 4 * M_HEADS, M_CHUNK), lambda b, c: (b, cm(c), 0, 0))]

    out_sds = jax.ShapeDtypeStruct((nb, tt, M_DIM), F32)
    return pl.pallas_call(
        _mlstm_kernel,
        grid=(nb, nc),
        in_specs=specs(fwd) + specs(bwd),
        out_specs=[pl.BlockSpec((None, M_CHUNK, M_DIM), lambda b, c: (b, fwd(c), 0)),
                   pl.BlockSpec((None, M_CHUNK, M_DIM), lambda b, c: (b, bwd(c), 0))],
        out_shape=[out_sds, out_sds],
        scratch_shapes=[pltpu.VMEM((2, M_HEADS, M_HEAD_DIM, M_HEAD_DIM), F32),
                        pltpu.VMEM((2, M_HEADS, 1, M_HEAD_DIM), F32),
                        pltpu.VMEM((2, M_HEADS, 1, 128), F32)],
        compiler_params=_params(2),
        name="mlstm_scan",
    )(qk, qk, v, g, qk, qk, v, g)


def mlstm_readout(h_sum, o, head_g):
    h = h_sum.reshape(h_sum.shape[:2] + (M_HEADS, M_HEAD_DIM))
    mu = jnp.mean(h, -1, keepdims=True)
    var = jnp.mean(jnp.square(h - mu), -1, keepdims=True)
    h = (h - mu) * lax.rsqrt(var + LN_EPS) * head_g.reshape(M_HEADS, M_HEAD_DIM)
    return h.reshape(h.shape[:2] + (M_DIM,)) * jax.nn.sigmoid(o)


def pool_mixer(u, pool_w, pool_s):
    nb, T, _ = u.shape
    ug = u.reshape(nb, T, P_GROUPS, P_GROUP_DIM)
    cs = jnp.concatenate([jnp.zeros((nb, 1, P_GROUPS, P_GROUP_DIM), F32), jnp.cumsum(ug, axis=1)], 1)
    t = jnp.arange(T)
    outs = []
    for g, w in enumerate(POOL_WINDOWS):
        lo = jnp.clip(t - w // 2, 0, T)
        hi = jnp.clip(t + w // 2, 0, T)
        csg = cs[:, :, g]
        mean = (csg[:, hi] - csg[:, lo]) / (hi - lo).astype(F32)[:, None]
        outs.append(mean - ug[:, :, g])
    pooled = jnp.stack(outs, 2)
    pg = pooled.transpose(2, 0, 1, 3).reshape(P_GROUPS, nb * T, P_GROUP_DIM)
    y = matmul(pg, pool_w).reshape(P_GROUPS, nb, T, P_GROUP_DIM).transpose(1, 2, 0, 3)
    return y.reshape(u.shape) * pool_s


def mixer_mlstm_pool(h_lat, h_ctx, w_in, conv_w, conv_b, gate_b, head_g, pool_w, pool_s, w_out, with_ctx_out):
    t_ctx = h_ctx.shape[1]
    qk_c, v_c, oc, g_c, uc = ab_project(h_ctx, w_in, conv_w, conv_b, gate_b)
    qk_l, v_l, ol, g_l, ul = ab_project(h_lat, w_in, conv_w, conv_b, gate_b)
    h_f, h_b = mlstm_scan(jnp.concatenate([qk_c, qk_l], 1), jnp.concatenate([v_c, v_l], 1),
                          jnp.concatenate([g_c, g_l], 1), t_ctx // M_CHUNK)
    h_sum = h_f + h_b
    y_lat = dense(jnp.concatenate([mlstm_readout(h_sum[:, t_ctx:], ol, head_g),
                                   pool_mixer(ul, pool_w, pool_s)], -1), w_out)
    if not with_ctx_out:
        return y_lat, None
    y_ctx = dense(jnp.concatenate([mlstm_readout(h_sum[:, :t_ctx], oc, head_g),
                                   pool_mixer(uc, pool_w, pool_s)], -1), w_out)
    return y_lat, y_ctx


def axial_rope_tables(n_tokens):
    rows = n_tokens // GRID_W
    row = jnp.repeat(jnp.arange(rows), GRID_W)
    col = jnp.tile(jnp.arange(GRID_W), rows)
    half = ROPE // 2
    inv = ROPE_THETA ** (-jnp.arange(0, half, 2, dtype=F32) / half)
    ang = jnp.stack([row[:, None] * inv, col[:, None] * inv], 1)
    return jnp.cos(ang), jnp.sin(ang)


def apply_axial_rope(x, cos, sin):
    xs = x.reshape(x.shape[:-1] + (2, 2, ROPE // 4))
    x1, x2 = xs[..., 0, :], xs[..., 1, :]
    out = jnp.stack([x1 * cos - x2 * sin, x2 * cos + x1 * sin], -2)
    return out.reshape(x.shape)


LOG2_E = 1.4426950408889634
Q_SCALE = (NOPE + ROPE) ** -0.5 * LOG2_E


def _rms_norm_rows(x, g):
    return x * lax.rsqrt(jnp.mean(jnp.square(x), axis=-1, keepdims=True) + RMS_EPS) * g


def _q_proj_kernel(x_ref, g_ref, w_ref, tab_ref, o_ref, xn_scr):
    @pl.when(pl.program_id(1) == 0)
    def _():
        xn_scr[...] = _rms_norm_rows(x_ref[...], g_ref[...]).astype(BF16)

    o = jnp.dot(xn_scr[...], w_ref[...], preferred_element_type=F32)
    r = o[:, NOPE:] * tab_ref[...]
    rot = r + pltpu.roll(r, ROPE, axis=1)
    o_ref[...] = (jnp.concatenate([o[:, :NOPE], rot[:, :ROPE]], axis=1) * Q_SCALE).astype(o_ref.dtype)


def q_proj(pq, q_norm_g, w_uq, tab):
    nb, t, _ = pq.shape
    w = w_uq.reshape(Q_LORA, A_HEADS, NOPE + ROPE)
    wpe = w[..., NOPE:].reshape(Q_LORA, A_HEADS, 2, 2, ROPE // 4)
    wsw = jnp.stack([-wpe[..., 1, :], wpe[..., 0, :]], -2).reshape(Q_LORA, A_HEADS, ROPE)
    wh = jnp.concatenate([w, wsw], -1).transpose(1, 0, 2).astype(BF16)
    tm = _pick_tile(t, (1024, 512, 256, 128))
    nt = t // tm
    return pl.pallas_call(
        _q_proj_kernel,
        grid=(nb * nt, A_HEADS),
        in_specs=[pl.BlockSpec((tm, Q_LORA), lambda i, h: (i, 0)),
                  pl.BlockSpec((1, Q_LORA), lambda i, h: (0, 0)),
                  pl.BlockSpec((None, Q_LORA, NOPE + 2 * ROPE), lambda i, h: (h, 0, 0)),
                  pl.BlockSpec((tm, 2 * ROPE), lambda i, h: (i % nt, 0))],
        out_specs=pl.BlockSpec((None, None, tm, NOPE + ROPE), lambda i, h: (i // nt, h, i % nt, 0)),
        out_shape=jax.ShapeDtypeStruct((nb, A_HEADS, t, NOPE + ROPE), BF16),
        scratch_shapes=[pltpu.VMEM((tm, Q_LORA), BF16)],
        compiler_params=_params(2),
        name="mla_q_proj",
    )(pq.reshape(nb * t, Q_LORA), q_norm_g.reshape(1, Q_LORA), wh, tab)


def _kv_proj_kernel(x_ref, g_ref, w_ref, kpe_ref, k_ref, v_ref, xn_scr):
    @pl.when(pl.program_id(1) == 0)
    def _():
        xn_scr[...] = _rms_norm_rows(x_ref[...], g_ref[...]).astype(BF16)

    o = jnp.dot(xn_scr[...], w_ref[...], preferred_element_type=F32)
    k_ref[...] = jnp.concatenate([o[:, :NOPE], kpe_ref[...]], axis=1).astype(k_ref.dtype)
    lane = lax.broadcasted_iota(jnp.int32, (o.shape[0], V_DIM), 1)
    ones_col = jnp.where(lane == 0, 1.0, 0.0)
    v_ref[...] = jnp.concatenate([o[:, NOPE:], ones_col], axis=1).astype(v_ref.dtype)


def kv_proj(pkv, kpe, kv_norm_g, w_ukv):
    nb, tk, _ = pkv.shape
    wh = w_ukv.reshape(KV_LORA, A_HEADS, NOPE + V_DIM).transpose(1, 0, 2).astype(BF16)
    tm = _pick_tile(tk, (2176, 1024, 512, 256, 128))
    nt = tk // tm
    return pl.pallas_call(
        _kv_proj_kernel,
        grid=(nb * nt, A_HEADS),
        in_specs=[pl.BlockSpec((tm, KV_LORA), lambda i, h: (i, 0)),
                  pl.BlockSpec((1, KV_LORA), lambda i, h: (0, 0)),
                  pl.BlockSpec((None, KV_LORA, NOPE + V_DIM), lambda i, h: (h, 0, 0)),
                  pl.BlockSpec((tm, ROPE), lambda i, h: (i, 0))],
        out_specs=[pl.BlockSpec((None, None, tm, NOPE + ROPE), lambda i, h: (i // nt, h, i % nt, 0)),
                   pl.BlockSpec((None, None, tm, 2 * V_DIM), lambda i, h: (i // nt, h, i % nt, 0))],
        out_shape=[jax.ShapeDtypeStruct((nb, A_HEADS, tk, NOPE + ROPE), BF16),
                   jax.ShapeDtypeStruct((nb, A_HEADS, tk, 2 * V_DIM), BF16)],
        scratch_shapes=[pltpu.VMEM((tm, KV_LORA), BF16)],
        compiler_params=_params(2),
        name="mla_kv_proj",
    )(pkv.reshape(nb * tk, KV_LORA), kv_norm_g.reshape(1, KV_LORA), wh, kpe.reshape(nb * tk, ROPE))


def _attn_kernel(q_ref, k_ref, v_ref, o_ref):
    s = lax.dot_general(q_ref[...], k_ref[...], (((1,), (1,)), ((), ())), preferred_element_type=F32)
    m = jnp.max(s, axis=-1, keepdims=True)
    p = jnp.exp2(s - m)
    o = jnp.dot(p.astype(BF16), v_ref[...], preferred_element_type=F32)
    o_ref[...] = (o[:, :V_DIM] / o[:, V_DIM:V_DIM + 1]).astype(o_ref.dtype)


def attend(q, k, vh):
    nb, _, tq, _ = q.shape
    tk = k.shape[2]
    bq = _pick_tile(tq, (512, 256, 128))
    return pl.pallas_call(
        _attn_kernel,
        grid=(nb, A_HEADS, tq // bq),
        in_specs=[pl.BlockSpec((None, None, bq, NOPE + ROPE), lambda b, h, i: (b, h, i, 0)),
                  pl.BlockSpec((None, None, tk, NOPE + ROPE), lambda b, h, i: (b, h, 0, 0)),
                  pl.BlockSpec((None, None, tk, 2 * V_DIM), lambda b, h, i: (b, h, 0, 0))],
        out_specs=pl.BlockSpec((None, bq, V_DIM), lambda b, h, i: (b, i, h)),
        out_shape=jax.ShapeDtypeStruct((nb, tq, A_HEADS * V_DIM), BF16),
        compiler_params=_params(3),
        name="mla_attention",
    )(q, k, vh)


def mixer_mla(h_lat, h_ctx, w_in, q_norm_g, kv_norm_g, w_uq, w_ukv, w_o, with_ctx_out):
    t_lat, t_ctx = h_lat.shape[1], h_ctx.shape[1]
    cos, sin = axial_rope_tables(t_lat)
    cos_full = jnp.repeat(cos, 2, axis=1).reshape(t_lat, ROPE)
    sin_full = jnp.repeat(sin, 2, axis=1).reshape(t_lat, ROPE)
    pq_l, pkv_l, pr_l = dense(h_lat, w_in, splits=(Q_LORA, KV_LORA, ROPE))
    if with_ctx_out:
        pq_c, pkv_c, pr_c = dense(h_ctx, w_in, splits=(Q_LORA, KV_LORA, ROPE))
    else:
        pkv_c, pr_c = dense(h_ctx, w_in[:, Q_LORA:], splits=(KV_LORA, ROPE))
    kpe = jnp.concatenate([pr_c, apply_axial_rope(pr_l, cos, sin)], 1)
    k, vh = kv_proj(jnp.concatenate([pkv_c, pkv_l], 1), kpe, kv_norm_g, w_ukv)
    q_l = q_proj(pq_l, q_norm_g, w_uq, jnp.concatenate([cos_full, sin_full], -1))
    y_lat = dense(attend(q_l, k, vh), w_o)
    if not with_ctx_out:
        return y_lat, None
    no_rot = jnp.concatenate([jnp.ones((t_ctx, ROPE), F32), jnp.zeros((t_ctx, ROPE), F32)], -1)
    q_c = q_proj(pq_c, q_norm_g, w_uq, no_rot)
    y_ctx = dense(attend(q_c, k[:, :, :t_ctx], vh[:, :, :t_ctx]), w_o)
    return y_lat, y_ctx


MOE_TF = 256
MOE_TN = 256
MOE_NF = D_MODEL // MOE_TF
MOE_NN = D_MODEL // MOE_TN
HALF_D = D_MODEL // 2


def pack_rows(h):
    hb = lax.bitcast_convert_type(h.astype(BF16), jnp.uint16).astype(jnp.uint32)
    return hb[:, :HALF_D] | (hb[:, HALF_D:] << 16)


def _moe_kernel(rows_ref, h_hbm, gate_ref, w1_ref, w3_ref, w2_ref, y_ref, xg, xb, hmid, sem):
    e = pl.program_id(0)
    j = pl.program_id(1)
    m = xg.shape[0]
    per_step = m // MOE_NF

    def issue(expert, lo, n):
        def body(i, carry):
            r = rows_ref[expert * m + lo + i]
            pltpu.make_async_copy(h_hbm.at[pl.ds(r, 1), :], xg.at[pl.ds(lo + i, 1), :], sem.at[0]).start()
            return carry
        lax.fori_loop(0, n, body, 0, unroll=8)

    @pl.when((e == 0) & (j == 0))
    def _():
        issue(0, 0, m)

    @pl.when(j == 0)
    def _():
        pltpu.make_async_copy(xg, xg, sem.at[0]).wait()
        w = xg[...]
        xb[:, :HALF_D] = pltpu.bitcast(w << 16, F32).astype(BF16)
        xb[:, HALF_D:] = pltpu.bitcast(w & jnp.uint32(0xFFFF0000), F32).astype(BF16)

    @pl.when((j >= 1) & (j <= MOE_NF) & (e + 1 < pl.num_programs(0)))
    def _():
        issue(e + 1, (j - 1) * per_step, per_step)

    @pl.when(j < MOE_NF)
    def _():
        x = xb[...]
        a = jnp.dot(x, w1_ref[...].astype(BF16), preferred_element_type=F32)
        g = jnp.dot(x, w3_ref[...].astype(BF16), preferred_element_type=F32)
        hmid[j] = (jax.nn.silu(a) * g).astype(BF16)

    @pl.when(j >= MOE_NF)
    def _():
        acc = jnp.dot(hmid[0], w2_ref[0:MOE_TF, :].astype(BF16), preferred_element_type=F32)
        for f in range(1, MOE_NF):
            acc += jnp.dot(hmid[f], w2_ref[f * MOE_TF:(f + 1) * MOE_TF, :].astype(BF16),
                           preferred_element_type=F32)
        y_ref[...] = acc * gate_ref[...]


def expert_ffn_rows(h_packed, rows, gates, w1, w3, w2, layer):
    n_e, m = rows.shape
    return pl.pallas_call(
        _moe_kernel,
        grid_spec=pltpu.PrefetchScalarGridSpec(
            num_scalar_prefetch=1,
            grid=(n_e, MOE_NF + MOE_NN),
            in_specs=[pl.BlockSpec(memory_space=pl.ANY),
                      pl.BlockSpec((None, m, 1), lambda e, j, r: (e, 0, 0)),
                      pl.BlockSpec((None, None, D_MODEL, MOE_TF),
                                   lambda e, j, r: (layer, e, 0, jnp.minimum(j, MOE_NF - 1))),
                      pl.BlockSpec((None, None, D_MODEL, MOE_TF),
                                   lambda e, j, r: (layer, e, 0, jnp.minimum(j, MOE_NF - 1))),
                      pl.BlockSpec((None, None, D_MODEL, MOE_TN),
                                   lambda e, j, r: (layer, e, 0, jnp.maximum(j - MOE_NF, 0)))],
            out_specs=pl.BlockSpec((None, m, MOE_TN), lambda e, j, r: (e, 0, jnp.maximum(j - MOE_NF, 0))),
            scratch_shapes=[pltpu.VMEM((m, HALF_D), jnp.uint32),
                            pltpu.VMEM((m, D_MODEL), BF16),
                            pltpu.VMEM((MOE_NF, m, MOE_TF), BF16),
                            pltpu.SemaphoreType.DMA((1,))]),
        out_shape=jax.ShapeDtypeStruct((n_e, m, D_MODEL), F32),
        compiler_params=_params(2),
        name="expert_ffn",
    )(rows.reshape(-1), h_packed, gates.reshape(n_e, m, 1), w1, w3, w2)


def route(h, w_router):
    cap = EC_FACTOR * h.shape[1] // N_EXPERTS
    aff = jax.nn.softmax(jnp.einsum('btd,de->bte', h, w_router, preferred_element_type=F32,
                                    precision=lax.Precision.HIGHEST), axis=-1)
    return lax.top_k(jnp.swapaxes(aff, 1, 2), cap)


def expert_choice_ffn(h_sets, w_router, w1, w3, w2, layer):
    nb, d = h_sets[0].shape[0], h_sets[0].shape[-1]
    rows, gates, base = [], [], 0
    for h in h_sets:
        gate, idx = route(h, w_router)
        t = h.shape[1]
        rid = base + jnp.arange(nb, dtype=jnp.int32)[:, None, None] * t + idx
        rows.append(rid.transpose(1, 0, 2).reshape(N_EXPERTS, -1))
        gates.append(gate.transpose(1, 0, 2).reshape(N_EXPERTS, -1))
        base += nb * t
    rows = jnp.concatenate(rows, 1)
    h_packed = pack_rows(jnp.concatenate([h.reshape(-1, d) for h in h_sets], 0))
    y = expert_ffn_rows(h_packed, rows, jnp.concatenate(gates, 1), w1, w3, w2, layer)
    f_all = jnp.zeros((base, d), F32).at[rows.reshape(-1)].add(y.reshape(-1, d))
    outs, lo = [], 0
    for h in h_sets:
        n = nb * h.shape[1]
        outs.append(f_all[lo:lo + n].reshape(h.shape))
        lo += n
    return outs


def kernel(x, c, ctx, c_ctx, ada_w, ada_b, ln_g, ln_b, ab_w_in, ab_conv_w, ab_conv_b, ab_gate_b,
           ab_head_g, ab_pool_w, ab_pool_s, ab_w_out, mla_w_in, mla_q_norm_g, mla_kv_norm_g,
           mla_w_uq, mla_w_ukv, mla_w_o, moe_router, moe_w1, moe_w3, moe_w2):
    x_lat, x_ctx = x, ctx
    for i in range(DEPTH):
        with_ctx = i < DEPTH - 1
        j = i // 2
        sh_l, sc_l, ga_l, shf_l, scf_l, gaf_l = adaln(c, ada_w[i], ada_b[i])
        sh_c, sc_c, ga_c, shf_c, scf_c, gaf_c = adaln(c_ctx, ada_w[i], ada_b[i])
        h_lat = modulate(x_lat, sh_l, sc_l)
        h_ctx = modulate(x_ctx, sh_c, sc_c)
        if i % 2 == 0:
            y_lat, y_ctx = mixer_mlstm_pool(h_lat, h_ctx, ab_w_in[j], ab_conv_w[j], ab_conv_b[j], ab_gate_b[j],
                                            ab_head_g[j], ab_pool_w[j], ab_pool_s[j], ab_w_out[j], with_ctx)
        else:
            y_lat, y_ctx = mixer_mla(h_lat, h_ctx, mla_w_in[j], mla_q_norm_g[j], mla_kv_norm_g[j],
                                     mla_w_uq[j], mla_w_ukv[j], mla_w_o[j], with_ctx)
        x_lat = post_norm(x_lat, y_lat, ga_l, ln_g[i, 0], ln_b[i, 0])
        h_sets = [modulate(x_lat, shf_l, scf_l)]
        if with_ctx:
            x_ctx = post_norm(x_ctx, y_ctx, ga_c, ln_g[i, 0], ln_b[i, 0])
            h_sets.append(modulate(x_ctx, shf_c, scf_c))
        f_sets = expert_choice_ffn(h_sets, moe_router[i], moe_w1, moe_w3, moe_w2, i)
        x_lat = post_norm(x_lat, f_sets[0], gaf_l, ln_g[i, 1], ln_b[i, 1])
        if with_ctx:
            x_ctx = post_norm(x_ctx, f_sets[1], gaf_c, ln_g[i, 1], ln_b[i, 1])
    return x_lat
```

```python
import jax
import jax.numpy as jnp
from jax import lax
from jax.experimental import pallas as pl
from jax.experimental.pallas import tpu as pltpu

D_MODEL = 2048
DEPTH = 2
F32 = jnp.float32
BF16 = jnp.bfloat16
GRID_W = 64

M_HEADS = 4
M_DIM = D_MODEL // 2
M_HEAD_DIM = M_DIM // M_HEADS
M_CHUNK = 128
M_CONV = 3
P_DIM = D_MODEL - M_DIM
P_GROUPS = 4
P_GROUP_DIM = P_DIM // P_GROUPS
POOL_WINDOWS = (2, 4, 8, 16)
A_HEADS = 16
Q_LORA = 1536
KV_LORA = 512
NOPE = 128
ROPE = 64
V_DIM = 128
ROPE_THETA = 10000.0
N_EXPERTS = 16
EC_FACTOR = 2
LN_EPS = 1e-5
RMS_EPS = 1e-6
ALPHA = (2 * DEPTH) ** 0.25

VMEM_LIMIT_BYTES = 56 * 1024 * 1024


def _params(n_axes):
    return pltpu.CompilerParams(dimension_semantics=("arbitrary",) * n_axes,
                                vmem_limit_bytes=VMEM_LIMIT_BYTES)


def _mm_kernel(x_ref, w_ref, o_ref):
    o_ref[...] = jnp.dot(x_ref[...].astype(BF16), w_ref[...].astype(BF16),
                         preferred_element_type=F32).astype(o_ref.dtype)


def _pick_tile(n, prefs):
    for t in prefs:
        if n % t == 0:
            return t
    return n


def matmul(x, w, out_dtype=F32):
    if x.ndim == 2:
        return _matmul_grouped(x[None], w[None], out_dtype)[0]
    return _matmul_grouped(x, w, out_dtype)


def _matmul_grouped(x, w, out_dtype):
    g, m, k = x.shape
    n = w.shape[-1]
    tm = _pick_tile(m, (1024, 512, 256, 128))
    tn = _pick_tile(n, (512, 384, 256, 128))
    return pl.pallas_call(
        _mm_kernel,
        grid=(g, n // tn, m // tm),
        in_specs=[pl.BlockSpec((None, tm, k), lambda e, j, i: (e, i, 0)),
                  pl.BlockSpec((None, k, tn), lambda e, j, i: (e, 0, j))],
        out_specs=pl.BlockSpec((None, tm, tn), lambda e, j, i: (e, i, j)),
        out_shape=jax.ShapeDtypeStruct((g, m, n), out_dtype),
        compiler_params=_params(3),
        name="matmul",
    )(x, w)


def dense(x, w, splits=None):
    b, t, k = x.shape
    x2 = x.reshape(b * t, k)
    if splits is None:
        return matmul(x2, w).reshape(b, t, w.shape[-1])
    outs, lo = [], 0
    for width in splits:
        outs.append(matmul(x2, w[:, lo:lo + width]).reshape(b, t, width))
        lo += width
    return outs


def adaln(cond, w, b):
    m = jnp.dot(jax.nn.silu(cond), w, precision=lax.Precision.HIGHEST) + b
    m = m.reshape(cond.shape[0], 6, 1, D_MODEL)
    return [m[:, k] for k in range(6)]


def modulate(x, shift, scale):
    return x * (1.0 + scale) + shift


NORM_ROWS = 256
HALF_D = D_MODEL // 2


def _pack_bf16_pairs(h):
    bits = pltpu.bitcast(h.astype(BF16).astype(F32), jnp.uint32)
    return (bits[:, :HALF_D] >> 16) | (bits[:, HALF_D:] & jnp.uint32(0xFFFF0000))


def _make_post_norm_kernel(with_ffn, with_next):
    def body(*refs):
        x_ref, y_ref, gate_ref, g_ref, b_ref = refs[:5]
        pos = 5
        if with_ffn:
            shf_ref, scf_ref, wr_ref = refs[pos:pos + 3]
            pos += 3
        if with_next:
            shn_ref, scn_ref = refs[pos:pos + 2]
            pos += 2
        outs = refs[pos:]
        z = ALPHA * x_ref[...] + (1.0 + gate_ref[...]) * y_ref[...]
        mu = jnp.mean(z, axis=-1, keepdims=True)
        zc = z - mu
        var = jnp.mean(zc * zc, axis=-1, keepdims=True)
        xn = zc * lax.rsqrt(var + LN_EPS) * g_ref[...] + b_ref[...]
        outs[0][...] = xn
        k = 1
        if with_ffn:
            h = xn * (1.0 + scf_ref[...]) + shf_ref[...]
            outs[k][...] = _pack_bf16_pairs(h)
            outs[k + 1][...] = jnp.dot(h, wr_ref[...], preferred_element_type=F32,
                                       precision=lax.Precision.HIGHEST)
            k += 2
        if with_next:
            outs[k][...] = (xn * (1.0 + scn_ref[...]) + shn_ref[...]).astype(BF16)
    return body


def post_norm_fused(x, y, y_row0, gate, ln_g, ln_b, rows_per_set, ffn=None, nxt=None):
    n = x.shape[0]
    tps = rows_per_set // NORM_ROWS
    off = y_row0 // NORM_ROWS
    row = pl.BlockSpec((NORM_ROWS, D_MODEL), lambda i: (i, 0))
    tab = pl.BlockSpec((None, 1, D_MODEL), lambda i: (i // tps, 0, 0))
    one = pl.BlockSpec((None, 1, D_MODEL), lambda i: (0, 0, 0))
    args = [x, y, gate, ln_g.reshape(1, 1, D_MODEL), ln_b.reshape(1, 1, D_MODEL)]
    in_specs = [row, pl.BlockSpec((NORM_ROWS, D_MODEL), lambda i: (i + off, 0)), tab, one, one]
    out_shape = [jax.ShapeDtypeStruct((n, D_MODEL), F32)]
    out_specs = [row]
    if ffn is not None:
        args += [ffn[0], ffn[1], ffn[2]]
        in_specs += [tab, tab, pl.BlockSpec((D_MODEL, N_EXPERTS), lambda i: (0, 0))]
        out_shape += [jax.ShapeDtypeStruct((n, HALF_D), jnp.uint32), jax.ShapeDtypeStruct((n, N_EXPERTS), F32)]
        out_specs += [pl.BlockSpec((NORM_ROWS, HALF_D), lambda i: (i, 0)),
                      pl.BlockSpec((NORM_ROWS, N_EXPERTS), lambda i: (i, 0))]
    if nxt is not None:
        args += [nxt[0], nxt[1]]
        in_specs += [tab, tab]
        out_shape += [jax.ShapeDtypeStruct((n, D_MODEL), BF16)]
        out_specs += [row]
    return pl.pallas_call(
        _make_post_norm_kernel(ffn is not None, nxt is not None),
        grid=(n // NORM_ROWS,),
        in_specs=in_specs, out_specs=out_specs, out_shape=out_shape,
        compiler_params=_params(1),
        name="post_norm",
    )(*args)


def depthwise_conv_centred(u, w, b):
    up = jnp.pad(u, ((0, 0), (M_CONV // 2, M_CONV // 2), (0, 0)))
    t = u.shape[1]
    out = sum(up[:, j:j + t] * w[j] for j in range(M_CONV))
    return out + b


def ab_project(h, w_in, conv_w, conv_b, gate_b):
    p, pg, u = dense(h, w_in, splits=(4 * M_DIM, 4 * M_HEADS, P_DIM))
    qk = jax.nn.silu(depthwise_conv_centred(p[..., :2 * M_DIM], conv_w, conv_b))
    kscale = jnp.concatenate([jnp.ones((M_DIM,), F32), jnp.full((M_DIM,), M_HEAD_DIM ** -0.5, F32)])
    qk = (qk * kscale).astype(BF16)
    v = p[..., 2 * M_DIM:3 * M_DIM].astype(BF16)
    o = p[..., 3 * M_DIM:4 * M_DIM]
    gates = (pg + gate_b).reshape(p.shape[:2] + (2, 2, M_HEADS))
    li = gates[..., 0, :]
    lf = jax.nn.log_sigmoid(gates[..., 1, :])
    grow = jnp.stack([li, lf], 3).reshape(p.shape[:2] + (4 * M_HEADS,))
    return qk, v, o, grow, u


def _mlstm_chain(q, k, v, li_row, lf_row, c_ref, n_ref, m_ref, reverse):
    L = M_CHUNK
    ti = lax.broadcasted_iota(jnp.int32, (L, L), 0)
    si = lax.broadcasted_iota(jnp.int32, (L, L), 1)
    mask = (si >= ti) if reverse else (si <= ti)
    mask_t = (ti >= si) if reverse else (ti <= si)
    lf_b = jnp.broadcast_to(lf_row, (L, L))
    lf_bt = lf_b.T
    li_col = jnp.broadcast_to(li_row, (L, L)).T[:, 0:1]
    b_col = jnp.sum(jnp.where(mask, lf_b, 0.0), axis=1, keepdims=True)
    b_row = jnp.sum(jnp.where(mask_t, lf_bt, 0.0), axis=0, keepdims=True)
    a_row = li_row - b_row
    dmat = jnp.where(mask, b_col + a_row, -jnp.inf)
    m_prev = m_ref[...][:, 0:1]
    inter = b_col + m_prev
    m_t = jnp.maximum(inter, jnp.max(dmat, axis=1, keepdims=True))
    w_intra = jnp.exp(dmat - m_t)
    w_inter = jnp.exp(inter - m_t)
    s = lax.dot_general(q, k, (((1,), (1,)), ((), ())), preferred_element_type=F32) * w_intra
    c_prev = c_ref[...]
    n_prev = n_ref[...]
    num = (w_inter * jnp.dot(q, c_prev.astype(BF16), preferred_element_type=F32)
           + jnp.dot(s.astype(BF16), v, preferred_element_type=F32))
    den = (w_inter * jnp.sum(q.astype(F32) * n_prev, axis=1, keepdims=True)
           + jnp.sum(s, axis=1, keepdims=True))
    h = num / jnp.maximum(jnp.abs(den), jnp.exp(-m_t))
    bl = jnp.sum(lf_row, axis=1, keepdims=True)
    m_new = jnp.maximum(bl + m_prev, jnp.max(bl + a_row, axis=1, keepdims=True))
    decay = jnp.exp(bl + m_prev - m_new)
    wg_col = jnp.exp(bl - b_col + li_col - m_new)
    kw = k.astype(F32) * wg_col
    c_ref[...] = decay * c_prev + jnp.dot(kw.T.astype(BF16), v, preferred_element_type=F32)
    n_ref[...] = decay * n_prev + jnp.sum(kw, axis=0, keepdims=True)
    m_ref[...] = jnp.broadcast_to(m_new, m_ref.shape)
    return h


def _mlstm_kernel(qf_ref, kf_ref, vf_ref, gf_ref, qb_ref, kb_ref, vb_ref, gb_ref,
                  hf_ref, hb_ref, c_scr, n_scr, m_scr):
    @pl.when(pl.program_id(1) == 0)
    def _():
        c_scr[...] = jnp.zeros_like(c_scr)
        n_scr[...] = jnp.zeros_like(n_scr)
        m_scr[...] = jnp.zeros_like(m_scr)

    dirs = ((qf_ref, kf_ref, vf_ref, gf_ref, hf_ref), (qb_ref, kb_ref, vb_ref, gb_ref, hb_ref))
    for d, (q_ref, k_ref, v_ref, g_ref, h_ref) in enumerate(dirs):
        for hd in range(M_HEADS):
            cols = slice(hd * M_HEAD_DIM, (hd + 1) * M_HEAD_DIM)
            row_i = d * 2 * M_HEADS + hd
            row_f = row_i + M_HEADS
            h = _mlstm_chain(q_ref[:, cols], k_ref[:, cols], v_ref[:, cols],
                             g_ref[row_i:row_i + 1, :], g_ref[row_f:row_f + 1, :],
                             c_scr.at[d, hd], n_scr.at[d, hd], m_scr.at[d, hd], reverse=(d == 1))
            h_ref[:, cols] = h.astype(h_ref.dtype)


def mlstm_scan(qk, v, grow, n_ctx_chunks):
    nb, tt, _ = v.shape
    nc = tt // M_CHUNK
    g = grow.reshape(nb, nc, M_CHUNK, 4 * M_HEADS).transpose(0, 1, 3, 2)

    def fwd(c):
        return c

    def bwd(c):
        return jnp.where(c < n_ctx_chunks, n_ctx_chunks - 1 - c, nc + n_ctx_chunks - 1 - c)

    def specs(cm):
        return [pl.BlockSpec((None, M_CHUNK, M_DIM), lambda b, c: (b, cm(c), 0)),
                pl.BlockSpec((None, M_CHUNK, M_DIM), lambda b, c: (b, cm(c), 1)),
                pl.BlockSpec((None, M_CHUNK, M_DIM), lambda b, c: (b, cm(c), 0)),
                pl.BlockSpec((None, None, 4 * M_HEADS, M_CHUNK), lambda b, c: (b, cm(c), 0, 0))]

    out_sds = jax.ShapeDtypeStruct((nb, tt, M_DIM), F32)
    return pl.pallas_call(
        _mlstm_kernel,
        grid=(nb, nc),
        in_specs=specs(fwd) + specs(bwd),
        out_specs=[pl.BlockSpec((None, M_CHUNK, M_DIM), lambda b, c: (b, fwd(c), 0)),
                   pl.BlockSpec((None, M_CHUNK, M_DIM), lambda b, c: (b, bwd(c), 0))],
        out_shape=[out_sds, out_sds],
        scratch_shapes=[pltpu.VMEM((2, M_HEADS, M_HEAD_DIM, M_HEAD_DIM), F32),
                        pltpu.VMEM((2, M_HEADS, 1, M_HEAD_DIM), F32),
                        pltpu.VMEM((2, M_HEADS, 1, 128), F32)],
        compiler_params=_params(2),
        name="mlstm_scan",
    )(qk, qk, v, g, qk, qk, v, g)


def mlstm_readout(h_sum, o, head_g):
    h = h_sum.reshape(h_sum.shape[:2] + (M_HEADS, M_HEAD_DIM))
    mu = jnp.mean(h, -1, keepdims=True)
    var = jnp.mean(jnp.square(h - mu), -1, keepdims=True)
    h = (h - mu) * lax.rsqrt(var + LN_EPS) * head_g.reshape(M_HEADS, M_HEAD_DIM)
    return h.reshape(h.shape[:2] + (M_DIM,)) * jax.nn.sigmoid(o)


def pool_mixer(u, pool_w, pool_s):
    nb, T, _ = u.shape
    ug = u.reshape(nb, T, P_GROUPS, P_GROUP_DIM)
    cs = jnp.concatenate([jnp.zeros((nb, 1, P_GROUPS, P_GROUP_DIM), F32), jnp.cumsum(ug, axis=1)], 1)
    t = jnp.arange(T)
    outs = []
    for g, w in enumerate(POOL_WINDOWS):
        lo = jnp.clip(t - w // 2, 0, T)
        hi = jnp.clip(t + w // 2, 0, T)
        csg = cs[:, :, g]
        mean = (csg[:, hi] - csg[:, lo]) / (hi - lo).astype(F32)[:, None]
        outs.append(mean - ug[:, :, g])
    pooled = jnp.stack(outs, 2)
    pg = pooled.transpose(2, 0, 1, 3).reshape(P_GROUPS, nb * T, P_GROUP_DIM)
    y = matmul(pg, pool_w).reshape(P_GROUPS, nb, T, P_GROUP_DIM).transpose(1, 2, 0, 3)
    return y.reshape(u.shape) * pool_s


def mixer_mlstm_pool(h_lat, h_ctx, w_in, conv_w, conv_b, gate_b, head_g, pool_w, pool_s, w_out, with_ctx_out):
    t_ctx = h_ctx.shape[1]
    qk_c, v_c, oc, g_c, uc = ab_project(h_ctx, w_in, conv_w, conv_b, gate_b)
    qk_l, v_l, ol, g_l, ul = ab_project(h_lat, w_in, conv_w, conv_b, gate_b)
    h_f, h_b = mlstm_scan(jnp.concatenate([qk_c, qk_l], 1), jnp.concatenate([v_c, v_l], 1),
                          jnp.concatenate([g_c, g_l], 1), t_ctx // M_CHUNK)
    h_sum = h_f + h_b
    y_lat = dense(jnp.concatenate([mlstm_readout(h_sum[:, t_ctx:], ol, head_g),
                                   pool_mixer(ul, pool_w, pool_s)], -1), w_out)
    if not with_ctx_out:
        return y_lat, None
    y_ctx = dense(jnp.concatenate([mlstm_readout(h_sum[:, :t_ctx], oc, head_g),
                                   pool_mixer(uc, pool_w, pool_s)], -1), w_out)
    return y_lat, y_ctx


def axial_rope_tables(n_tokens):
    rows = n_tokens // GRID_W
    row = jnp.repeat(jnp.arange(rows), GRID_W)
    col = jnp.tile(jnp.arange(GRID_W), rows)
    half = ROPE // 2
    inv = ROPE_THETA ** (-jnp.arange(0, half, 2, dtype=F32) / half)
    ang = jnp.stack([row[:, None] * inv, col[:, None] * inv], 1)
    return jnp.cos(ang), jnp.sin(ang)


def apply_axial_rope(x, cos, sin):
    xs = x.reshape(x.shape[:-1] + (2, 2, ROPE // 4))
    x1, x2 = xs[..., 0, :], xs[..., 1, :]
    out = jnp.stack([x1 * cos - x2 * sin, x2 * cos + x1 * sin], -2)
    return out.reshape(x.shape)


LOG2_E = 1.4426950408889634
Q_SCALE = (NOPE + ROPE) ** -0.5 * LOG2_E
ATTN_KEY_SPLITS = 2


def _rms_norm_rows(x, g):
    return x * lax.rsqrt(jnp.mean(jnp.square(x), axis=-1, keepdims=True) + RMS_EPS) * g


def _q_proj_kernel(x_ref, g_ref, w_ref, tab_ref, o_ref, xn_scr):
    @pl.when(pl.program_id(1) == 0)
    def _():
        xn_scr[...] = _rms_norm_rows(x_ref[...], g_ref[...]).astype(BF16)

    o = jnp.dot(xn_scr[...], w_ref[...], preferred_element_type=F32)
    r = o[:, NOPE:] * tab_ref[...]
    rot = r + pltpu.roll(r, ROPE, axis=1)
    o_ref[...] = (jnp.concatenate([o[:, :NOPE], rot[:, :ROPE]], axis=1) * Q_SCALE).astype(o_ref.dtype)


def q_proj(pq, q_norm_g, w_uq, tab):
    nb, t, _ = pq.shape
    w = w_uq.reshape(Q_LORA, A_HEADS, NOPE + ROPE)
    wpe = w[..., NOPE:].reshape(Q_LORA, A_HEADS, 2, 2, ROPE // 4)
    wsw = jnp.stack([-wpe[..., 1, :], wpe[..., 0, :]], -2).reshape(Q_LORA, A_HEADS, ROPE)
    wh = jnp.concatenate([w, wsw], -1).transpose(1, 0, 2).astype(BF16)
    tm = _pick_tile(t, (1024, 512, 256, 128))
    nt = t // tm
    return pl.pallas_call(
        _q_proj_kernel,
        grid=(nb * nt, A_HEADS),
        in_specs=[pl.BlockSpec((tm, Q_LORA), lambda i, h: (i, 0)),
                  pl.BlockSpec((1, Q_LORA), lambda i, h: (0, 0)),
                  pl.BlockSpec((None, Q_LORA, NOPE + 2 * ROPE), lambda i, h: (h, 0, 0)),
                  pl.BlockSpec((tm, 2 * ROPE), lambda i, h: (i % nt, 0))],
        out_specs=pl.BlockSpec((None, None, tm, NOPE + ROPE), lambda i, h: (i // nt, h, i % nt, 0)),
        out_shape=jax.ShapeDtypeStruct((nb, A_HEADS, t, NOPE + ROPE), BF16),
        scratch_shapes=[pltpu.VMEM((tm, Q_LORA), BF16)],
        compiler_params=_params(2),
        name="mla_q_proj",
    )(pq.reshape(nb * t, Q_LORA), q_norm_g.reshape(1, Q_LORA), wh, tab)


def _kv_proj_kernel(x_ref, g_ref, w_ref, kpe_ref, k_ref, v_ref, xn_scr):
    @pl.when(pl.program_id(1) == 0)
    def _():
        xn_scr[...] = _rms_norm_rows(x_ref[...], g_ref[...]).astype(BF16)

    o = jnp.dot(xn_scr[...], w_ref[...], preferred_element_type=F32)
    k_ref[...] = jnp.concatenate([o[:, :NOPE], kpe_ref[...]], axis=1).astype(k_ref.dtype)
    lane = lax.broadcasted_iota(jnp.int32, (o.shape[0], V_DIM), 1)
    ones_col = jnp.where(lane == 0, 1.0, 0.0)
    v_ref[...] = jnp.concatenate([o[:, NOPE:], ones_col], axis=1).astype(v_ref.dtype)


def kv_proj(pkv, kpe, kv_norm_g, w_ukv):
    nb, tk, _ = pkv.shape
    wh = w_ukv.reshape(KV_LORA, A_HEADS, NOPE + V_DIM).transpose(1, 0, 2).astype(BF16)
    tm = _pick_tile(tk, (2176, 1024, 512, 256, 128))
    nt = tk // tm
    return pl.pallas_call(
        _kv_proj_kernel,
        grid=(nb * nt, A_HEADS),
        in_specs=[pl.BlockSpec((tm, KV_LORA), lambda i, h: (i, 0)),
                  pl.BlockSpec((1, KV_LORA), lambda i, h: (0, 0)),
                  pl.BlockSpec((None, KV_LORA, NOPE + V_DIM), lambda i, h: (h, 0, 0)),
                  pl.BlockSpec((tm, ROPE), lambda i, h: (i, 0))],
        out_specs=[pl.BlockSpec((None, None, tm, NOPE + ROPE), lambda i, h: (i // nt, h, i % nt, 0)),
                   pl.BlockSpec((None, None, tm, 2 * V_DIM), lambda i, h: (i // nt, h, i % nt, 0))],
        out_shape=[jax.ShapeDtypeStruct((nb, A_HEADS, tk, NOPE + ROPE), BF16),
                   jax.ShapeDtypeStruct((nb, A_HEADS, tk, 2 * V_DIM), BF16)],
        scratch_shapes=[pltpu.VMEM((tm, KV_LORA), BF16)],
        compiler_params=_params(2),
        name="mla_kv_proj",
    )(pkv.reshape(nb * tk, KV_LORA), kv_norm_g.reshape(1, KV_LORA), wh, kpe.reshape(nb * tk, ROPE))


def _attn_kernel(q_ref, k_ref, v_ref, o_ref):
    q = q_ref[...]
    tk = k_ref.shape[0]
    slab = tk // ATTN_KEY_SPLITS
    m = o = None
    for c in range(ATTN_KEY_SPLITS):
        ks = slice(c * slab, (c + 1) * slab)
        s = lax.dot_general(q, k_ref[ks, :], (((1,), (1,)), ((), ())), preferred_element_type=F32)
        mc = jnp.max(s, axis=-1, keepdims=True)
        m_new = mc if m is None else jnp.maximum(m, mc)
        p = jnp.exp2(s - m_new).astype(BF16)
        oc = jnp.dot(p, v_ref[ks, :], preferred_element_type=F32)
        o = oc if o is None else o * jnp.exp2(m - m_new) + oc
        m = m_new
    o_ref[...] = (o[:, :V_DIM] / o[:, V_DIM:V_DIM + 1]).astype(o_ref.dtype)


def attend(q, k, vh):
    nb, _, tq, _ = q.shape
    tk = k.shape[2]
    bq = _pick_tile(tq, (512, 256, 128))
    return pl.pallas_call(
        _attn_kernel,
        grid=(nb, A_HEADS, tq // bq),
        in_specs=[pl.BlockSpec((None, None, bq, NOPE + ROPE), lambda b, h, i: (b, h, i, 0)),
                  pl.BlockSpec((None, None, tk, NOPE + ROPE), lambda b, h, i: (b, h, 0, 0)),
                  pl.BlockSpec((None, None, tk, 2 * V_DIM), lambda b, h, i: (b, h, 0, 0))],
        out_specs=pl.BlockSpec((None, bq, V_DIM), lambda b, h, i: (b, i, h)),
        out_shape=jax.ShapeDtypeStruct((nb, tq, A_HEADS * V_DIM), BF16),
        compiler_params=_params(3),
        name="mla_attention",
    )(q, k, vh)


def mixer_mla(h_lat, h_ctx, w_in, q_norm_g, kv_norm_g, w_uq, w_ukv, w_o, with_ctx_out):
    t_lat, t_ctx = h_lat.shape[1], h_ctx.shape[1]
    cos, sin = axial_rope_tables(t_lat)
    cos_full = jnp.repeat(cos, 2, axis=1).reshape(t_lat, ROPE)
    sin_full = jnp.repeat(sin, 2, axis=1).reshape(t_lat, ROPE)
    pq_l, pkv_l, pr_l = dense(h_lat, w_in, splits=(Q_LORA, KV_LORA, ROPE))
    if with_ctx_out:
        pq_c, pkv_c, pr_c = dense(h_ctx, w_in, splits=(Q_LORA, KV_LORA, ROPE))
    else:
        pkv_c, pr_c = dense(h_ctx, w_in[:, Q_LORA:], splits=(KV_LORA, ROPE))
    kpe = jnp.concatenate([pr_c, apply_axial_rope(pr_l, cos, sin)], 1)
    k, vh = kv_proj(jnp.concatenate([pkv_c, pkv_l], 1), kpe, kv_norm_g, w_ukv)
    q_l = q_proj(pq_l, q_norm_g, w_uq, jnp.concatenate([cos_full, sin_full], -1))
    y_lat = dense(attend(q_l, k, vh), w_o)
    if not with_ctx_out:
        return y_lat, None
    no_rot = jnp.concatenate([jnp.ones((t_ctx, ROPE), F32), jnp.zeros((t_ctx, ROPE), F32)], -1)
    q_c = q_proj(pq_c, q_norm_g, w_uq, no_rot)
    y_ctx = dense(attend(q_c, k[:, :, :t_ctx], vh[:, :, :t_ctx]), w_o)
    return y_lat, y_ctx


MOE_TF = 256
MOE_TN = 256
MOE_NF = D_MODEL // MOE_TF
MOE_NN = D_MODEL // MOE_TN


def _moe_kernel(rows_ref, h_hbm, gate_ref, w1_ref, w3_ref, w2_ref, y_ref, xg, xb, hmid, sem):
    e = pl.program_id(0)
    j = pl.program_id(1)
    m = xg.shape[0]
    per_step = m // MOE_NF

    def issue(expert, lo, n):
        def body(i, carry):
            r = rows_ref[expert * m + lo + i]
            pltpu.make_async_copy(h_hbm.at[pl.ds(r, 1), :], xg.at[pl.ds(lo + i, 1), :], sem.at[0]).start()
            return carry
        lax.fori_loop(0, n, body, 0, unroll=8)

    @pl.when((e == 0) & (j == 0))
    def _():
        issue(0, 0, m)

    @pl.when(j == 0)
    def _():
        pltpu.make_async_copy(xg, xg, sem.at[0]).wait()
        w = xg[...]
        xb[:, :HALF_D] = pltpu.bitcast(w << 16, F32).astype(BF16)
        xb[:, HALF_D:] = pltpu.bitcast(w & jnp.uint32(0xFFFF0000), F32).astype(BF16)

    @pl.when((j >= 1) & (j <= MOE_NF) & (e + 1 < pl.num_programs(0)))
    def _():
        issue(e + 1, (j - 1) * per_step, per_step)

    @pl.when(j < MOE_NF)
    def _():
        x = xb[...]
        a = jnp.dot(x, w1_ref[...].astype(BF16), preferred_element_type=F32)
        g = jnp.dot(x, w3_ref[...].astype(BF16), preferred_element_type=F32)
        hmid[j] = (jax.nn.silu(a) * g).astype(BF16)

    @pl.when(j >= MOE_NF)
    def _():
        acc = jnp.dot(hmid[0], w2_ref[0:MOE_TF, :].astype(BF16), preferred_element_type=F32)
        for f in range(1, MOE_NF):
            acc += jnp.dot(hmid[f], w2_ref[f * MOE_TF:(f + 1) * MOE_TF, :].astype(BF16),
                           preferred_element_type=F32)
        y_ref[...] = acc * gate_ref[...]


def expert_ffn_rows(h_packed, rows, gates, w1, w3, w2, layer):
    n_e, m = rows.shape
    return pl.pallas_call(
        _moe_kernel,
        grid_spec=pltpu.PrefetchScalarGridSpec(
            num_scalar_prefetch=1,
            grid=(n_e, MOE_NF + MOE_NN),
            in_specs=[pl.BlockSpec(memory_space=pl.ANY),
                      pl.BlockSpec((None, m, 1), lambda e, j, r: (e, 0, 0)),
                      pl.BlockSpec((None, None, D_MODEL, MOE_TF),
                                   lambda e, j, r: (layer, e, 0, jnp.minimum(j, MOE_NF - 1))),
                      pl.BlockSpec((None, None, D_MODEL, MOE_TF),
                                   lambda e, j, r: (layer, e, 0, jnp.minimum(j, MOE_NF - 1))),
                      pl.BlockSpec((None, None, D_MODEL, MOE_TN),
                                   lambda e, j, r: (layer, e, 0, jnp.maximum(j - MOE_NF, 0)))],
            out_specs=pl.BlockSpec((None, m, MOE_TN), lambda e, j, r: (e, 0, jnp.maximum(j - MOE_NF, 0))),
            scratch_shapes=[pltpu.VMEM((m, HALF_D), jnp.uint32),
                            pltpu.VMEM((m, D_MODEL), BF16),
                            pltpu.VMEM((MOE_NF, m, MOE_TF), BF16),
                            pltpu.SemaphoreType.DMA((1,))]),
        out_shape=jax.ShapeDtypeStruct((n_e, m, D_MODEL), F32),
        compiler_params=_params(2),
        name="expert_ffn",
    )(rows.reshape(-1), h_packed, gates.reshape(n_e, m, 1), w1, w3, w2)


def route(logits, nb):
    t = logits.shape[0] // nb
    cap = EC_FACTOR * t // N_EXPERTS
    aff = jax.nn.softmax(logits.reshape(nb, t, N_EXPERTS), axis=-1)
    return lax.top_k(jnp.swapaxes(aff, 1, 2), cap)


def expert_choice_ffn(sets, nb, w1, w3, w2, layer):
    rows, gates, base = [], [], 0
    for _, logits in sets:
        gate, idx = route(logits, nb)
        t = logits.shape[0] // nb
        rid = base + jnp.arange(nb, dtype=jnp.int32)[:, None, None] * t + idx
        rows.append(rid.transpose(1, 0, 2).reshape(N_EXPERTS, -1))
        gates.append(gate.transpose(1, 0, 2).reshape(N_EXPERTS, -1))
        base += nb * t
    rows = jnp.concatenate(rows, 1)
    h_packed = jnp.concatenate([hp for hp, _ in sets], 0) if len(sets) > 1 else sets[0][0]
    y = expert_ffn_rows(h_packed, rows, jnp.concatenate(gates, 1), w1, w3, w2, layer)
    return jnp.zeros((base, D_MODEL), F32).at[rows.reshape(-1)].add(y.reshape(-1, D_MODEL))


def kernel(x, c, ctx, c_ctx, ada_w, ada_b, ln_g, ln_b, ab_w_in, ab_conv_w, ab_conv_b, ab_gate_b,
           ab_head_g, ab_pool_w, ab_pool_s, ab_w_out, mla_w_in, mla_q_norm_g, mla_kv_norm_g,
           mla_w_uq, mla_w_ukv, mla_w_o, moe_router, moe_w1, moe_w3, moe_w2):
    nb, t_lat, _ = x.shape
    t_ctx = ctx.shape[1]
    n_lat, n_ctx = nb * t_lat, nb * t_ctx
    mods_l = [adaln(c, ada_w[i], ada_b[i]) for i in range(DEPTH)]
    mods_c = [adaln(c_ctx[None], ada_w[i], ada_b[i]) for i in range(DEPTH)]
    x_lat, x_ctx = x.reshape(n_lat, D_MODEL), ctx.reshape(n_ctx, D_MODEL)
    h_lat = modulate(x, mods_l[0][0], mods_l[0][1]).astype(BF16)
    h_ctx = modulate(ctx, mods_c[0][0], mods_c[0][1]).astype(BF16)
    for i in range(DEPTH):
        with_ctx = i < DEPTH - 1
        j = i // 2
        _, _, ga_l, shf_l, scf_l, gaf_l = mods_l[i]
        _, _, ga_c, shf_c, scf_c, gaf_c = mods_c[i]
        if i % 2 == 0:
            y_lat, y_ctx = mixer_mlstm_pool(h_lat, h_ctx, ab_w_in[j], ab_conv_w[j], ab_conv_b[j], ab_gate_b[j],
                                            ab_head_g[j], ab_pool_w[j], ab_pool_s[j], ab_w_out[j], with_ctx)
        else:
            y_lat, y_ctx = mixer_mla(h_lat, h_ctx, mla_w_in[j], mla_q_norm_g[j], mla_kv_norm_g[j],
                                     mla_w_uq[j], mla_w_ukv[j], mla_w_o[j], with_ctx)
        x_lat, hp_l, lg_l = post_norm_fused(x_lat, y_lat.reshape(n_lat, D_MODEL), 0, ga_l, ln_g[i, 0], ln_b[i, 0],
                                            t_lat, ffn=(shf_l, scf_l, moe_router[i]))
        sets = [(hp_l, lg_l)]
        if with_ctx:
            x_ctx, hp_c, lg_c = post_norm_fused(x_ctx, y_ctx.reshape(n_ctx, D_MODEL), 0, ga_c, ln_g[i, 0],
                                                ln_b[i, 0], n_ctx, ffn=(shf_c, scf_c, moe_router[i]))
            sets.append((hp_c, lg_c))
        f_all = expert_choice_ffn(sets, nb, moe_w1, moe_w3, moe_w2, i)
        nxt_l = (mods_l[i + 1][0], mods_l[i + 1][1]) if with_ctx else None
        res = post_norm_fused(x_lat, f_all, 0, gaf_l, ln_g[i, 1], ln_b[i, 1], t_lat, nxt=nxt_l)
        x_lat = res[0]
        if with_ctx:
            h_lat = res[1].reshape(nb, t_lat, D_MODEL)
            x_ctx, h_ctx = post_norm_fused(x_ctx, f_all, n_lat, gaf_c, ln_g[i, 1], ln_b[i, 1], n_ctx,
                                           nxt=(mods_c[i + 1][0], mods_c[i + 1][1]))
            h_ctx = h_ctx.reshape(nb, t_ctx, D_MODEL)
    return x_lat.reshape(x.shape)
```

```python
import jax
import jax.numpy as jnp
from jax import lax
from jax.experimental import pallas as pl
from jax.experimental.pallas import tpu as pltpu

D_MODEL = 2048
DEPTH = 2
F32 = jnp.float32
BF16 = jnp.bfloat16
GRID_W = 64

M_HEADS = 4
M_DIM = D_MODEL // 2
M_HEAD_DIM = M_DIM // M_HEADS
M_CHUNK = 128
M_CONV = 3
P_DIM = D_MODEL - M_DIM
P_GROUPS = 4
P_GROUP_DIM = P_DIM // P_GROUPS
POOL_WINDOWS = (2, 4, 8, 16)
A_HEADS = 16
Q_LORA = 1536
KV_LORA = 512
NOPE = 128
ROPE = 64
V_DIM = 128
ROPE_THETA = 10000.0
N_EXPERTS = 16
EC_FACTOR = 2
LN_EPS = 1e-5
RMS_EPS = 1e-6
ALPHA = (2 * DEPTH) ** 0.25

VMEM_LIMIT_BYTES = 56 * 1024 * 1024


def _params(n_axes):
    return pltpu.CompilerParams(dimension_semantics=("arbitrary",) * n_axes,
                                vmem_limit_bytes=VMEM_LIMIT_BYTES)


def _mm_kernel(x_ref, w_ref, o_ref):
    o_ref[...] = jnp.dot(x_ref[...].astype(BF16), w_ref[...].astype(BF16),
                         preferred_element_type=F32).astype(o_ref.dtype)


def _pick_tile(n, prefs):
    for t in prefs:
        if n % t == 0:
            return t
    return n


def matmul(x, w, out_dtype=F32):
    if x.ndim == 2:
        return _matmul_grouped(x[None], w[None], out_dtype)[0]
    return _matmul_grouped(x, w, out_dtype)


def _matmul_grouped(x, w, out_dtype):
    g, m, k = x.shape
    n = w.shape[-1]
    tm = _pick_tile(m, (1024, 512, 256, 128))
    tn = _pick_tile(n, (512, 384, 256, 128))
    return pl.pallas_call(
        _mm_kernel,
        grid=(g, n // tn, m // tm),
        in_specs=[pl.BlockSpec((None, tm, k), lambda e, j, i: (e, i, 0)),
                  pl.BlockSpec((None, k, tn), lambda e, j, i: (e, 0, j))],
        out_specs=pl.BlockSpec((None, tm, tn), lambda e, j, i: (e, i, j)),
        out_shape=jax.ShapeDtypeStruct((g, m, n), out_dtype),
        compiler_params=_params(3),
        name="matmul",
    )(x, w)


def dense(x, w, splits=None):
    b, t, k = x.shape
    x2 = x.reshape(b * t, k)
    if splits is None:
        return matmul(x2, w).reshape(b, t, w.shape[-1])
    outs, lo = [], 0
    for width in splits:
        outs.append(matmul(x2, w[:, lo:lo + width]).reshape(b, t, width))
        lo += width
    return outs


def adaln(cond, w, b):
    m = jnp.dot(jax.nn.silu(cond), w, precision=lax.Precision.HIGHEST) + b
    m = m.reshape(cond.shape[0], 6, 1, D_MODEL)
    return [m[:, k] for k in range(6)]


def modulate(x, shift, scale):
    return x * (1.0 + scale) + shift


NORM_ROWS = 256
HALF_D = D_MODEL // 2


def _pack_bf16_pairs(h):
    bits = pltpu.bitcast(h.astype(BF16).astype(F32), jnp.uint32)
    return (bits[:, :HALF_D] >> 16) | (bits[:, HALF_D:] & jnp.uint32(0xFFFF0000))


def _make_post_norm_kernel(with_ffn, with_next):
    def body(*refs):
        x_ref, y_ref, gate_ref, g_ref, b_ref = refs[:5]
        pos = 5
        if with_ffn:
            shf_ref, scf_ref, wr_ref = refs[pos:pos + 3]
            pos += 3
        if with_next:
            shn_ref, scn_ref = refs[pos:pos + 2]
            pos += 2
        outs = refs[pos:]
        z = ALPHA * x_ref[...] + (1.0 + gate_ref[...]) * y_ref[...]
        mu = jnp.mean(z, axis=-1, keepdims=True)
        zc = z - mu
        var = jnp.mean(zc * zc, axis=-1, keepdims=True)
        xn = zc * lax.rsqrt(var + LN_EPS) * g_ref[...] + b_ref[...]
        outs[0][...] = xn
        k = 1
        if with_ffn:
            h = xn * (1.0 + scf_ref[...]) + shf_ref[...]
            outs[k][...] = _pack_bf16_pairs(h)
            outs[k + 1][...] = jnp.dot(h, wr_ref[...], preferred_element_type=F32,
                                       precision=lax.Precision.HIGHEST)
            k += 2
        if with_next:
            outs[k][...] = (xn * (1.0 + scn_ref[...]) + shn_ref[...]).astype(BF16)
    return body


def post_norm_fused(x, y, y_row0, gate, ln_g, ln_b, rows_per_set, ffn=None, nxt=None):
    n = x.shape[0]
    tps = rows_per_set // NORM_ROWS
    off = y_row0 // NORM_ROWS
    row = pl.BlockSpec((NORM_ROWS, D_MODEL), lambda i: (i, 0))
    tab = pl.BlockSpec((None, 1, D_MODEL), lambda i: (i // tps, 0, 0))
    one = pl.BlockSpec((None, 1, D_MODEL), lambda i: (0, 0, 0))
    args = [x, y, gate, ln_g.reshape(1, 1, D_MODEL), ln_b.reshape(1, 1, D_MODEL)]
    in_specs = [row, pl.BlockSpec((NORM_ROWS, D_MODEL), lambda i: (i + off, 0)), tab, one, one]
    out_shape = [jax.ShapeDtypeStruct((n, D_MODEL), F32)]
    out_specs = [row]
    if ffn is not None:
        args += [ffn[0], ffn[1], ffn[2]]
        in_specs += [tab, tab, pl.BlockSpec((D_MODEL, N_EXPERTS), lambda i: (0, 0))]
        out_shape += [jax.ShapeDtypeStruct((n, HALF_D), jnp.uint32), jax.ShapeDtypeStruct((n, N_EXPERTS), F32)]
        out_specs += [pl.BlockSpec((NORM_ROWS, HALF_D), lambda i: (i, 0)),
                      pl.BlockSpec((NORM_ROWS, N_EXPERTS), lambda i: (i, 0))]
    if nxt is not None:
        args += [nxt[0], nxt[1]]
        in_specs += [tab, tab]
        out_shape += [jax.ShapeDtypeStruct((n, D_MODEL), BF16)]
        out_specs += [row]
    return pl.pallas_call(
        _make_post_norm_kernel(ffn is not None, nxt is not None),
        grid=(n // NORM_ROWS,),
        in_specs=in_specs, out_specs=out_specs, out_shape=out_shape,
        compiler_params=_params(1),
        name="post_norm",
    )(*args)


def depthwise_conv_centred(u, w, b):
    up = jnp.pad(u, ((0, 0), (M_CONV // 2, M_CONV // 2), (0, 0)))
    t = u.shape[1]
    out = sum(up[:, j:j + t] * w[j] for j in range(M_CONV))
    return out + b


def ab_project(h, w_in, conv_w, conv_b, gate_b):
    p, pg, u = dense(h, w_in, splits=(4 * M_DIM, 4 * M_HEADS, P_DIM))
    qk = jax.nn.silu(depthwise_conv_centred(p[..., :2 * M_DIM], conv_w, conv_b))
    kscale = jnp.concatenate([jnp.ones((M_DIM,), F32), jnp.full((M_DIM,), M_HEAD_DIM ** -0.5, F32)])
    qk = (qk * kscale).astype(BF16)
    v = p[..., 2 * M_DIM:3 * M_DIM].astype(BF16)
    gates = (pg + gate_b).reshape(p.shape[:2] + (2, 2, M_HEADS))
    li = gates[..., 0, :]
    lf = jax.nn.log_sigmoid(gates[..., 1, :])
    grow = jnp.stack([li, lf], 3).reshape(p.shape[:2] + (4 * M_HEADS,))
    return qk, v, p, grow, u


def _mlstm_chain(q, k, v, li_row, lf_row, c_ref, n_ref, m_ref, reverse):
    L = M_CHUNK
    ti = lax.broadcasted_iota(jnp.int32, (L, L), 0)
    si = lax.broadcasted_iota(jnp.int32, (L, L), 1)
    mask = (si >= ti) if reverse else (si <= ti)
    mask_t = (ti >= si) if reverse else (ti <= si)
    lf_b = jnp.broadcast_to(lf_row, (L, L))
    lf_bt = lf_b.T
    li_col = jnp.broadcast_to(li_row, (L, L)).T[:, 0:1]
    b_col = jnp.sum(jnp.where(mask, lf_b, 0.0), axis=1, keepdims=True)
    b_row = jnp.sum(jnp.where(mask_t, lf_bt, 0.0), axis=0, keepdims=True)
    a_row = li_row - b_row
    dmat = jnp.where(mask, b_col + a_row, -jnp.inf)
    m_prev = m_ref[...][:, 0:1]
    inter = b_col + m_prev
    m_t = jnp.maximum(inter, jnp.max(dmat, axis=1, keepdims=True))
    w_intra = jnp.exp(dmat - m_t)
    w_inter = jnp.exp(inter - m_t)
    s = lax.dot_general(q, k, (((1,), (1,)), ((), ())), preferred_element_type=F32) * w_intra
    c_prev = c_ref[...]
    n_prev = n_ref[...]
    num = (w_inter * jnp.dot(q, c_prev.astype(BF16), preferred_element_type=F32)
           + jnp.dot(s.astype(BF16), v, preferred_element_type=F32))
    den = (w_inter * jnp.sum(q.astype(F32) * n_prev, axis=1, keepdims=True)
           + jnp.sum(s, axis=1, keepdims=True))
    h = num / jnp.maximum(jnp.abs(den), jnp.exp(-m_t))
    bl = jnp.sum(lf_row, axis=1, keepdims=True)
    m_new = jnp.maximum(bl + m_prev, jnp.max(bl + a_row, axis=1, keepdims=True))
    decay = jnp.exp(bl + m_prev - m_new)
    wg_col = jnp.exp(bl - b_col + li_col - m_new)
    kw = k.astype(F32) * wg_col
    c_ref[...] = decay * c_prev + jnp.dot(kw.T.astype(BF16), v, preferred_element_type=F32)
    n_ref[...] = decay * n_prev + jnp.sum(kw, axis=0, keepdims=True)
    m_ref[...] = jnp.broadcast_to(m_new, m_ref.shape)
    return h


def _mlstm_kernel(qf_ref, kf_ref, vf_ref, gf_ref, qb_ref, kb_ref, vb_ref, gb_ref,
                  hf_ref, hb_ref, c_scr, n_scr, m_scr):
    @pl.when(pl.program_id(1) == 0)
    def _():
        c_scr[...] = jnp.zeros_like(c_scr)
        n_scr[...] = jnp.zeros_like(n_scr)
        m_scr[...] = jnp.zeros_like(m_scr)

    dirs = ((qf_ref, kf_ref, vf_ref, gf_ref, hf_ref), (qb_ref, kb_ref, vb_ref, gb_ref, hb_ref))
    for d, (q_ref, k_ref, v_ref, g_ref, h_ref) in enumerate(dirs):
        for hd in range(M_HEADS):
            cols = slice(hd * M_HEAD_DIM, (hd + 1) * M_HEAD_DIM)
            row_i = d * 2 * M_HEADS + hd
            row_f = row_i + M_HEADS
            h = _mlstm_chain(q_ref[:, cols], k_ref[:, cols], v_ref[:, cols],
                             g_ref[row_i:row_i + 1, :], g_ref[row_f:row_f + 1, :],
                             c_scr.at[d, hd], n_scr.at[d, hd], m_scr.at[d, hd], reverse=(d == 1))
            h_ref[:, cols] = h.astype(h_ref.dtype)


def mlstm_scan(qk, v, grow, n_ctx_chunks):
    nb, tt, _ = v.shape
    nc = tt // M_CHUNK
    g = grow.reshape(nb, nc, M_CHUNK, 4 * M_HEADS).transpose(0, 1, 3, 2)

    def fwd(c):
        return c

    def bwd(c):
        return jnp.where(c < n_ctx_chunks, n_ctx_chunks - 1 - c, nc + n_ctx_chunks - 1 - c)

    def specs(cm):
        return [pl.BlockSpec((None, M_CHUNK, M_DIM), lambda b, c: (b, cm(c), 0)),
                pl.BlockSpec((None, M_CHUNK, M_DIM), lambda b, c: (b, cm(c), 1)),
                pl.BlockSpec((None, M_CHUNK, M_DIM), lambda b, c: (b, cm(c), 0)),
                pl.BlockSpec((None, None, 4 * M_HEADS, M_CHUNK), lambda b, c: (b, cm(c), 0, 0))]

    out_sds = jax.ShapeDtypeStruct((nb, tt, M_DIM), F32)
    return pl.pallas_call(
        _mlstm_kernel,
        grid=(nb, nc),
        in_specs=specs(fwd) + specs(bwd),
        out_specs=[pl.BlockSpec((None, M_CHUNK, M_DIM), lambda b, c: (b, fwd(c), 0)),
                   pl.BlockSpec((None, M_CHUNK, M_DIM), lambda b, c: (b, bwd(c), 0))],
        out_shape=[out_sds, out_sds],
        scratch_shapes=[pltpu.VMEM((2, M_HEADS, M_HEAD_DIM, M_HEAD_DIM), F32),
                        pltpu.VMEM((2, M_HEADS, 1, M_HEAD_DIM), F32),
                        pltpu.VMEM((2, M_HEADS, 1, 128), F32)],
        compiler_params=_params(2),
        name="mlstm_scan",
    )(qk, qk, v, g, qk, qk, v, g)


POST_ROWS = 256
POOL_HALO = 8


def _mixer_a_post_kernel(hf_ref, hb_ref, o_ref, u_ref, uprev_ref, unext_ref, hg_ref, pw_ref, ps_ref, out_ref):
    i = pl.program_id(1)
    n_tiles = pl.num_programs(1)
    t_seq = n_tiles * POST_ROWS
    hs = hf_ref[...] + hb_ref[...]
    gate = jax.nn.sigmoid(o_ref[...])
    for hd in range(M_HEADS):
        cols = slice(hd * M_HEAD_DIM, (hd + 1) * M_HEAD_DIM)
        x = hs[:, cols]
        xc = x - jnp.mean(x, axis=-1, keepdims=True)
        var = jnp.mean(xc * xc, axis=-1, keepdims=True)
        out_ref[:, cols] = (xc * lax.rsqrt(var + LN_EPS) * hg_ref[:, cols] * gate[:, cols]).astype(out_ref.dtype)
    u = u_ref[...]
    prev = jnp.where(i == 0, 0.0, uprev_ref[...])
    nxt = jnp.where(i == n_tiles - 1, 0.0, unext_ref[...])
    ext = jnp.concatenate([prev, u, nxt], axis=0)
    n_ext = POST_ROWS + 2 * POOL_HALO
    t = i * POST_ROWS + lax.broadcasted_iota(jnp.int32, (POST_ROWS, 1), 0)
    for g, w in enumerate(POOL_WINDOWS):
        cols = slice(g * P_GROUP_DIM, (g + 1) * P_GROUP_DIM)
        a = ext[:, cols]
        a = a + pltpu.roll(a, 1, axis=0)
        step = 1
        while 2 * step < w:
            a = pltpu.roll(a, step, axis=0) + pltpu.roll(a, n_ext - step, axis=0)
            step *= 2
        count = (jnp.minimum(t + w // 2, t_seq) - jnp.maximum(t - w // 2, 0)).astype(F32)
        pooled = a[POOL_HALO:POOL_HALO + POST_ROWS] / count - u[:, cols]
        y = jnp.dot(pooled.astype(BF16), pw_ref[g].astype(BF16), preferred_element_type=F32)
        out_ref[:, M_DIM + g * P_GROUP_DIM:M_DIM + (g + 1) * P_GROUP_DIM] = (y * ps_ref[:, cols]).astype(out_ref.dtype)


def mixer_a_post(h_f, h_b, h_row0, p, u, head_g, pool_w, pool_s):
    nb, t, _ = u.shape
    nt = t // POST_ROWS
    off = h_row0 // POST_ROWS
    per = POST_ROWS // POOL_HALO
    return pl.pallas_call(
        _mixer_a_post_kernel,
        grid=(nb, nt),
        in_specs=[pl.BlockSpec((None, POST_ROWS, M_DIM), lambda b, i: (b, i + off, 0)),
                  pl.BlockSpec((None, POST_ROWS, M_DIM), lambda b, i: (b, i + off, 0)),
                  pl.BlockSpec((None, POST_ROWS, M_DIM), lambda b, i: (b, i, 3)),
                  pl.BlockSpec((None, POST_ROWS, P_DIM), lambda b, i: (b, i, 0)),
                  pl.BlockSpec((None, POOL_HALO, P_DIM), lambda b, i: (b, jnp.maximum(i * per - 1, 0), 0)),
                  pl.BlockSpec((None, POOL_HALO, P_DIM), lambda b, i: (b, jnp.minimum((i + 1) * per, nt * per - 1), 0)),
                  pl.BlockSpec((1, M_DIM), lambda b, i: (0, 0)),
                  pl.BlockSpec((P_GROUPS, P_GROUP_DIM, P_GROUP_DIM), lambda b, i: (0, 0, 0)),
                  pl.BlockSpec((1, P_DIM), lambda b, i: (0, 0))],
        out_specs=pl.BlockSpec((None, POST_ROWS, D_MODEL), lambda b, i: (b, i, 0)),
        out_shape=jax.ShapeDtypeStruct((nb, t, D_MODEL), BF16),
        compiler_params=_params(2),
        name="mixer_a_post",
    )(h_f, h_b, p, u, u, u, head_g.reshape(1, M_DIM), pool_w, pool_s.reshape(1, P_DIM))


def mixer_mlstm_pool(h_lat, h_ctx, w_in, conv_w, conv_b, gate_b, head_g, pool_w, pool_s, w_out, with_ctx_out):
    t_ctx = h_ctx.shape[1]
    qk_c, v_c, pc, g_c, uc = ab_project(h_ctx, w_in, conv_w, conv_b, gate_b)
    qk_l, v_l, pl_, g_l, ul = ab_project(h_lat, w_in, conv_w, conv_b, gate_b)
    h_f, h_b = mlstm_scan(jnp.concatenate([qk_c, qk_l], 1), jnp.concatenate([v_c, v_l], 1),
                          jnp.concatenate([g_c, g_l], 1), t_ctx // M_CHUNK)
    y_lat = dense(mixer_a_post(h_f, h_b, t_ctx, pl_, ul, head_g, pool_w, pool_s), w_out)
    if not with_ctx_out:
        return y_lat, None
    y_ctx = dense(mixer_a_post(h_f, h_b, 0, pc, uc, head_g, pool_w, pool_s), w_out)
    return y_lat, y_ctx


def axial_rope_tables(n_tokens):
    rows = n_tokens // GRID_W
    row = jnp.repeat(jnp.arange(rows), GRID_W)
    col = jnp.tile(jnp.arange(GRID_W), rows)
    half = ROPE // 2
    inv = ROPE_THETA ** (-jnp.arange(0, half, 2, dtype=F32) / half)
    ang = jnp.stack([row[:, None] * inv, col[:, None] * inv], 1)
    return jnp.cos(ang), jnp.sin(ang)


def apply_axial_rope(x, cos, sin):
    xs = x.reshape(x.shape[:-1] + (2, 2, ROPE // 4))
    x1, x2 = xs[..., 0, :], xs[..., 1, :]
    out = jnp.stack([x1 * cos - x2 * sin, x2 * cos + x1 * sin], -2)
    return out.reshape(x.shape)


LOG2_E = 1.4426950408889634
Q_SCALE = (NOPE + ROPE) ** -0.5 * LOG2_E
ATTN_KEY_SPLITS = 4


def _rms_norm_rows(x, g):
    return x * lax.rsqrt(jnp.mean(jnp.square(x), axis=-1, keepdims=True) + RMS_EPS) * g


def _q_proj_kernel(x_ref, g_ref, w_ref, tab_ref, o_ref, xn_scr):
    @pl.when(pl.program_id(1) == 0)
    def _():
        xn_scr[...] = _rms_norm_rows(x_ref[...], g_ref[...]).astype(BF16)

    o = jnp.dot(xn_scr[...], w_ref[...], preferred_element_type=F32)
    r = o[:, NOPE:] * tab_ref[...]
    rot = r + pltpu.roll(r, ROPE, axis=1)
    o_ref[...] = (jnp.concatenate([o[:, :NOPE], rot[:, :ROPE]], axis=1) * Q_SCALE).astype(o_ref.dtype)


def q_proj(pq, q_norm_g, w_uq, tab):
    nb, t, _ = pq.shape
    w = w_uq.reshape(Q_LORA, A_HEADS, NOPE + ROPE)
    wpe = w[..., NOPE:].reshape(Q_LORA, A_HEADS, 2, 2, ROPE // 4)
    wsw = jnp.stack([-wpe[..., 1, :], wpe[..., 0, :]], -2).reshape(Q_LORA, A_HEADS, ROPE)
    wh = jnp.concatenate([w, wsw], -1).transpose(1, 0, 2).astype(BF16)
    tm = _pick_tile(t, (1024, 512, 256, 128))
    nt = t // tm
    return pl.pallas_call(
        _q_proj_kernel,
        grid=(nb * nt, A_HEADS),
        in_specs=[pl.BlockSpec((tm, Q_LORA), lambda i, h: (i, 0)),
                  pl.BlockSpec((1, Q_LORA), lambda i, h: (0, 0)),
                  pl.BlockSpec((None, Q_LORA, NOPE + 2 * ROPE), lambda i, h: (h, 0, 0)),
                  pl.BlockSpec((tm, 2 * ROPE), lambda i, h: (i % nt, 0))],
        out_specs=pl.BlockSpec((None, None, tm, NOPE + ROPE), lambda i, h: (i // nt, h, i % nt, 0)),
        out_shape=jax.ShapeDtypeStruct((nb, A_HEADS, t, NOPE + ROPE), BF16),
        scratch_shapes=[pltpu.VMEM((tm, Q_LORA), BF16)],
        compiler_params=_params(2),
        name="mla_q_proj",
    )(pq.reshape(nb * t, Q_LORA), q_norm_g.reshape(1, Q_LORA), wh, tab)


def _kv_proj_kernel(x_ref, g_ref, w_ref, kpe_ref, k_ref, v_ref, xn_scr):
    @pl.when(pl.program_id(1) == 0)
    def _():
        xn_scr[...] = _rms_norm_rows(x_ref[...], g_ref[...]).astype(BF16)

    o = jnp.dot(xn_scr[...], w_ref[...], preferred_element_type=F32)
    k_ref[...] = jnp.concatenate([o[:, :NOPE], kpe_ref[...]], axis=1).astype(k_ref.dtype)
    lane = lax.broadcasted_iota(jnp.int32, (o.shape[0], V_DIM), 1)
    ones_col = jnp.where(lane == 0, 1.0, 0.0)
    v_ref[...] = jnp.concatenate([o[:, NOPE:], ones_col], axis=1).astype(v_ref.dtype)


def kv_proj(pkv, kpe, kv_norm_g, w_ukv):
    nb, tk, _ = pkv.shape
    wh = w_ukv.reshape(KV_LORA, A_HEADS, NOPE + V_DIM).transpose(1, 0, 2).astype(BF16)
    tm = _pick_tile(tk, (2176, 1024, 512, 256, 128))
    nt = tk // tm
    return pl.pallas_call(
        _kv_proj_kernel,
        grid=(nb * nt, A_HEADS),
        in_specs=[pl.BlockSpec((tm, KV_LORA), lambda i, h: (i, 0)),
                  pl.BlockSpec((1, KV_LORA), lambda i, h: (0, 0)),
                  pl.BlockSpec((None, KV_LORA, NOPE + V_DIM), lambda i, h: (h, 0, 0)),
                  pl.BlockSpec((tm, ROPE), lambda i, h: (i, 0))],
        out_specs=[pl.BlockSpec((None, None, tm, NOPE + ROPE), lambda i, h: (i // nt, h, i % nt, 0)),
                   pl.BlockSpec((None, None, tm, 2 * V_DIM), lambda i, h: (i // nt, h, i % nt, 0))],
        out_shape=[jax.ShapeDtypeStruct((nb, A_HEADS, tk, NOPE + ROPE), BF16),
                   jax.ShapeDtypeStruct((nb, A_HEADS, tk, 2 * V_DIM), BF16)],
        scratch_shapes=[pltpu.VMEM((tm, KV_LORA), BF16)],
        compiler_params=_params(2),
        name="mla_kv_proj",
    )(pkv.reshape(nb * tk, KV_LORA), kv_norm_g.reshape(1, KV_LORA), wh, kpe.reshape(nb * tk, ROPE))


def _attn_kernel(q_ref, k_ref, v_ref, o_ref):
    q = q_ref[...]
    n_lanes = k_ref.shape[0] // 128
    sizes = [(n_lanes // ATTN_KEY_SPLITS + (c < n_lanes % ATTN_KEY_SPLITS)) * 128 for c in range(ATTN_KEY_SPLITS)]
    m = o = None
    lo = 0
    for size in filter(None, sizes):
        ks = slice(lo, lo + size)
        lo += size
        s = lax.dot_general(q, k_ref[ks, :], (((1,), (1,)), ((), ())), preferred_element_type=F32)
        mc = jnp.max(s, axis=-1, keepdims=True)
        m_new = mc if m is None else jnp.maximum(m, mc)
        p = jnp.exp2((s - m_new).astype(BF16))
        oc = jnp.dot(p, v_ref[ks, :], preferred_element_type=F32)
        o = oc if o is None else o * jnp.exp2(m - m_new) + oc
        m = m_new
    o_ref[...] = (o[:, :V_DIM] / o[:, V_DIM:V_DIM + 1]).astype(o_ref.dtype)


def attend(q, k, vh):
    nb, _, tq, _ = q.shape
    tk = k.shape[2]
    bq = _pick_tile(tq, (512, 256, 128))
    return pl.pallas_call(
        _attn_kernel,
        grid=(nb, A_HEADS, tq // bq),
        in_specs=[pl.BlockSpec((None, None, bq, NOPE + ROPE), lambda b, h, i: (b, h, i, 0)),
                  pl.BlockSpec((None, None, tk, NOPE + ROPE), lambda b, h, i: (b, h, 0, 0)),
                  pl.BlockSpec((None, None, tk, 2 * V_DIM), lambda b, h, i: (b, h, 0, 0))],
        out_specs=pl.BlockSpec((None, bq, V_DIM), lambda b, h, i: (b, i, h)),
        out_shape=jax.ShapeDtypeStruct((nb, tq, A_HEADS * V_DIM), BF16),
        compiler_params=_params(3),
        name="mla_attention",
    )(q, k, vh)


def mixer_mla(h_lat, h_ctx, w_in, q_norm_g, kv_norm_g, w_uq, w_ukv, w_o, with_ctx_out):
    t_lat, t_ctx = h_lat.shape[1], h_ctx.shape[1]
    cos, sin = axial_rope_tables(t_lat)
    cos_full = jnp.repeat(cos, 2, axis=1).reshape(t_lat, ROPE)
    sin_full = jnp.repeat(sin, 2, axis=1).reshape(t_lat, ROPE)
    pq_l, pkv_l, pr_l = dense(h_lat, w_in, splits=(Q_LORA, KV_LORA, ROPE))
    if with_ctx_out:
        pq_c, pkv_c, pr_c = dense(h_ctx, w_in, splits=(Q_LORA, KV_LORA, ROPE))
    else:
        pkv_c, pr_c = dense(h_ctx, w_in[:, Q_LORA:], splits=(KV_LORA, ROPE))
    kpe = jnp.concatenate([pr_c, apply_axial_rope(pr_l, cos, sin)], 1)
    k, vh = kv_proj(jnp.concatenate([pkv_c, pkv_l], 1), kpe, kv_norm_g, w_ukv)
    q_l = q_proj(pq_l, q_norm_g, w_uq, jnp.concatenate([cos_full, sin_full], -1))
    y_lat = dense(attend(q_l, k, vh), w_o)
    if not with_ctx_out:
        return y_lat, None
    no_rot = jnp.concatenate([jnp.ones((t_ctx, ROPE), F32), jnp.zeros((t_ctx, ROPE), F32)], -1)
    q_c = q_proj(pq_c, q_norm_g, w_uq, no_rot)
    y_ctx = dense(attend(q_c, k[:, :, :t_ctx], vh[:, :, :t_ctx]), w_o)
    return y_lat, y_ctx


MOE_TF = 256
MOE_TN = 256
MOE_NF = D_MODEL // MOE_TF
MOE_NN = D_MODEL // MOE_TN


def _moe_kernel(rows_ref, h_hbm, gate_ref, w1_ref, w3_ref, w2_ref, y_ref, xg, xb, hmid, sem):
    e = pl.program_id(0)
    j = pl.program_id(1)
    m = xg.shape[0]
    per_step = m // MOE_NF

    def issue(expert, lo, n):
        def body(i, carry):
            r = rows_ref[expert * m + lo + i]
            pltpu.make_async_copy(h_hbm.at[pl.ds(r, 1), :], xg.at[pl.ds(lo + i, 1), :], sem.at[0]).start()
            return carry
        lax.fori_loop(0, n, body, 0, unroll=8)

    @pl.when((e == 0) & (j == 0))
    def _():
        issue(0, 0, m)

    @pl.when(j == 0)
    def _():
        pltpu.make_async_copy(xg, xg, sem.at[0]).wait()
        w = xg[...]
        xb[:, :HALF_D] = pltpu.bitcast(w << 16, F32).astype(BF16)
        xb[:, HALF_D:] = pltpu.bitcast(w & jnp.uint32(0xFFFF0000), F32).astype(BF16)

    @pl.when((j >= 1) & (j <= MOE_NF) & (e + 1 < pl.num_programs(0)))
    def _():
        issue(e + 1, (j - 1) * per_step, per_step)

    @pl.when(j < MOE_NF)
    def _():
        x = xb[...]
        a = jnp.dot(x, w1_ref[...].astype(BF16), preferred_element_type=F32)
        g = jnp.dot(x, w3_ref[...].astype(BF16), preferred_element_type=F32)
        hmid[j] = (jax.nn.silu(a) * g).astype(BF16)

    @pl.when(j >= MOE_NF)
    def _():
        acc = jnp.dot(hmid[0], w2_ref[0:MOE_TF, :].astype(BF16), preferred_element_type=F32)
        for f in range(1, MOE_NF):
            acc += jnp.dot(hmid[f], w2_ref[f * MOE_TF:(f + 1) * MOE_TF, :].astype(BF16),
                           preferred_element_type=F32)
        y_ref[...] = acc * gate_ref[...]


def expert_ffn_rows(h_packed, rows, gates, w1, w3, w2, layer):
    n_e, m = rows.shape
    return pl.pallas_call(
        _moe_kernel,
        grid_spec=pltpu.PrefetchScalarGridSpec(
            num_scalar_prefetch=1,
            grid=(n_e, MOE_NF + MOE_NN),
            in_specs=[pl.BlockSpec(memory_space=pl.ANY),
                      pl.BlockSpec((None, m, 1), lambda e, j, r: (e, 0, 0)),
                      pl.BlockSpec((None, None, D_MODEL, MOE_TF),
                                   lambda e, j, r: (layer, e, 0, jnp.minimum(j, MOE_NF - 1))),
                      pl.BlockSpec((None, None, D_MODEL, MOE_TF),
                                   lambda e, j, r: (layer, e, 0, jnp.minimum(j, MOE_NF - 1))),
                      pl.BlockSpec((None, None, D_MODEL, MOE_TN),
                                   lambda e, j, r: (layer, e, 0, jnp.maximum(j - MOE_NF, 0)))],
            out_specs=pl.BlockSpec((None, m, MOE_TN), lambda e, j, r: (e, 0, jnp.maximum(j - MOE_NF, 0))),
            scratch_shapes=[pltpu.VMEM((m, HALF_D), jnp.uint32),
                            pltpu.VMEM((m, D_MODEL), BF16),
                            pltpu.VMEM((MOE_NF, m, MOE_TF), BF16),
                            pltpu.SemaphoreType.DMA((1,))]),
        out_shape=jax.ShapeDtypeStruct((n_e, m, D_MODEL), F32),
        compiler_params=_params(2),
        name="expert_ffn",
    )(rows.reshape(-1), h_packed, gates.reshape(n_e, m, 1), w1, w3, w2)


def route(logits, nb):
    t = logits.shape[0] // nb
    cap = EC_FACTOR * t // N_EXPERTS
    aff = jax.nn.softmax(logits.reshape(nb, t, N_EXPERTS), axis=-1)
    return lax.top_k(jnp.swapaxes(aff, 1, 2), cap)


def expert_choice_ffn(sets, nb, w1, w3, w2, layer):
    rows, gates, base = [], [], 0
    for _, logits in sets:
        gate, idx = route(logits, nb)
        t = logits.shape[0] // nb
        rid = base + jnp.arange(nb, dtype=jnp.int32)[:, None, None] * t + idx
        rows.append(rid.transpose(1, 0, 2).reshape(N_EXPERTS, -1))
        gates.append(gate.transpose(1, 0, 2).reshape(N_EXPERTS, -1))
        base += nb * t
    rows = jnp.concatenate(rows, 1)
    h_packed = jnp.concatenate([hp for hp, _ in sets], 0) if len(sets) > 1 else sets[0][0]
    y = expert_ffn_rows(h_packed, rows, jnp.concatenate(gates, 1), w1, w3, w2, layer)
    return jnp.zeros((base, D_MODEL), F32).at[rows.reshape(-1)].add(y.reshape(-1, D_MODEL))


def kernel(x, c, ctx, c_ctx, ada_w, ada_b, ln_g, ln_b, ab_w_in, ab_conv_w, ab_conv_b, ab_gate_b,
           ab_head_g, ab_pool_w, ab_pool_s, ab_w_out, mla_w_in, mla_q_norm_g, mla_kv_norm_g,
           mla_w_uq, mla_w_ukv, mla_w_o, moe_router, moe_w1, moe_w3, moe_w2):
    nb, t_lat, _ = x.shape
    t_ctx = ctx.shape[1]
    n_lat, n_ctx = nb * t_lat, nb * t_ctx
    cond = jnp.concatenate([c, c_ctx[None]], 0)
    mods = [adaln(cond, ada_w[i], ada_b[i]) for i in range(DEPTH)]
    mods_l = [[m[:nb] for m in ms] for ms in mods]
    mods_c = [[m[nb:] for m in ms] for ms in mods]
    x_lat, x_ctx = x.reshape(n_lat, D_MODEL), ctx.reshape(n_ctx, D_MODEL)
    h_lat = modulate(x, mods_l[0][0], mods_l[0][1]).astype(BF16)
    h_ctx = modulate(ctx, mods_c[0][0], mods_c[0][1]).astype(BF16)
    for i in range(DEPTH):
        with_ctx = i < DEPTH - 1
        j = i // 2
        _, _, ga_l, shf_l, scf_l, gaf_l = mods_l[i]
        _, _, ga_c, shf_c, scf_c, gaf_c = mods_c[i]
        if i % 2 == 0:
            y_lat, y_ctx = mixer_mlstm_pool(h_lat, h_ctx, ab_w_in[j], ab_conv_w[j], ab_conv_b[j], ab_gate_b[j],
                                            ab_head_g[j], ab_pool_w[j], ab_pool_s[j], ab_w_out[j], with_ctx)
        else:
            y_lat, y_ctx = mixer_mla(h_lat, h_ctx, mla_w_in[j], mla_q_norm_g[j], mla_kv_norm_g[j],
                                     mla_w_uq[j], mla_w_ukv[j], mla_w_o[j], with_ctx)
        x_lat, hp_l, lg_l = post_norm_fused(x_lat, y_lat.reshape(n_lat, D_MODEL), 0, ga_l, ln_g[i, 0], ln_b[i, 0],
                                            t_lat, ffn=(shf_l, scf_l, moe_router[i]))
        sets = [(hp_l, lg_l)]
        if with_ctx:
            x_ctx, hp_c, lg_c = post_norm_fused(x_ctx, y_ctx.reshape(n_ctx, D_MODEL), 0, ga_c, ln_g[i, 0],
                                                ln_b[i, 0], n_ctx, ffn=(shf_c, scf_c, moe_router[i]))
            sets.append((hp_c, lg_c))
        f_all = expert_choice_ffn(sets, nb, moe_w1, moe_w3, moe_w2, i)
        nxt_l = (mods_l[i + 1][0], mods_l[i + 1][1]) if with_ctx else None
        res = post_norm_fused(x_lat, f_all, 0, gaf_l, ln_g[i, 1], ln_b[i, 1], t_lat, nxt=nxt_l)
        x_lat = res[0]
        if with_ctx:
            h_lat = res[1].reshape(nb, t_lat, D_MODEL)
            x_ctx, h_ctx = post_norm_fused(x_ctx, f_all, n_lat, gaf_c, ln_g[i, 1], ln_b[i, 1], n_ctx,
                                           nxt=(mods_c[i + 1][0], mods_c[i + 1][1]))
            h_ctx = h_ctx.reshape(nb, t_ctx, D_MODEL)
    return x_lat.reshape(x.shape)
```

```python
import jax
import jax.numpy as jnp
from jax import lax
from jax.experimental import pallas as pl
from jax.experimental.pallas import tpu as pltpu

D_MODEL = 2048
DEPTH = 2
F32 = jnp.float32
BF16 = jnp.bfloat16
GRID_W = 64

M_HEADS = 4
M_DIM = D_MODEL // 2
M_HEAD_DIM = M_DIM // M_HEADS
M_CHUNK = 128
M_CONV = 3
P_DIM = D_MODEL - M_DIM
P_GROUPS = 4
P_GROUP_DIM = P_DIM // P_GROUPS
POOL_WINDOWS = (2, 4, 8, 16)
A_HEADS = 16
Q_LORA = 1536
KV_LORA = 512
NOPE = 128
ROPE = 64
V_DIM = 128
ROPE_THETA = 10000.0
N_EXPERTS = 16
EC_FACTOR = 2
LN_EPS = 1e-5
RMS_EPS = 1e-6
ALPHA = (2 * DEPTH) ** 0.25

V7X_VMEM_BYTES = 64 * 1024 * 1024
VMEM_LIMIT_BYTES = V7X_VMEM_BYTES - 6 * 1024 * 1024


def _params(n_axes):
    return pltpu.CompilerParams(dimension_semantics=("arbitrary",) * n_axes,
                                vmem_limit_bytes=VMEM_LIMIT_BYTES)


def _mm_kernel(x_ref, w_ref, o_ref):
    o_ref[...] = jnp.dot(x_ref[...].astype(BF16), w_ref[...].astype(BF16),
                         preferred_element_type=F32).astype(o_ref.dtype)


def _pick_tile(n, prefs):
    for t in prefs:
        if n % t == 0:
            return t
    return n


def matmul(x, w, out_dtype=F32):
    if x.ndim == 2:
        return _matmul_grouped(x[None], w[None], out_dtype)[0]
    return _matmul_grouped(x, w, out_dtype)


def _matmul_grouped(x, w, out_dtype):
    g, m, k = x.shape
    n = w.shape[-1]
    tm = _pick_tile(m, (1024, 512, 256, 128))
    tn = _pick_tile(n, (512, 384, 256, 128))
    return pl.pallas_call(
        _mm_kernel,
        grid=(g, n // tn, m // tm),
        in_specs=[pl.BlockSpec((None, tm, k), lambda e, j, i: (e, i, 0)),
                  pl.BlockSpec((None, k, tn), lambda e, j, i: (e, 0, j))],
        out_specs=pl.BlockSpec((None, tm, tn), lambda e, j, i: (e, i, j)),
        out_shape=jax.ShapeDtypeStruct((g, m, n), out_dtype),
        compiler_params=_params(3),
        name="matmul",
    )(x, w)


def dense(x, w, splits=None):
    b, t, k = x.shape
    x2 = x.reshape(b * t, k)
    if splits is None:
        return matmul(x2, w).reshape(b, t, w.shape[-1])
    outs, lo = [], 0
    for width in splits:
        outs.append(matmul(x2, w[:, lo:lo + width]).reshape(b, t, width))
        lo += width
    return outs


def adaln(cond, w, b):
    m = jnp.dot(jax.nn.silu(cond), w, precision=lax.Precision.HIGHEST) + b
    m = m.reshape(cond.shape[0], 6, 1, D_MODEL)
    return [m[:, k] for k in range(6)]


def modulate(x, shift, scale):
    return x * (1.0 + scale) + shift


NORM_ROWS = 256
HALF_D = D_MODEL // 2


def _pack_bf16_pairs(h):
    bits = pltpu.bitcast(h.astype(BF16).astype(F32), jnp.uint32)
    return (bits[:, :HALF_D] >> 16) | (bits[:, HALF_D:] & jnp.uint32(0xFFFF0000))


def _make_post_norm_kernel(with_ffn, with_next):
    def body(*refs):
        x_ref, y_ref, gate_ref, g_ref, b_ref = refs[:5]
        pos = 5
        if with_ffn:
            shf_ref, scf_ref, wr_ref = refs[pos:pos + 3]
            pos += 3
        if with_next:
            shn_ref, scn_ref = refs[pos:pos + 2]
            pos += 2
        outs = refs[pos:]
        z = ALPHA * x_ref[...] + (1.0 + gate_ref[...]) * y_ref[...]
        mu = jnp.mean(z, axis=-1, keepdims=True)
        zc = z - mu
        var = jnp.mean(zc * zc, axis=-1, keepdims=True)
        xn = zc * lax.rsqrt(var + LN_EPS) * g_ref[...] + b_ref[...]
        outs[0][...] = xn
        k = 1
        if with_ffn:
            h = xn * (1.0 + scf_ref[...]) + shf_ref[...]
            outs[k][...] = _pack_bf16_pairs(h)
            outs[k + 1][...] = jnp.dot(h, wr_ref[...], preferred_element_type=F32,
                                       precision=lax.Precision.HIGHEST)
            k += 2
        if with_next:
            outs[k][...] = (xn * (1.0 + scn_ref[...]) + shn_ref[...]).astype(BF16)
    return body


def post_norm_fused(x, y, y_row0, gate, ln_g, ln_b, rows_per_set, ffn=None, nxt=None):
    n = x.shape[0]
    tps = rows_per_set // NORM_ROWS
    off = y_row0 // NORM_ROWS
    row = pl.BlockSpec((NORM_ROWS, D_MODEL), lambda i: (i, 0))
    tab = pl.BlockSpec((None, 1, D_MODEL), lambda i: (i // tps, 0, 0))
    one = pl.BlockSpec((None, 1, D_MODEL), lambda i: (0, 0, 0))
    args = [x, y, gate, ln_g.reshape(1, 1, D_MODEL), ln_b.reshape(1, 1, D_MODEL)]
    in_specs = [row, pl.BlockSpec((NORM_ROWS, D_MODEL), lambda i: (i + off, 0)), tab, one, one]
    out_shape = [jax.ShapeDtypeStruct((n, D_MODEL), F32)]
    out_specs = [row]
    if ffn is not None:
        args += [ffn[0], ffn[1], ffn[2]]
        in_specs += [tab, tab, pl.BlockSpec((D_MODEL, N_EXPERTS), lambda i: (0, 0))]
        out_shape += [jax.ShapeDtypeStruct((n, HALF_D), jnp.uint32), jax.ShapeDtypeStruct((n, N_EXPERTS), F32)]
        out_specs += [pl.BlockSpec((NORM_ROWS, HALF_D), lambda i: (i, 0)),
                      pl.BlockSpec((NORM_ROWS, N_EXPERTS), lambda i: (i, 0))]
    if nxt is not None:
        args += [nxt[0], nxt[1]]
        in_specs += [tab, tab]
        out_shape += [jax.ShapeDtypeStruct((n, D_MODEL), BF16)]
        out_specs += [row]
    return pl.pallas_call(
        _make_post_norm_kernel(ffn is not None, nxt is not None),
        grid=(n // NORM_ROWS,),
        in_specs=in_specs, out_specs=out_specs, out_shape=out_shape,
        compiler_params=_params(1),
        name="post_norm",
    )(*args)


def depthwise_conv_centred(u, w, b):
    up = jnp.pad(u, ((0, 0), (M_CONV // 2, M_CONV // 2), (0, 0)))
    t = u.shape[1]
    out = sum(up[:, j:j + t] * w[j] for j in range(M_CONV))
    return out + b


def ab_project(h, w_in, conv_w, conv_b, gate_b):
    p, pg, u = dense(h, w_in, splits=(4 * M_DIM, 4 * M_HEADS, P_DIM))
    qk = jax.nn.silu(depthwise_conv_centred(p[..., :2 * M_DIM], conv_w, conv_b))
    kscale = jnp.concatenate([jnp.ones((M_DIM,), F32), jnp.full((M_DIM,), M_HEAD_DIM ** -0.5, F32)])
    qk = (qk * kscale).astype(BF16)
    v = p[..., 2 * M_DIM:3 * M_DIM].astype(BF16)
    gates = (pg + gate_b).reshape(p.shape[:2] + (2, 2, M_HEADS))
    li = gates[..., 0, :]
    lf = jax.nn.log_sigmoid(gates[..., 1, :])
    grow = jnp.stack([li, lf], 3).reshape(p.shape[:2] + (4 * M_HEADS,))
    return qk, v, p, grow, u


def _mlstm_chain(q, k, v, li_row, lf_row, c_ref, n_ref, m_ref, reverse):
    L = M_CHUNK
    ti = lax.broadcasted_iota(jnp.int32, (L, L), 0)
    si = lax.broadcasted_iota(jnp.int32, (L, L), 1)
    mask = (si >= ti) if reverse else (si <= ti)
    mask_t = (ti >= si) if reverse else (ti <= si)
    lf_b = jnp.broadcast_to(lf_row, (L, L))
    lf_bt = lf_b.T
    li_col = jnp.broadcast_to(li_row, (L, L)).T[:, 0:1]
    b_col = jnp.sum(jnp.where(mask, lf_b, 0.0), axis=1, keepdims=True)
    b_row = jnp.sum(jnp.where(mask_t, lf_bt, 0.0), axis=0, keepdims=True)
    a_row = li_row - b_row
    dmat = jnp.where(mask, b_col + a_row, -jnp.inf)
    m_prev = m_ref[...][:, 0:1]
    inter = b_col + m_prev
    m_t = jnp.maximum(inter, jnp.max(dmat, axis=1, keepdims=True))
    w_intra = jnp.exp(dmat - m_t)
    w_inter = jnp.exp(inter - m_t)
    s = lax.dot_general(q, k, (((1,), (1,)), ((), ())), preferred_element_type=F32) * w_intra
    c_prev = c_ref[...]
    n_prev = n_ref[...]
    num = (w_inter * jnp.dot(q, c_prev.astype(BF16), preferred_element_type=F32)
           + jnp.dot(s.astype(BF16), v, preferred_element_type=F32))
    den = (w_inter * jnp.sum(q.astype(F32) * n_prev, axis=1, keepdims=True)
           + jnp.sum(s, axis=1, keepdims=True))
    h = num / jnp.maximum(jnp.abs(den), jnp.exp(-m_t))
    bl = jnp.sum(lf_row, axis=1, keepdims=True)
    m_new = jnp.maximum(bl + m_prev, jnp.max(bl + a_row, axis=1, keepdims=True))
    decay = jnp.exp(bl + m_prev - m_new)
    wg_col = jnp.exp(bl - b_col + li_col - m_new)
    kw = k.astype(F32) * wg_col
    c_ref[...] = decay * c_prev + jnp.dot(kw.T.astype(BF16), v, preferred_element_type=F32)
    n_ref[...] = decay * n_prev + jnp.sum(kw, axis=0, keepdims=True)
    m_ref[...] = jnp.broadcast_to(m_new, m_ref.shape)
    return h


def _mlstm_kernel(qf_ref, kf_ref, vf_ref, gf_ref, qb_ref, kb_ref, vb_ref, gb_ref,
                  hf_ref, hb_ref, c_scr, n_scr, m_scr):
    @pl.when(pl.program_id(1) == 0)
    def _():
        c_scr[...] = jnp.zeros_like(c_scr)
        n_scr[...] = jnp.zeros_like(n_scr)
        m_scr[...] = jnp.zeros_like(m_scr)

    dirs = ((qf_ref, kf_ref, vf_ref, gf_ref, hf_ref), (qb_ref, kb_ref, vb_ref, gb_ref, hb_ref))
    for d, (q_ref, k_ref, v_ref, g_ref, h_ref) in enumerate(dirs):
        for hd in range(M_HEADS):
            cols = slice(hd * M_HEAD_DIM, (hd + 1) * M_HEAD_DIM)
            row_i = d * 2 * M_HEADS + hd
            row_f = row_i + M_HEADS
            h = _mlstm_chain(q_ref[:, cols], k_ref[:, cols], v_ref[:, cols],
                             g_ref[row_i:row_i + 1, :], g_ref[row_f:row_f + 1, :],
                             c_scr.at[d, hd], n_scr.at[d, hd], m_scr.at[d, hd], reverse=(d == 1))
            h_ref[:, cols] = h.astype(h_ref.dtype)


def mlstm_scan(qk, v, grow, n_ctx_chunks):
    nb, tt, _ = v.shape
    nc = tt // M_CHUNK
    g = grow.reshape(nb, nc, M_CHUNK, 4 * M_HEADS).transpose(0, 1, 3, 2)

    def fwd(c):
        return c

    def bwd(c):
        return jnp.where(c < n_ctx_chunks, n_ctx_chunks - 1 - c, nc + n_ctx_chunks - 1 - c)

    def specs(cm):
        return [pl.BlockSpec((None, M_CHUNK, M_DIM), lambda b, c: (b, cm(c), 0)),
                pl.BlockSpec((None, M_CHUNK, M_DIM), lambda b, c: (b, cm(c), 1)),
                pl.BlockSpec((None, M_CHUNK, M_DIM), lambda b, c: (b, cm(c), 0)),
                pl.BlockSpec((None, None, 4 * M_HEADS, M_CHUNK), lambda b, c: (b, cm(c), 0, 0))]

    out_sds = jax.ShapeDtypeStruct((nb, tt, M_DIM), F32)
    return pl.pallas_call(
        _mlstm_kernel,
        grid=(nb, nc),
        in_specs=specs(fwd) + specs(bwd),
        out_specs=[pl.BlockSpec((None, M_CHUNK, M_DIM), lambda b, c: (b, fwd(c), 0)),
                   pl.BlockSpec((None, M_CHUNK, M_DIM), lambda b, c: (b, bwd(c), 0))],
        out_shape=[out_sds, out_sds],
        scratch_shapes=[pltpu.VMEM((2, M_HEADS, M_HEAD_DIM, M_HEAD_DIM), F32),
                        pltpu.VMEM((2, M_HEADS, 1, M_HEAD_DIM), F32),
                        pltpu.VMEM((2, M_HEADS, 1, 128), F32)],
        compiler_params=_params(2),
        name="mlstm_scan",
    )(qk, qk, v, g, qk, qk, v, g)


POST_ROWS = 256
POOL_HALO = 8


def _mixer_a_post_kernel(hf_ref, hb_ref, o_ref, u_ref, uprev_ref, unext_ref, hg_ref, pw_ref, ps_ref, out_ref):
    i = pl.program_id(1)
    n_tiles = pl.num_programs(1)
    t_seq = n_tiles * POST_ROWS
    hs = hf_ref[...] + hb_ref[...]
    gate = jax.nn.sigmoid(o_ref[...])
    for hd in range(M_HEADS):
        cols = slice(hd * M_HEAD_DIM, (hd + 1) * M_HEAD_DIM)
        x = hs[:, cols]
        xc = x - jnp.mean(x, axis=-1, keepdims=True)
        var = jnp.mean(xc * xc, axis=-1, keepdims=True)
        out_ref[:, cols] = (xc * lax.rsqrt(var + LN_EPS) * hg_ref[:, cols] * gate[:, cols]).astype(out_ref.dtype)
    u = u_ref[...]
    prev = jnp.where(i == 0, 0.0, uprev_ref[...])
    nxt = jnp.where(i == n_tiles - 1, 0.0, unext_ref[...])
    ext = jnp.concatenate([prev, u, nxt], axis=0)
    n_ext = POST_ROWS + 2 * POOL_HALO
    t = i * POST_ROWS + lax.broadcasted_iota(jnp.int32, (POST_ROWS, 1), 0)
    for g, w in enumerate(POOL_WINDOWS):
        cols = slice(g * P_GROUP_DIM, (g + 1) * P_GROUP_DIM)
        a = ext[:, cols]
        a = a + pltpu.roll(a, 1, axis=0)
        step = 1
        while 2 * step < w:
            a = pltpu.roll(a, step, axis=0) + pltpu.roll(a, n_ext - step, axis=0)
            step *= 2
        count = (jnp.minimum(t + w // 2, t_seq) - jnp.maximum(t - w // 2, 0)).astype(F32)
        pooled = a[POOL_HALO:POOL_HALO + POST_ROWS] / count - u[:, cols]
        y = jnp.dot(pooled.astype(BF16), pw_ref[g].astype(BF16), preferred_element_type=F32)
        out_ref[:, M_DIM + g * P_GROUP_DIM:M_DIM + (g + 1) * P_GROUP_DIM] = (y * ps_ref[:, cols]).astype(out_ref.dtype)


def mixer_a_post(h_f, h_b, h_row0, p, u, head_g, pool_w, pool_s):
    nb, t, _ = u.shape
    nt = t // POST_ROWS
    off = h_row0 // POST_ROWS
    per = POST_ROWS // POOL_HALO
    return pl.pallas_call(
        _mixer_a_post_kernel,
        grid=(nb, nt),
        in_specs=[pl.BlockSpec((None, POST_ROWS, M_DIM), lambda b, i: (b, i + off, 0)),
                  pl.BlockSpec((None, POST_ROWS, M_DIM), lambda b, i: (b, i + off, 0)),
                  pl.BlockSpec((None, POST_ROWS, M_DIM), lambda b, i: (b, i, 3)),
                  pl.BlockSpec((None, POST_ROWS, P_DIM), lambda b, i: (b, i, 0)),
                  pl.BlockSpec((None, POOL_HALO, P_DIM), lambda b, i: (b, jnp.maximum(i * per - 1, 0), 0)),
                  pl.BlockSpec((None, POOL_HALO, P_DIM), lambda b, i: (b, jnp.minimum((i + 1) * per, nt * per - 1), 0)),
                  pl.BlockSpec((1, M_DIM), lambda b, i: (0, 0)),
                  pl.BlockSpec((P_GROUPS, P_GROUP_DIM, P_GROUP_DIM), lambda b, i: (0, 0, 0)),
                  pl.BlockSpec((1, P_DIM), lambda b, i: (0, 0))],
        out_specs=pl.BlockSpec((None, POST_ROWS, D_MODEL), lambda b, i: (b, i, 0)),
        out_shape=jax.ShapeDtypeStruct((nb, t, D_MODEL), BF16),
        compiler_params=_params(2),
        name="mixer_a_post",
    )(h_f, h_b, p, u, u, u, head_g.reshape(1, M_DIM), pool_w, pool_s.reshape(1, P_DIM))


def mixer_mlstm_pool(h_lat, h_ctx, w_in, conv_w, conv_b, gate_b, head_g, pool_w, pool_s, w_out, with_ctx_out):
    t_ctx = h_ctx.shape[1]
    qk_c, v_c, pc, g_c, uc = ab_project(h_ctx, w_in, conv_w, conv_b, gate_b)
    qk_l, v_l, pl_, g_l, ul = ab_project(h_lat, w_in, conv_w, conv_b, gate_b)
    h_f, h_b = mlstm_scan(jnp.concatenate([qk_c, qk_l], 1), jnp.concatenate([v_c, v_l], 1),
                          jnp.concatenate([g_c, g_l], 1), t_ctx // M_CHUNK)
    y_lat = dense(mixer_a_post(h_f, h_b, t_ctx, pl_, ul, head_g, pool_w, pool_s), w_out)
    if not with_ctx_out:
        return y_lat, None
    y_ctx = dense(mixer_a_post(h_f, h_b, 0, pc, uc, head_g, pool_w, pool_s), w_out)
    return y_lat, y_ctx


def axial_rope_tables(n_tokens):
    rows = n_tokens // GRID_W
    row = jnp.repeat(jnp.arange(rows), GRID_W)
    col = jnp.tile(jnp.arange(GRID_W), rows)
    half = ROPE // 2
    inv = ROPE_THETA ** (-jnp.arange(0, half, 2, dtype=F32) / half)
    ang = jnp.stack([row[:, None] * inv, col[:, None] * inv], 1)
    return jnp.cos(ang), jnp.sin(ang)


def apply_axial_rope(x, cos, sin):
    xs = x.reshape(x.shape[:-1] + (2, 2, ROPE // 4))
    x1, x2 = xs[..., 0, :], xs[..., 1, :]
    out = jnp.stack([x1 * cos - x2 * sin, x2 * cos + x1 * sin], -2)
    return out.reshape(x.shape)


LOG2_E = 1.4426950408889634
Q_SCALE = (NOPE + ROPE) ** -0.5 * LOG2_E
ATTN_KEY_SPLITS = 4


def _rms_norm_rows(x, g):
    return x * lax.rsqrt(jnp.mean(jnp.square(x), axis=-1, keepdims=True) + RMS_EPS) * g


def _q_proj_kernel(x_ref, g_ref, w_ref, tab_ref, o_ref, xn_scr):
    @pl.when(pl.program_id(1) == 0)
    def _():
        xn_scr[...] = _rms_norm_rows(x_ref[...], g_ref[...]).astype(BF16)

    o = jnp.dot(xn_scr[...], w_ref[...], preferred_element_type=F32)
    r = o[:, NOPE:] * tab_ref[...]
    rot = r + pltpu.roll(r, ROPE, axis=1)
    o_ref[...] = (jnp.concatenate([o[:, :NOPE], rot[:, :ROPE]], axis=1) * Q_SCALE).astype(o_ref.dtype)


def q_proj(pq, q_norm_g, w_uq, tab):
    nb, t, _ = pq.shape
    w = w_uq.reshape(Q_LORA, A_HEADS, NOPE + ROPE)
    wpe = w[..., NOPE:].reshape(Q_LORA, A_HEADS, 2, 2, ROPE // 4)
    wsw = jnp.stack([-wpe[..., 1, :], wpe[..., 0, :]], -2).reshape(Q_LORA, A_HEADS, ROPE)
    wh = jnp.concatenate([w, wsw], -1).transpose(1, 0, 2).astype(BF16)
    tm = _pick_tile(t, (1024, 512, 256, 128))
    nt = t // tm
    return pl.pallas_call(
        _q_proj_kernel,
        grid=(nb * nt, A_HEADS),
        in_specs=[pl.BlockSpec((tm, Q_LORA), lambda i, h: (i, 0)),
                  pl.BlockSpec((1, Q_LORA), lambda i, h: (0, 0)),
                  pl.BlockSpec((None, Q_LORA, NOPE + 2 * ROPE), lambda i, h: (h, 0, 0)),
                  pl.BlockSpec((tm, 2 * ROPE), lambda i, h: (i % nt, 0))],
        out_specs=pl.BlockSpec((None, None, tm, NOPE + ROPE), lambda i, h: (i // nt, h, i % nt, 0)),
        out_shape=jax.ShapeDtypeStruct((nb, A_HEADS, t, NOPE + ROPE), BF16),
        scratch_shapes=[pltpu.VMEM((tm, Q_LORA), BF16)],
        compiler_params=_params(2),
        name="mla_q_proj",
    )(pq.reshape(nb * t, Q_LORA), q_norm_g.reshape(1, Q_LORA), wh, tab)


def _kv_proj_kernel(x_ref, g_ref, w_ref, kpe_ref, k_ref, v_ref, xn_scr):
    @pl.when(pl.program_id(1) == 0)
    def _():
        xn_scr[...] = _rms_norm_rows(x_ref[...], g_ref[...]).astype(BF16)

    o = jnp.dot(xn_scr[...], w_ref[...], preferred_element_type=F32)
    k_ref[...] = jnp.concatenate([o[:, :NOPE], kpe_ref[...]], axis=1).astype(k_ref.dtype)
    lane = lax.broadcasted_iota(jnp.int32, (o.shape[0], V_DIM), 1)
    ones_col = jnp.where(lane == 0, 1.0, 0.0)
    v_ref[...] = jnp.concatenate([o[:, NOPE:], ones_col], axis=1).astype(v_ref.dtype)


def kv_proj(pkv, kpe, kv_norm_g, w_ukv):
    nb, tk, _ = pkv.shape
    wh = w_ukv.reshape(KV_LORA, A_HEADS, NOPE + V_DIM).transpose(1, 0, 2).astype(BF16)
    tm = _pick_tile(tk, (2176, 1024, 512, 256, 128))
    nt = tk // tm
    return pl.pallas_call(
        _kv_proj_kernel,
        grid=(nb * nt, A_HEADS),
        in_specs=[pl.BlockSpec((tm, KV_LORA), lambda i, h: (i, 0)),
                  pl.BlockSpec((1, KV_LORA), lambda i, h: (0, 0)),
                  pl.BlockSpec((None, KV_LORA, NOPE + V_DIM), lambda i, h: (h, 0, 0)),
                  pl.BlockSpec((tm, ROPE), lambda i, h: (i, 0))],
        out_specs=[pl.BlockSpec((None, None, tm, NOPE + ROPE), lambda i, h: (i // nt, h, i % nt, 0)),
                   pl.BlockSpec((None, None, tm, 2 * V_DIM), lambda i, h: (i // nt, h, i % nt, 0))],
        out_shape=[jax.ShapeDtypeStruct((nb, A_HEADS, tk, NOPE + ROPE), BF16),
                   jax.ShapeDtypeStruct((nb, A_HEADS, tk, 2 * V_DIM), BF16)],
        scratch_shapes=[pltpu.VMEM((tm, KV_LORA), BF16)],
        compiler_params=_params(2),
        name="mla_kv_proj",
    )(pkv.reshape(nb * tk, KV_LORA), kv_norm_g.reshape(1, KV_LORA), wh, kpe.reshape(nb * tk, ROPE))


def _attn_kernel(q_ref, k_ref, v_ref, o_ref):
    q = q_ref[...]
    n_lanes = k_ref.shape[0] // 128
    sizes = [(n_lanes // ATTN_KEY_SPLITS + (c < n_lanes % ATTN_KEY_SPLITS)) * 128 for c in range(ATTN_KEY_SPLITS)]
    m = o = None
    lo = 0
    for size in filter(None, sizes):
        ks = slice(lo, lo + size)
        lo += size
        s = lax.dot_general(q, k_ref[ks, :], (((1,), (1,)), ((), ())), preferred_element_type=F32)
        mc = jnp.max(s, axis=-1, keepdims=True)
        m_new = mc if m is None else jnp.maximum(m, mc)
        p = jnp.exp2((s - m_new).astype(BF16))
        oc = jnp.dot(p, v_ref[ks, :], preferred_element_type=F32)
        o = oc if o is None else o * jnp.exp2(m - m_new) + oc
        m = m_new
    o_ref[...] = (o[:, :V_DIM] / o[:, V_DIM:V_DIM + 1]).astype(o_ref.dtype)


def attend(q, k, vh):
    nb, _, tq, _ = q.shape
    tk = k.shape[2]
    bq = _pick_tile(tq, (512, 256, 128))
    return pl.pallas_call(
        _attn_kernel,
        grid=(nb, A_HEADS, tq // bq),
        in_specs=[pl.BlockSpec((None, None, bq, NOPE + ROPE), lambda b, h, i: (b, h, i, 0)),
                  pl.BlockSpec((None, None, tk, NOPE + ROPE), lambda b, h, i: (b, h, 0, 0)),
                  pl.BlockSpec((None, None, tk, 2 * V_DIM), lambda b, h, i: (b, h, 0, 0))],
        out_specs=pl.BlockSpec((None, bq, V_DIM), lambda b, h, i: (b, i, h)),
        out_shape=jax.ShapeDtypeStruct((nb, tq, A_HEADS * V_DIM), BF16),
        compiler_params=_params(3),
        name="mla_attention",
    )(q, k, vh)


def mixer_mla(h_lat, h_ctx, w_in, q_norm_g, kv_norm_g, w_uq, w_ukv, w_o, with_ctx_out):
    t_lat, t_ctx = h_lat.shape[1], h_ctx.shape[1]
    cos, sin = axial_rope_tables(t_lat)
    cos_full = jnp.repeat(cos, 2, axis=1).reshape(t_lat, ROPE)
    sin_full = jnp.repeat(sin, 2, axis=1).reshape(t_lat, ROPE)
    pq_l, pkv_l, pr_l = dense(h_lat, w_in, splits=(Q_LORA, KV_LORA, ROPE))
    if with_ctx_out:
        pq_c, pkv_c, pr_c = dense(h_ctx, w_in, splits=(Q_LORA, KV_LORA, ROPE))
    else:
        pkv_c, pr_c = dense(h_ctx, w_in[:, Q_LORA:], splits=(KV_LORA, ROPE))
    kpe = jnp.concatenate([pr_c, apply_axial_rope(pr_l, cos, sin)], 1)
    k, vh = kv_proj(jnp.concatenate([pkv_c, pkv_l], 1), kpe, kv_norm_g, w_ukv)
    q_l = q_proj(pq_l, q_norm_g, w_uq, jnp.concatenate([cos_full, sin_full], -1))
    y_lat = dense(attend(q_l, k, vh), w_o)
    if not with_ctx_out:
        return y_lat, None
    no_rot = jnp.concatenate([jnp.ones((t_ctx, ROPE), F32), jnp.zeros((t_ctx, ROPE), F32)], -1)
    q_c = q_proj(pq_c, q_norm_g, w_uq, no_rot)
    y_ctx = dense(attend(q_c, k[:, :, :t_ctx], vh[:, :, :t_ctx]), w_o)
    return y_lat, y_ctx


MOE_TF = 256
MOE_TN = 256
MOE_NF = D_MODEL // MOE_TF
MOE_NN = D_MODEL // MOE_TN


def _moe_kernel(rows_ref, h_hbm, gate_ref, w1_ref, w3_ref, w2_ref, f_in_hbm, f_hbm,
                xg, xb, hmid, ybuf, stage, sem_g, sem_r, sem_w):
    del f_in_hbm
    e = pl.program_id(0)
    j = pl.program_id(1)
    n_e = pl.num_programs(0)
    m = xg.shape[0]
    per_step = m // MOE_NF
    last_step = MOE_NF + MOE_NN - 1

    def gather(expert, lo, n):
        def body(i, carry):
            r = rows_ref[expert * m + lo + i]
            pltpu.make_async_copy(h_hbm.at[pl.ds(r, 1), :], xg.at[pl.ds(lo + i, 1), :], sem_g.at[0]).start()
            return carry
        lax.fori_loop(0, n, body, 0, unroll=8)

    def f_rows(expert, lo, to_vmem):
        def body(i, carry):
            r = rows_ref[expert * m + lo + i]
            if to_vmem:
                pltpu.make_async_copy(f_hbm.at[pl.ds(r, 1), :], stage.at[pl.ds(i, 1), :], sem_r.at[0]).start()
            else:
                pltpu.make_async_copy(stage.at[pl.ds(i, 1), :], f_hbm.at[pl.ds(r, 1), :], sem_w.at[0]).start()
            return carry
        lax.fori_loop(0, per_step, body, 0, unroll=8)

    def wait_all(sem):
        pltpu.make_async_copy(stage, stage, sem.at[0]).wait()

    def add_piece(expert, lo):
        wait_all(sem_r)
        if not isinstance(lo, int):
            lo = pl.multiple_of(lo, 16)
        for n in range(MOE_NN):
            cols = slice(n * MOE_TN, (n + 1) * MOE_TN)
            stage[:, cols] += ybuf[n, pl.ds(lo, per_step), :].astype(F32)
        f_rows(expert, lo, to_vmem=False)

    @pl.when((e == 0) & (j == 0))
    def _():
        gather(0, 0, m)

    @pl.when((e > 0) & (j >= 1) & (j <= MOE_NF))
    def _():
        wait_all(sem_w)

    @pl.when((e > 0) & (j < MOE_NF))
    def _():
        f_rows(e - 1, j * per_step, to_vmem=True)

    @pl.when(j == 0)
    def _():
        pltpu.make_async_copy(xg, xg, sem_g.at[0]).wait()
        w = xg[...]
        xb[:, :HALF_D] = pltpu.bitcast(w << 16, F32).astype(BF16)
        xb[:, HALF_D:] = pltpu.bitcast(w & jnp.uint32(0xFFFF0000), F32).astype(BF16)

    @pl.when(j < MOE_NF)
    def _():
        x = xb[...]
        a = jnp.dot(x, w1_ref[...].astype(BF16), preferred_element_type=F32)
        g = jnp.dot(x, w3_ref[...].astype(BF16), preferred_element_type=F32)
        hmid[j] = (jax.nn.silu(a) * g).astype(BF16)

    @pl.when((e > 0) & (j < MOE_NF))
    def _():
        add_piece(e - 1, j * per_step)

    @pl.when((j >= MOE_NF) & (e + 1 < n_e))
    def _():
        gather(e + 1, (j - MOE_NF) * per_step, per_step)

    @pl.when(j >= MOE_NF)
    def _():
        acc = jnp.dot(hmid[0], w2_ref[0:MOE_TF, :].astype(BF16), preferred_element_type=F32)
        for f in range(1, MOE_NF):
            acc += jnp.dot(hmid[f], w2_ref[f * MOE_TF:(f + 1) * MOE_TF, :].astype(BF16),
                           preferred_element_type=F32)
        ybuf[j - MOE_NF] = (acc * gate_ref[...]).astype(BF16)

    @pl.when((e == n_e - 1) & (j == last_step))
    def _():
        for piece in range(MOE_NF):
            f_rows(e, piece * per_step, to_vmem=True)
            add_piece(e, piece * per_step)
            wait_all(sem_w)


def expert_ffn_rows(h_packed, rows, gates, w1, w3, w2, layer):
    n_e, m = rows.shape
    n_rows = h_packed.shape[0]
    any_spec = pl.BlockSpec(memory_space=pl.ANY)
    return pl.pallas_call(
        _moe_kernel,
        grid_spec=pltpu.PrefetchScalarGridSpec(
            num_scalar_prefetch=1,
            grid=(n_e, MOE_NF + MOE_NN),
            in_specs=[any_spec,
                      pl.BlockSpec((None, m, 1), lambda e, j, r: (e, 0, 0)),
                      pl.BlockSpec((None, None, D_MODEL, MOE_TF),
                                   lambda e, j, r: (layer, e, 0, jnp.minimum(j, MOE_NF - 1))),
                      pl.BlockSpec((None, None, D_MODEL, MOE_TF),
                                   lambda e, j, r: (layer, e, 0, jnp.minimum(j, MOE_NF - 1))),
                      pl.BlockSpec((None, None, D_MODEL, MOE_TN),
                                   lambda e, j, r: (layer, e, 0, jnp.maximum(j - MOE_NF, 0))),
                      any_spec],
            out_specs=any_spec,
            scratch_shapes=[pltpu.VMEM((m, HALF_D), jnp.uint32),
                            pltpu.VMEM((m, D_MODEL), BF16),
                            pltpu.VMEM((MOE_NF, m, MOE_TF), BF16),
                            pltpu.VMEM((MOE_NN, m, MOE_TN), BF16),
                            pltpu.VMEM((m // MOE_NF, D_MODEL), F32),
                            pltpu.SemaphoreType.DMA((1,)),
                            pltpu.SemaphoreType.DMA((1,)),
                            pltpu.SemaphoreType.DMA((1,))]),
        out_shape=jax.ShapeDtypeStruct((n_rows, D_MODEL), F32),
        input_output_aliases={6: 0},
        compiler_params=_params(2),
        name="expert_ffn",
    )(rows.reshape(-1), h_packed, gates.reshape(n_e, m, 1), w1, w3, w2, jnp.zeros((n_rows, D_MODEL), F32))


def route(logits, nb):
    t = logits.shape[0] // nb
    cap = EC_FACTOR * t // N_EXPERTS
    aff = jax.nn.softmax(logits.reshape(nb, t, N_EXPERTS), axis=-1)
    return lax.top_k(jnp.swapaxes(aff, 1, 2), cap)


def expert_choice_ffn(sets, nb, w1, w3, w2, layer):
    rows, gates, base = [], [], 0
    for _, logits in sets:
        gate, idx = route(logits, nb)
        t = logits.shape[0] // nb
        rid = base + jnp.arange(nb, dtype=jnp.int32)[:, None, None] * t + idx
        rows.append(rid.transpose(1, 0, 2).reshape(N_EXPERTS, -1))
        gates.append(gate.transpose(1, 0, 2).reshape(N_EXPERTS, -1))
        base += nb * t
    rows = jnp.concatenate(rows, 1)
    h_packed = jnp.concatenate([hp for hp, _ in sets], 0) if len(sets) > 1 else sets[0][0]
    return expert_ffn_rows(h_packed, rows, jnp.concatenate(gates, 1), w1, w3, w2, layer)


def kernel(x, c, ctx, c_ctx, ada_w, ada_b, ln_g, ln_b, ab_w_in, ab_conv_w, ab_conv_b, ab_gate_b,
           ab_head_g, ab_pool_w, ab_pool_s, ab_w_out, mla_w_in, mla_q_norm_g, mla_kv_norm_g,
           mla_w_uq, mla_w_ukv, mla_w_o, moe_router, moe_w1, moe_w3, moe_w2):
    nb, t_lat, _ = x.shape
    t_ctx = ctx.shape[1]
    n_lat, n_ctx = nb * t_lat, nb * t_ctx
    cond = jnp.concatenate([c, c_ctx[None]], 0)
    mods = [adaln(cond, ada_w[i], ada_b[i]) for i in range(DEPTH)]
    mods_l = [[m[:nb] for m in ms] for ms in mods]
    mods_c = [[m[nb:] for m in ms] for ms in mods]
    x_lat, x_ctx = x.reshape(n_lat, D_MODEL), ctx.reshape(n_ctx, D_MODEL)
    h_lat = modulate(x, mods_l[0][0], mods_l[0][1]).astype(BF16)
    h_ctx = modulate(ctx, mods_c[0][0], mods_c[0][1]).astype(BF16)
    for i in range(DEPTH):
        with_ctx = i < DEPTH - 1
        j = i // 2
        _, _, ga_l, shf_l, scf_l, gaf_l = mods_l[i]
        _, _, ga_c, shf_c, scf_c, gaf_c = mods_c[i]
        if i % 2 == 0:
            y_lat, y_ctx = mixer_mlstm_pool(h_lat, h_ctx, ab_w_in[j], ab_conv_w[j], ab_conv_b[j], ab_gate_b[j],
                                            ab_head_g[j], ab_pool_w[j], ab_pool_s[j], ab_w_out[j], with_ctx)
        else:
            y_lat, y_ctx = mixer_mla(h_lat, h_ctx, mla_w_in[j], mla_q_norm_g[j], mla_kv_norm_g[j],
                                     mla_w_uq[j], mla_w_ukv[j], mla_w_o[j], with_ctx)
        x_lat, hp_l, lg_l = post_norm_fused(x_lat, y_lat.reshape(n_lat, D_MODEL), 0, ga_l, ln_g[i, 0], ln_b[i, 0],
                                            t_lat, ffn=(shf_l, scf_l, moe_router[i]))
        sets = [(hp_l, lg_l)]
        if with_ctx:
            x_ctx, hp_c, lg_c = post_norm_fused(x_ctx, y_ctx.reshape(n_ctx, D_MODEL), 0, ga_c, ln_g[i, 0],
                                                ln_b[i, 0], n_ctx, ffn=(shf_c, scf_c, moe_router[i]))
            sets.append((hp_c, lg_c))
        f_all = expert_choice_ffn(sets, nb, moe_w1, moe_w3, moe_w2, i)
        nxt_l = (mods_l[i + 1][0], mods_l[i + 1][1]) if with_ctx else None
        res = post_norm_fused(x_lat, f_all, 0, gaf_l, ln_g[i, 1], ln_b[i, 1], t_lat, nxt=nxt_l)
        x_lat = res[0]
        if with_ctx:
            h_lat = res[1].reshape(nb, t_lat, D_MODEL)
            x_ctx, h_ctx = post_norm_fused(x_ctx, f_all, n_lat, gaf_c, ln_g[i, 1], ln_b[i, 1], n_ctx,
                                           nxt=(mods_c[i + 1][0], mods_c[i + 1][1]))
            h_ctx = h_ctx.reshape(nb, t_ctx, D_MODEL)
    return x_lat.reshape(x.shape)
```

```python
import jax
import jax.numpy as jnp
from jax import lax
from jax.experimental import pallas as pl
from jax.experimental.pallas import tpu as pltpu

D_MODEL = 2048
DEPTH = 2
F32 = jnp.float32
BF16 = jnp.bfloat16
GRID_W = 64

M_HEADS = 4
M_DIM = D_MODEL // 2
M_HEAD_DIM = M_DIM // M_HEADS
M_CHUNK = 128
M_CONV = 3
P_DIM = D_MODEL - M_DIM
P_GROUPS = 4
P_GROUP_DIM = P_DIM // P_GROUPS
POOL_WINDOWS = (2, 4, 8, 16)
A_HEADS = 16
Q_LORA = 1536
KV_LORA = 512
NOPE = 128
ROPE = 64
V_DIM = 128
ROPE_THETA = 10000.0
N_EXPERTS = 16
EC_FACTOR = 2
LN_EPS = 1e-5
RMS_EPS = 1e-6
ALPHA = (2 * DEPTH) ** 0.25

V7X_VMEM_BYTES = 64 * 1024 * 1024
VMEM_LIMIT_BYTES = V7X_VMEM_BYTES - 6 * 1024 * 1024


def _params(n_axes):
    return pltpu.CompilerParams(dimension_semantics=("arbitrary",) * n_axes,
                                vmem_limit_bytes=VMEM_LIMIT_BYTES)


def _mm_kernel(x_ref, w_ref, o_ref):
    o_ref[...] = jnp.dot(x_ref[...].astype(BF16), w_ref[...].astype(BF16),
                         preferred_element_type=F32).astype(o_ref.dtype)


def _pick_tile(n, prefs):
    for t in prefs:
        if n % t == 0:
            return t
    return n


def matmul(x, w, out_dtype=F32):
    if x.ndim == 2:
        return _matmul_grouped(x[None], w[None], out_dtype)[0]
    return _matmul_grouped(x, w, out_dtype)


def _matmul_grouped(x, w, out_dtype):
    g, m, k = x.shape
    n = w.shape[-1]
    tm = _pick_tile(m, (1024, 512, 256, 128))
    tn = _pick_tile(n, (512, 384, 256, 128))
    return pl.pallas_call(
        _mm_kernel,
        grid=(g, n // tn, m // tm),
        in_specs=[pl.BlockSpec((None, tm, k), lambda e, j, i: (e, i, 0)),
                  pl.BlockSpec((None, k, tn), lambda e, j, i: (e, 0, j))],
        out_specs=pl.BlockSpec((None, tm, tn), lambda e, j, i: (e, i, j)),
        out_shape=jax.ShapeDtypeStruct((g, m, n), out_dtype),
        compiler_params=_params(3),
        name="matmul",
    )(x, w)


def dense(x, w, splits=None):
    b, t, k = x.shape
    x2 = x.reshape(b * t, k)
    if splits is None:
        return matmul(x2, w).reshape(b, t, w.shape[-1])
    outs, lo = [], 0
    for width in splits:
        outs.append(matmul(x2, w[:, lo:lo + width]).reshape(b, t, width))
        lo += width
    return outs


def adaln(cond, w, b):
    m = jnp.dot(jax.nn.silu(cond), w, precision=lax.Precision.HIGHEST) + b
    m = m.reshape(cond.shape[0], 6, 1, D_MODEL)
    return [m[:, k] for k in range(6)]


def modulate(x, shift, scale):
    return x * (1.0 + scale) + shift


NORM_ROWS = 256
HALF_D = D_MODEL // 2


def _pack_bf16_pairs(h):
    bits = pltpu.bitcast(h.astype(BF16).astype(F32), jnp.uint32)
    return (bits[:, :HALF_D] >> 16) | (bits[:, HALF_D:] & jnp.uint32(0xFFFF0000))


def _make_post_norm_kernel(with_ffn, with_next):
    def body(*refs):
        x_ref, y_ref, gate_ref, g_ref, b_ref = refs[:5]
        pos = 5
        if with_ffn:
            shf_ref, scf_ref, wr_ref = refs[pos:pos + 3]
            pos += 3
        if with_next:
            shn_ref, scn_ref = refs[pos:pos + 2]
            pos += 2
        outs = refs[pos:]
        z = ALPHA * x_ref[...] + (1.0 + gate_ref[...]) * y_ref[...]
        mu = jnp.mean(z, axis=-1, keepdims=True)
        zc = z - mu
        var = jnp.mean(zc * zc, axis=-1, keepdims=True)
        xn = zc * lax.rsqrt(var + LN_EPS) * g_ref[...] + b_ref[...]
        outs[0][...] = xn
        k = 1
        if with_ffn:
            h = xn * (1.0 + scf_ref[...]) + shf_ref[...]
            outs[k][...] = _pack_bf16_pairs(h)
            outs[k + 1][...] = jnp.dot(h, wr_ref[...], preferred_element_type=F32,
                                       precision=lax.Precision.HIGHEST)
            k += 2
        if with_next:
            outs[k][...] = (xn * (1.0 + scn_ref[...]) + shn_ref[...]).astype(BF16)
    return body


def post_norm_fused(x, y, y_row0, gate, ln_g, ln_b, rows_per_set, ffn=None, nxt=None):
    n = x.shape[0]
    tps = rows_per_set // NORM_ROWS
    off = y_row0 // NORM_ROWS
    row = pl.BlockSpec((NORM_ROWS, D_MODEL), lambda i: (i, 0))
    tab = pl.BlockSpec((None, 1, D_MODEL), lambda i: (i // tps, 0, 0))
    one = pl.BlockSpec((None, 1, D_MODEL), lambda i: (0, 0, 0))
    args = [x, y, gate, ln_g.reshape(1, 1, D_MODEL), ln_b.reshape(1, 1, D_MODEL)]
    in_specs = [row, pl.BlockSpec((NORM_ROWS, D_MODEL), lambda i: (i + off, 0)), tab, one, one]
    out_shape = [jax.ShapeDtypeStruct((n, D_MODEL), F32)]
    out_specs = [row]
    if ffn is not None:
        args += [ffn[0], ffn[1], ffn[2]]
        in_specs += [tab, tab, pl.BlockSpec((D_MODEL, N_EXPERTS), lambda i: (0, 0))]
        out_shape += [jax.ShapeDtypeStruct((n, HALF_D), jnp.uint32), jax.ShapeDtypeStruct((n, N_EXPERTS), F32)]
        out_specs += [pl.BlockSpec((NORM_ROWS, HALF_D), lambda i: (i, 0)),
                      pl.BlockSpec((NORM_ROWS, N_EXPERTS), lambda i: (i, 0))]
    if nxt is not None:
        args += [nxt[0], nxt[1]]
        in_specs += [tab, tab]
        out_shape += [jax.ShapeDtypeStruct((n, D_MODEL), BF16)]
        out_specs += [row]
    return pl.pallas_call(
        _make_post_norm_kernel(ffn is not None, nxt is not None),
        grid=(n // NORM_ROWS,),
        in_specs=in_specs, out_specs=out_specs, out_shape=out_shape,
        compiler_params=_params(1),
        name="post_norm",
    )(*args)


def depthwise_conv_centred(u, w, b):
    up = jnp.pad(u, ((0, 0), (M_CONV // 2, M_CONV // 2), (0, 0)))
    t = u.shape[1]
    out = sum(up[:, j:j + t] * w[j] for j in range(M_CONV))
    return out + b


def ab_project(h, w_in, conv_w, conv_b, gate_b):
    p, pg, u = dense(h, w_in, splits=(4 * M_DIM, 4 * M_HEADS, P_DIM))
    qk = jax.nn.silu(depthwise_conv_centred(p[..., :2 * M_DIM], conv_w, conv_b))
    kscale = jnp.concatenate([jnp.ones((M_DIM,), F32), jnp.full((M_DIM,), M_HEAD_DIM ** -0.5, F32)])
    qk = (qk * kscale).astype(BF16)
    v = p[..., 2 * M_DIM:3 * M_DIM].astype(BF16)
    gates = (pg + gate_b).reshape(p.shape[:2] + (2, 2, M_HEADS))
    li = gates[..., 0, :]
    lf = jax.nn.log_sigmoid(gates[..., 1, :])
    grow = jnp.stack([li, lf], 3).reshape(p.shape[:2] + (4 * M_HEADS,))
    return qk, v, p, grow, u


def _mlstm_chain(q, k, v, li_row, lf_row, c_ref, n_ref, m_ref, reverse):
    L = M_CHUNK
    ti = lax.broadcasted_iota(jnp.int32, (L, L), 0)
    si = lax.broadcasted_iota(jnp.int32, (L, L), 1)
    mask = (si >= ti) if reverse else (si <= ti)
    mask_t = (ti >= si) if reverse else (ti <= si)
    lf_b = jnp.broadcast_to(lf_row, (L, L))
    lf_bt = lf_b.T
    li_col = jnp.broadcast_to(li_row, (L, L)).T[:, 0:1]
    b_col = jnp.sum(jnp.where(mask, lf_b, 0.0), axis=1, keepdims=True)
    b_row = jnp.sum(jnp.where(mask_t, lf_bt, 0.0), axis=0, keepdims=True)
    a_row = li_row - b_row
    dmat = jnp.where(mask, b_col + a_row, -jnp.inf)
    m_prev = m_ref[...][:, 0:1]
    inter = b_col + m_prev
    m_t = jnp.maximum(inter, jnp.max(dmat, axis=1, keepdims=True))
    w_intra = jnp.exp(dmat - m_t)
    w_inter = jnp.exp(inter - m_t)
    s = lax.dot_general(q, k, (((1,), (1,)), ((), ())), preferred_element_type=F32) * w_intra
    c_prev = c_ref[...]
    n_prev = n_ref[...]
    num = (w_inter * jnp.dot(q, c_prev.astype(BF16), preferred_element_type=F32)
           + jnp.dot(s.astype(BF16), v, preferred_element_type=F32))
    den = (w_inter * jnp.sum(q.astype(F32) * n_prev, axis=1, keepdims=True)
           + jnp.sum(s, axis=1, keepdims=True))
    h = num / jnp.maximum(jnp.abs(den), jnp.exp(-m_t))
    bl = jnp.sum(lf_row, axis=1, keepdims=True)
    m_new = jnp.maximum(bl + m_prev, jnp.max(bl + a_row, axis=1, keepdims=True))
    decay = jnp.exp(bl + m_prev - m_new)
    wg_col = jnp.exp(bl - b_col + li_col - m_new)
    kw = k.astype(F32) * wg_col
    c_ref[...] = decay * c_prev + jnp.dot(kw.T.astype(BF16), v, preferred_element_type=F32)
    n_ref[...] = decay * n_prev + jnp.sum(kw, axis=0, keepdims=True)
    m_ref[...] = jnp.broadcast_to(m_new, m_ref.shape)
    return h


def _mlstm_kernel(qf_ref, kf_ref, vf_ref, gf_ref, qb_ref, kb_ref, vb_ref, gb_ref,
                  hf_ref, hb_ref, c_scr, n_scr, m_scr):
    @pl.when(pl.program_id(1) == 0)
    def _():
        c_scr[...] = jnp.zeros_like(c_scr)
        n_scr[...] = jnp.zeros_like(n_scr)
        m_scr[...] = jnp.zeros_like(m_scr)

    dirs = ((qf_ref, kf_ref, vf_ref, gf_ref, hf_ref), (qb_ref, kb_ref, vb_ref, gb_ref, hb_ref))
    for d, (q_ref, k_ref, v_ref, g_ref, h_ref) in enumerate(dirs):
        for hd in range(M_HEADS):
            cols = slice(hd * M_HEAD_DIM, (hd + 1) * M_HEAD_DIM)
            row_i = d * 2 * M_HEADS + hd
            row_f = row_i + M_HEADS
            h = _mlstm_chain(q_ref[:, cols], k_ref[:, cols], v_ref[:, cols],
                             g_ref[row_i:row_i + 1, :], g_ref[row_f:row_f + 1, :],
                             c_scr.at[d, hd], n_scr.at[d, hd], m_scr.at[d, hd], reverse=(d == 1))
            h_ref[:, cols] = h.astype(h_ref.dtype)


def mlstm_scan(qk, v, grow, n_ctx_chunks):
    nb, tt, _ = v.shape
    nc = tt // M_CHUNK
    g = grow.reshape(nb, nc, M_CHUNK, 4 * M_HEADS).transpose(0, 1, 3, 2)

    def fwd(c):
        return c

    def bwd(c):
        return jnp.where(c < n_ctx_chunks, n_ctx_chunks - 1 - c, nc + n_ctx_chunks - 1 - c)

    def specs(cm):
        return [pl.BlockSpec((None, M_CHUNK, M_DIM), lambda b, c: (b, cm(c), 0)),
                pl.BlockSpec((None, M_CHUNK, M_DIM), lambda b, c: (b, cm(c), 1)),
                pl.BlockSpec((None, M_CHUNK, M_DIM), lambda b, c: (b, cm(c), 0)),
                pl.BlockSpec((None, None, 4 * M_HEADS, M_CHUNK), lambda b, c: (b, cm(c), 0, 0))]

    out_sds = jax.ShapeDtypeStruct((nb, tt, M_DIM), F32)
    return pl.pallas_call(
        _mlstm_kernel,
        grid=(nb, nc),
        in_specs=specs(fwd) + specs(bwd),
        out_specs=[pl.BlockSpec((None, M_CHUNK, M_DIM), lambda b, c: (b, fwd(c), 0)),
                   pl.BlockSpec((None, M_CHUNK, M_DIM), lambda b, c: (b, bwd(c), 0))],
        out_shape=[out_sds, out_sds],
        scratch_shapes=[pltpu.VMEM((2, M_HEADS, M_HEAD_DIM, M_HEAD_DIM), F32),
                        pltpu.VMEM((2, M_HEADS, 1, M_HEAD_DIM), F32),
                        pltpu.VMEM((2, M_HEADS, 1, 128), F32)],
        compiler_params=_params(2),
        name="mlstm_scan",
    )(qk, qk, v, g, qk, qk, v, g)


POST_ROWS = 256
POOL_HALO = 8


def _mixer_a_post_kernel(hf_ref, hb_ref, o_ref, u_ref, uprev_ref, unext_ref, hg_ref, pw_ref, ps_ref, out_ref):
    i = pl.program_id(1)
    n_tiles = pl.num_programs(1)
    t_seq = n_tiles * POST_ROWS
    hs = hf_ref[...] + hb_ref[...]
    gate = jax.nn.sigmoid(o_ref[...])
    for hd in range(M_HEADS):
        cols = slice(hd * M_HEAD_DIM, (hd + 1) * M_HEAD_DIM)
        x = hs[:, cols]
        xc = x - jnp.mean(x, axis=-1, keepdims=True)
        var = jnp.mean(xc * xc, axis=-1, keepdims=True)
        out_ref[:, cols] = (xc * lax.rsqrt(var + LN_EPS) * hg_ref[:, cols] * gate[:, cols]).astype(out_ref.dtype)
    u = u_ref[...]
    prev = jnp.where(i == 0, 0.0, uprev_ref[...])
    nxt = jnp.where(i == n_tiles - 1, 0.0, unext_ref[...])
    ext = jnp.concatenate([prev, u, nxt], axis=0)
    n_ext = POST_ROWS + 2 * POOL_HALO
    t = i * POST_ROWS + lax.broadcasted_iota(jnp.int32, (POST_ROWS, 1), 0)
    for g, w in enumerate(POOL_WINDOWS):
        cols = slice(g * P_GROUP_DIM, (g + 1) * P_GROUP_DIM)
        a = ext[:, cols]
        a = a + pltpu.roll(a, 1, axis=0)
        step = 1
        while 2 * step < w:
            a = pltpu.roll(a, step, axis=0) + pltpu.roll(a, n_ext - step, axis=0)
            step *= 2
        count = (jnp.minimum(t + w // 2, t_seq) - jnp.maximum(t - w // 2, 0)).astype(F32)
        pooled = a[POOL_HALO:POOL_HALO + POST_ROWS] / count - u[:, cols]
        y = jnp.dot(pooled.astype(BF16), pw_ref[g].astype(BF16), preferred_element_type=F32)
        out_ref[:, M_DIM + g * P_GROUP_DIM:M_DIM + (g + 1) * P_GROUP_DIM] = (y * ps_ref[:, cols]).astype(out_ref.dtype)


def mixer_a_post(h_f, h_b, h_row0, p, u, head_g, pool_w, pool_s):
    nb, t, _ = u.shape
    nt = t // POST_ROWS
    off = h_row0 // POST_ROWS
    per = POST_ROWS // POOL_HALO
    return pl.pallas_call(
        _mixer_a_post_kernel,
        grid=(nb, nt),
        in_specs=[pl.BlockSpec((None, POST_ROWS, M_DIM), lambda b, i: (b, i + off, 0)),
                  pl.BlockSpec((None, POST_ROWS, M_DIM), lambda b, i: (b, i + off, 0)),
                  pl.BlockSpec((None, POST_ROWS, M_DIM), lambda b, i: (b, i, 3)),
                  pl.BlockSpec((None, POST_ROWS, P_DIM), lambda b, i: (b, i, 0)),
                  pl.BlockSpec((None, POOL_HALO, P_DIM), lambda b, i: (b, jnp.maximum(i * per - 1, 0), 0)),
                  pl.BlockSpec((None, POOL_HALO, P_DIM), lambda b, i: (b, jnp.minimum((i + 1) * per, nt * per - 1), 0)),
                  pl.BlockSpec((1, M_DIM), lambda b, i: (0, 0)),
                  pl.BlockSpec((P_GROUPS, P_GROUP_DIM, P_GROUP_DIM), lambda b, i: (0, 0, 0)),
                  pl.BlockSpec((1, P_DIM), lambda b, i: (0, 0))],
        out_specs=pl.BlockSpec((None, POST_ROWS, D_MODEL), lambda b, i: (b, i, 0)),
        out_shape=jax.ShapeDtypeStruct((nb, t, D_MODEL), BF16),
        compiler_params=_params(2),
        name="mixer_a_post",
    )(h_f, h_b, p, u, u, u, head_g.reshape(1, M_DIM), pool_w, pool_s.reshape(1, P_DIM))


def mixer_mlstm_pool(h_lat, h_ctx, w_in, conv_w, conv_b, gate_b, head_g, pool_w, pool_s, w_out, with_ctx_out):
    t_ctx = h_ctx.shape[1]
    qk_c, v_c, pc, g_c, uc = ab_project(h_ctx, w_in, conv_w, conv_b, gate_b)
    qk_l, v_l, pl_, g_l, ul = ab_project(h_lat, w_in, conv_w, conv_b, gate_b)
    h_f, h_b = mlstm_scan(jnp.concatenate([qk_c, qk_l], 1), jnp.concatenate([v_c, v_l], 1),
                          jnp.concatenate([g_c, g_l], 1), t_ctx // M_CHUNK)
    y_lat = dense(mixer_a_post(h_f, h_b, t_ctx, pl_, ul, head_g, pool_w, pool_s), w_out)
    if not with_ctx_out:
        return y_lat, None
    y_ctx = dense(mixer_a_post(h_f, h_b, 0, pc, uc, head_g, pool_w, pool_s), w_out)
    return y_lat, y_ctx


def axial_rope_tables(n_tokens):
    rows = n_tokens // GRID_W
    row = jnp.repeat(jnp.arange(rows), GRID_W)
    col = jnp.tile(jnp.arange(GRID_W), rows)
    half = ROPE // 2
    inv = ROPE_THETA ** (-jnp.arange(0, half, 2, dtype=F32) / half)
    ang = jnp.stack([row[:, None] * inv, col[:, None] * inv], 1)
    return jnp.cos(ang), jnp.sin(ang)


def apply_axial_rope(x, cos, sin):
    xs = x.reshape(x.shape[:-1] + (2, 2, ROPE // 4))
    x1, x2 = xs[..., 0, :], xs[..., 1, :]
    out = jnp.stack([x1 * cos - x2 * sin, x2 * cos + x1 * sin], -2)
    return out.reshape(x.shape)


LOG2_E = 1.4426950408889634
Q_SCALE = (NOPE + ROPE) ** -0.5 * LOG2_E
ATTN_KEY_SPLITS = 4


def _rms_norm_rows(x, g):
    return x * lax.rsqrt(jnp.mean(jnp.square(x), axis=-1, keepdims=True) + RMS_EPS) * g


def _q_proj_kernel(x_ref, g_ref, w_ref, tab_ref, o_ref, xn_scr):
    @pl.when(pl.program_id(1) == 0)
    def _():
        xn_scr[...] = _rms_norm_rows(x_ref[...], g_ref[...]).astype(BF16)

    o = jnp.dot(xn_scr[...], w_ref[...], preferred_element_type=F32)
    r = o[:, NOPE:] * tab_ref[...]
    rot = r + pltpu.roll(r, ROPE, axis=1)
    o_ref[...] = (jnp.concatenate([o[:, :NOPE], rot[:, :ROPE]], axis=1) * Q_SCALE).astype(o_ref.dtype)


def q_proj(pq, q_norm_g, w_uq, tab):
    nb, t, _ = pq.shape
    w = w_uq.reshape(Q_LORA, A_HEADS, NOPE + ROPE)
    wpe = w[..., NOPE:].reshape(Q_LORA, A_HEADS, 2, 2, ROPE // 4)
    wsw = jnp.stack([-wpe[..., 1, :], wpe[..., 0, :]], -2).reshape(Q_LORA, A_HEADS, ROPE)
    wh = jnp.concatenate([w, wsw], -1).transpose(1, 0, 2).astype(BF16)
    tm = _pick_tile(t, (1024, 512, 256, 128))
    nt = t // tm
    return pl.pallas_call(
        _q_proj_kernel,
        grid=(nb * nt, A_HEADS),
        in_specs=[pl.BlockSpec((tm, Q_LORA), lambda i, h: (i, 0)),
                  pl.BlockSpec((1, Q_LORA), lambda i, h: (0, 0)),
                  pl.BlockSpec((None, Q_LORA, NOPE + 2 * ROPE), lambda i, h: (h, 0, 0)),
                  pl.BlockSpec((tm, 2 * ROPE), lambda i, h: (i % nt, 0))],
        out_specs=pl.BlockSpec((None, None, tm, NOPE + ROPE), lambda i, h: (i // nt, h, i % nt, 0)),
        out_shape=jax.ShapeDtypeStruct((nb, A_HEADS, t, NOPE + ROPE), BF16),
        scratch_shapes=[pltpu.VMEM((tm, Q_LORA), BF16)],
        compiler_params=_params(2),
        name="mla_q_proj",
    )(pq.reshape(nb * t, Q_LORA), q_norm_g.reshape(1, Q_LORA), wh, tab)


def _kv_proj_kernel(x_ref, g_ref, w_ref, kpe_ref, k_ref, v_ref, xn_scr):
    @pl.when(pl.program_id(1) == 0)
    def _():
        xn_scr[...] = _rms_norm_rows(x_ref[...], g_ref[...]).astype(BF16)

    o = jnp.dot(xn_scr[...], w_ref[...], preferred_element_type=F32)
    k_ref[...] = jnp.concatenate([o[:, :NOPE], kpe_ref[...]], axis=1).astype(k_ref.dtype)
    lane = lax.broadcasted_iota(jnp.int32, (o.shape[0], V_DIM), 1)
    ones_col = jnp.where(lane == 0, 1.0, 0.0)
    v_ref[...] = jnp.concatenate([o[:, NOPE:], ones_col], axis=1).astype(v_ref.dtype)


def kv_proj(pkv, kpe, kv_norm_g, w_ukv):
    nb, tk, _ = pkv.shape
    wh = w_ukv.reshape(KV_LORA, A_HEADS, NOPE + V_DIM).transpose(1, 0, 2).astype(BF16)
    tm = _pick_tile(tk, (2176, 1024, 512, 256, 128))
    nt = tk // tm
    return pl.pallas_call(
        _kv_proj_kernel,
        grid=(nb * nt, A_HEADS),
        in_specs=[pl.BlockSpec((tm, KV_LORA), lambda i, h: (i, 0)),
                  pl.BlockSpec((1, KV_LORA), lambda i, h: (0, 0)),
                  pl.BlockSpec((None, KV_LORA, NOPE + V_DIM), lambda i, h: (h, 0, 0)),
                  pl.BlockSpec((tm, ROPE), lambda i, h: (i, 0))],
        out_specs=[pl.BlockSpec((None, None, tm, NOPE + ROPE), lambda i, h: (i // nt, h, i % nt, 0)),
                   pl.BlockSpec((None, None, tm, 2 * V_DIM), lambda i, h: (i // nt, h, i % nt, 0))],
        out_shape=[jax.ShapeDtypeStruct((nb, A_HEADS, tk, NOPE + ROPE), BF16),
                   jax.ShapeDtypeStruct((nb, A_HEADS, tk, 2 * V_DIM), BF16)],
        scratch_shapes=[pltpu.VMEM((tm, KV_LORA), BF16)],
        compiler_params=_params(2),
        name="mla_kv_proj",
    )(pkv.reshape(nb * tk, KV_LORA), kv_norm_g.reshape(1, KV_LORA), wh, kpe.reshape(nb * tk, ROPE))


def _attn_kernel(q_ref, k_ref, v_ref, o_ref):
    q = q_ref[...]
    n_lanes = k_ref.shape[0] // 128
    sizes = [(n_lanes // ATTN_KEY_SPLITS + (c < n_lanes % ATTN_KEY_SPLITS)) * 128 for c in range(ATTN_KEY_SPLITS)]
    m = o = None
    lo = 0
    for size in filter(None, sizes):
        ks = slice(lo, lo + size)
        lo += size
        s = lax.dot_general(q, k_ref[ks, :], (((1,), (1,)), ((), ())), preferred_element_type=F32)
        mc = jnp.max(s, axis=-1, keepdims=True)
        m_new = mc if m is None else jnp.maximum(m, mc)
        p = jnp.exp2((s - m_new).astype(BF16))
        oc = jnp.dot(p, v_ref[ks, :], preferred_element_type=F32)
        o = oc if o is None else o * jnp.exp2(m - m_new) + oc
        m = m_new
    o_ref[...] = (o[:, :V_DIM] / o[:, V_DIM:V_DIM + 1]).astype(o_ref.dtype)


def attend(q, k, vh):
    nb, _, tq, _ = q.shape
    tk = k.shape[2]
    bq = _pick_tile(tq, (1024, 512, 256, 128))
    return pl.pallas_call(
        _attn_kernel,
        grid=(nb, A_HEADS, tq // bq),
        in_specs=[pl.BlockSpec((None, None, bq, NOPE + ROPE), lambda b, h, i: (b, h, i, 0)),
                  pl.BlockSpec((None, None, tk, NOPE + ROPE), lambda b, h, i: (b, h, 0, 0)),
                  pl.BlockSpec((None, None, tk, 2 * V_DIM), lambda b, h, i: (b, h, 0, 0))],
        out_specs=pl.BlockSpec((None, bq, V_DIM), lambda b, h, i: (b, i, h)),
        out_shape=jax.ShapeDtypeStruct((nb, tq, A_HEADS * V_DIM), BF16),
        compiler_params=_params(3),
        name="mla_attention",
    )(q, k, vh)


def mixer_mla(h_lat, h_ctx, w_in, q_norm_g, kv_norm_g, w_uq, w_ukv, w_o, with_ctx_out):
    t_lat, t_ctx = h_lat.shape[1], h_ctx.shape[1]
    cos, sin = axial_rope_tables(t_lat)
    cos_full = jnp.repeat(cos, 2, axis=1).reshape(t_lat, ROPE)
    sin_full = jnp.repeat(sin, 2, axis=1).reshape(t_lat, ROPE)
    pq_l, pkv_l, pr_l = dense(h_lat, w_in, splits=(Q_LORA, KV_LORA, ROPE))
    if with_ctx_out:
        pq_c, pkv_c, pr_c = dense(h_ctx, w_in, splits=(Q_LORA, KV_LORA, ROPE))
    else:
        pkv_c, pr_c = dense(h_ctx, w_in[:, Q_LORA:], splits=(KV_LORA, ROPE))
    kpe = jnp.concatenate([pr_c, apply_axial_rope(pr_l, cos, sin)], 1)
    k, vh = kv_proj(jnp.concatenate([pkv_c, pkv_l], 1), kpe, kv_norm_g, w_ukv)
    q_l = q_proj(pq_l, q_norm_g, w_uq, jnp.concatenate([cos_full, sin_full], -1))
    y_lat = dense(attend(q_l, k, vh), w_o)
    if not with_ctx_out:
        return y_lat, None
    no_rot = jnp.concatenate([jnp.ones((t_ctx, ROPE), F32), jnp.zeros((t_ctx, ROPE), F32)], -1)
    q_c = q_proj(pq_c, q_norm_g, w_uq, no_rot)
    y_ctx = dense(attend(q_c, k[:, :, :t_ctx], vh[:, :, :t_ctx]), w_o)
    return y_lat, y_ctx


MOE_TF = 256
MOE_TN = 256
MOE_NF = D_MODEL // MOE_TF
MOE_NN = D_MODEL // MOE_TN


def _moe_kernel(rows_ref, h_hbm, gate_ref, w1_ref, w3_ref, w2_ref, f_in_hbm, f_hbm,
                xg, xb, hmid, ybuf, stage, sem_g, sem_r, sem_w):
    del f_in_hbm
    e = pl.program_id(0)
    j = pl.program_id(1)
    n_e = pl.num_programs(0)
    m = xg.shape[0]
    per_step = m // MOE_NF
    last_step = MOE_NF + MOE_NN - 1

    prev_e = jnp.maximum(e - 1, 0)
    next_e = jnp.minimum(e + 1, n_e - 1)

    def issue(body, n, inline):
        if inline:
            for i in range(n):
                body(i, 0)
        else:
            lax.fori_loop(0, n, body, 0, unroll=8)

    def gather(expert, lo, n, inline=False):
        def body(i, carry):
            r = rows_ref[expert * m + lo + i]
            pltpu.make_async_copy(h_hbm.at[pl.ds(r, 1), :], xg.at[pl.ds(lo + i, 1), :], sem_g.at[0]).start()
            return carry
        issue(body, n, inline)

    def f_rows(expert, lo, to_vmem, inline=False):
        def body(i, carry):
            r = rows_ref[expert * m + lo + i]
            if to_vmem:
                pltpu.make_async_copy(f_hbm.at[pl.ds(r, 1), :], stage.at[pl.ds(i, 1), :], sem_r.at[0]).start()
            else:
                pltpu.make_async_copy(stage.at[pl.ds(i, 1), :], f_hbm.at[pl.ds(r, 1), :], sem_w.at[0]).start()
            return carry
        issue(body, per_step, inline)

    def wait_all(sem):
        pltpu.make_async_copy(stage, stage, sem.at[0]).wait()

    def add_piece(expert, lo, inline=False):
        wait_all(sem_r)
        if not isinstance(lo, int):
            lo = pl.multiple_of(lo, 16)
        for n in range(MOE_NN):
            cols = slice(n * MOE_TN, (n + 1) * MOE_TN)
            stage[:, cols] += ybuf[n, pl.ds(lo, per_step), :].astype(F32)
        f_rows(expert, lo, to_vmem=False, inline=inline)

    @pl.when((e == 0) & (j == 0))
    def _():
        gather(0, 0, m)
        ybuf[...] = jnp.zeros_like(ybuf)

    @pl.when((j >= 1) & (j <= MOE_NF))
    def _():
        wait_all(sem_w)

    @pl.when(j == 0)
    def _():
        pltpu.make_async_copy(xg, xg, sem_g.at[0]).wait()
        w = xg[...]
        xb[:, :HALF_D] = pltpu.bitcast(w << 16, F32).astype(BF16)
        xb[:, HALF_D:] = pltpu.bitcast(w & jnp.uint32(0xFFFF0000), F32).astype(BF16)

    @pl.when(j < MOE_NF)
    def _():
        f_rows(prev_e, j * per_step, to_vmem=True, inline=True)
        x = xb[...]
        a = jnp.dot(x, w1_ref[...].astype(BF16), preferred_element_type=F32)
        g = jnp.dot(x, w3_ref[...].astype(BF16), preferred_element_type=F32)
        hmid[j] = (jax.nn.silu(a) * g).astype(BF16)
        add_piece(prev_e, j * per_step, inline=True)

    @pl.when(j >= MOE_NF)
    def _():
        gather(next_e, (j - MOE_NF) * per_step, per_step, inline=True)
        acc = jnp.dot(hmid[0], w2_ref[0:MOE_TF, :].astype(BF16), preferred_element_type=F32)
        for f in range(1, MOE_NF):
            acc += jnp.dot(hmid[f], w2_ref[f * MOE_TF:(f + 1) * MOE_TF, :].astype(BF16),
                           preferred_element_type=F32)
        ybuf[j - MOE_NF] = (acc * gate_ref[...]).astype(BF16)

    @pl.when((e == n_e - 1) & (j == last_step))
    def _():
        pltpu.make_async_copy(xg, xg, sem_g.at[0]).wait()
        for piece in range(MOE_NF):
            f_rows(e, piece * per_step, to_vmem=True)
            add_piece(e, piece * per_step)
            wait_all(sem_w)


def expert_ffn_rows(h_packed, rows, gates, w1, w3, w2, layer):
    n_e, m = rows.shape
    n_rows = h_packed.shape[0]
    any_spec = pl.BlockSpec(memory_space=pl.ANY)
    return pl.pallas_call(
        _moe_kernel,
        grid_spec=pltpu.PrefetchScalarGridSpec(
            num_scalar_prefetch=1,
            grid=(n_e, MOE_NF + MOE_NN),
            in_specs=[any_spec,
                      pl.BlockSpec((None, m, 1), lambda e, j, r: (e, 0, 0)),
                      pl.BlockSpec((None, None, D_MODEL, MOE_TF),
                                   lambda e, j, r: (layer, e, 0, jnp.minimum(j, MOE_NF - 1))),
                      pl.BlockSpec((None, None, D_MODEL, MOE_TF),
                                   lambda e, j, r: (layer, e, 0, jnp.minimum(j, MOE_NF - 1))),
                      pl.BlockSpec((None, None, D_MODEL, MOE_TN),
                                   lambda e, j, r: (layer, e, 0, jnp.maximum(j - MOE_NF, 0))),
                      any_spec],
            out_specs=any_spec,
            scratch_shapes=[pltpu.VMEM((m, HALF_D), jnp.uint32),
                            pltpu.VMEM((m, D_MODEL), BF16),
                            pltpu.VMEM((MOE_NF, m, MOE_TF), BF16),
                            pltpu.VMEM((MOE_NN, m, MOE_TN), BF16),
                            pltpu.VMEM((m // MOE_NF, D_MODEL), F32),
                            pltpu.SemaphoreType.DMA((1,)),
                            pltpu.SemaphoreType.DMA((1,)),
                            pltpu.SemaphoreType.DMA((1,))]),
        out_shape=jax.ShapeDtypeStruct((n_rows, D_MODEL), F32),
        input_output_aliases={6: 0},
        compiler_params=_params(2),
        name="expert_ffn",
    )(rows.reshape(-1), h_packed, gates.reshape(n_e, m, 1), w1, w3, w2, jnp.zeros((n_rows, D_MODEL), F32))


def route(logits, nb):
    t = logits.shape[0] // nb
    cap = EC_FACTOR * t // N_EXPERTS
    aff = jax.nn.softmax(logits.reshape(nb, t, N_EXPERTS), axis=-1)
    return lax.top_k(jnp.swapaxes(aff, 1, 2), cap)


def expert_choice_ffn(sets, nb, w1, w3, w2, layer):
    rows, gates, base = [], [], 0
    for _, logits in sets:
        gate, idx = route(logits, nb)
        t = logits.shape[0] // nb
        rid = base + jnp.arange(nb, dtype=jnp.int32)[:, None, None] * t + idx
        rows.append(rid.transpose(1, 0, 2).reshape(N_EXPERTS, -1))
        gates.append(gate.transpose(1, 0, 2).reshape(N_EXPERTS, -1))
        base += nb * t
    rows = jnp.concatenate(rows, 1)
    h_packed = jnp.concatenate([hp for hp, _ in sets], 0) if len(sets) > 1 else sets[0][0]
    return expert_ffn_rows(h_packed, rows, jnp.concatenate(gates, 1), w1, w3, w2, layer)


def kernel(x, c, ctx, c_ctx, ada_w, ada_b, ln_g, ln_b, ab_w_in, ab_conv_w, ab_conv_b, ab_gate_b,
           ab_head_g, ab_pool_w, ab_pool_s, ab_w_out, mla_w_in, mla_q_norm_g, mla_kv_norm_g,
           mla_w_uq, mla_w_ukv, mla_w_o, moe_router, moe_w1, moe_w3, moe_w2):
    nb, t_lat, _ = x.shape
    t_ctx = ctx.shape[1]
    n_lat, n_ctx = nb * t_lat, nb * t_ctx
    cond = jnp.concatenate([c, c_ctx[None]], 0)
    mods = [adaln(cond, ada_w[i], ada_b[i]) for i in range(DEPTH)]
    mods_l = [[m[:nb] for m in ms] for ms in mods]
    mods_c = [[m[nb:] for m in ms] for ms in mods]
    x_lat, x_ctx = x.reshape(n_lat, D_MODEL), ctx.reshape(n_ctx, D_MODEL)
    h_lat = modulate(x, mods_l[0][0], mods_l[0][1]).astype(BF16)
    h_ctx = modulate(ctx, mods_c[0][0], mods_c[0][1]).astype(BF16)
    for i in range(DEPTH):
        with_ctx = i < DEPTH - 1
        j = i // 2
        _, _, ga_l, shf_l, scf_l, gaf_l = mods_l[i]
        _, _, ga_c, shf_c, scf_c, gaf_c = mods_c[i]
        if i % 2 == 0:
            y_lat, y_ctx = mixer_mlstm_pool(h_lat, h_ctx, ab_w_in[j], ab_conv_w[j], ab_conv_b[j], ab_gate_b[j],
                                            ab_head_g[j], ab_pool_w[j], ab_pool_s[j], ab_w_out[j], with_ctx)
        else:
            y_lat, y_ctx = mixer_mla(h_lat, h_ctx, mla_w_in[j], mla_q_norm_g[j], mla_kv_norm_g[j],
                                     mla_w_uq[j], mla_w_ukv[j], mla_w_o[j], with_ctx)
        x_lat, hp_l, lg_l = post_norm_fused(x_lat, y_lat.reshape(n_lat, D_MODEL), 0, ga_l, ln_g[i, 0], ln_b[i, 0],
                                            t_lat, ffn=(shf_l, scf_l, moe_router[i]))
        sets = [(hp_l, lg_l)]
        if with_ctx:
            x_ctx, hp_c, lg_c = post_norm_fused(x_ctx, y_ctx.reshape(n_ctx, D_MODEL), 0, ga_c, ln_g[i, 0],
                                                ln_b[i, 0], n_ctx, ffn=(shf_c, scf_c, moe_router[i]))
            sets.append((hp_c, lg_c))
        f_all = expert_choice_ffn(sets, nb, moe_w1, moe_w3, moe_w2, i)
        nxt_l = (mods_l[i + 1][0], mods_l[i + 1][1]) if with_ctx else None
        res = post_norm_fused(x_lat, f_all, 0, gaf_l, ln_g[i, 1], ln_b[i, 1], t_lat, nxt=nxt_l)
        x_lat = res[0]
        if with_ctx:
            h_lat = res[1].reshape(nb, t_lat, D_MODEL)
            x_ctx, h_ctx = post_norm_fused(x_ctx, f_all, n_lat, gaf_c, ln_g[i, 1], ln_b[i, 1], n_ctx,
                                           nxt=(mods_c[i + 1][0], mods_c[i + 1][1]))
            h_ctx = h_ctx.reshape(nb, t_ctx, D_MODEL)
    return x_lat.reshape(x.shape)
```

```python
import jax
import jax.numpy as jnp
from jax import lax
from jax.experimental import pallas as pl
from jax.experimental.pallas import tpu as pltpu

D_MODEL = 2048
DEPTH = 2
F32 = jnp.float32
BF16 = jnp.bfloat16
GRID_W = 64

M_HEADS = 4
M_DIM = D_MODEL // 2
M_HEAD_DIM = M_DIM // M_HEADS
M_CHUNK = 128
M_CONV = 3
P_DIM = D_MODEL - M_DIM
P_GROUPS = 4
P_GROUP_DIM = P_DIM // P_GROUPS
POOL_WINDOWS = (2, 4, 8, 16)
A_HEADS = 16
Q_LORA = 1536
KV_LORA = 512
NOPE = 128
ROPE = 64
V_DIM = 128
ROPE_THETA = 10000.0
N_EXPERTS = 16
EC_FACTOR = 2
LN_EPS = 1e-5
RMS_EPS = 1e-6
ALPHA = (2 * DEPTH) ** 0.25

V7X_VMEM_BYTES = 64 * 1024 * 1024
VMEM_LIMIT_BYTES = V7X_VMEM_BYTES - 6 * 1024 * 1024


def _params(n_axes):
    return pltpu.CompilerParams(dimension_semantics=("arbitrary",) * n_axes,
                                vmem_limit_bytes=VMEM_LIMIT_BYTES)


def _mm_kernel(x_ref, w_ref, o_ref):
    o_ref[...] = jnp.dot(x_ref[...].astype(BF16), w_ref[...].astype(BF16),
                         preferred_element_type=F32).astype(o_ref.dtype)


def _pick_tile(n, prefs):
    for t in prefs:
        if n % t == 0:
            return t
    return n


def matmul(x, w, out_dtype=F32):
    if x.ndim == 2:
        return _matmul_grouped(x[None], w[None], out_dtype)[0]
    return _matmul_grouped(x, w, out_dtype)


def _matmul_grouped(x, w, out_dtype):
    g, m, k = x.shape
    n = w.shape[-1]
    tm = _pick_tile(m, (1024, 512, 256, 128))
    tn = _pick_tile(n, (512, 384, 256, 128))
    return pl.pallas_call(
        _mm_kernel,
        grid=(g, n // tn, m // tm),
        in_specs=[pl.BlockSpec((None, tm, k), lambda e, j, i: (e, i, 0)),
                  pl.BlockSpec((None, k, tn), lambda e, j, i: (e, 0, j))],
        out_specs=pl.BlockSpec((None, tm, tn), lambda e, j, i: (e, i, j)),
        out_shape=jax.ShapeDtypeStruct((g, m, n), out_dtype),
        compiler_params=_params(3),
        name="matmul",
    )(x, w)


def dense(x, w, splits=None):
    b, t, k = x.shape
    x2 = x.reshape(b * t, k)
    if splits is None:
        return matmul(x2, w).reshape(b, t, w.shape[-1])
    outs, lo = [], 0
    for width in splits:
        outs.append(matmul(x2, w[:, lo:lo + width]).reshape(b, t, width))
        lo += width
    return outs


def adaln(cond, w, b):
    m = jnp.dot(jax.nn.silu(cond), w, precision=lax.Precision.HIGHEST) + b
    m = m.reshape(cond.shape[0], 6, 1, D_MODEL)
    return [m[:, k] for k in range(6)]


def modulate(x, shift, scale):
    return x * (1.0 + scale) + shift


NORM_ROWS = 256
HALF_D = D_MODEL // 2


def _pack_bf16_pairs(h):
    bits = pltpu.bitcast(h.astype(BF16).astype(F32), jnp.uint32)
    return (bits[:, :HALF_D] >> 16) | (bits[:, HALF_D:] & jnp.uint32(0xFFFF0000))


def _make_post_norm_kernel(with_ffn, with_next):
    def body(*refs):
        x_ref, y_ref, gate_ref, g_ref, b_ref = refs[:5]
        pos = 5
        if with_ffn:
            shf_ref, scf_ref, wr_ref = refs[pos:pos + 3]
            pos += 3
        if with_next:
            shn_ref, scn_ref = refs[pos:pos + 2]
            pos += 2
        outs = refs[pos:]
        z = ALPHA * x_ref[...] + (1.0 + gate_ref[...]) * y_ref[...]
        mu = jnp.mean(z, axis=-1, keepdims=True)
        zc = z - mu
        var = jnp.mean(zc * zc, axis=-1, keepdims=True)
        xn = zc * lax.rsqrt(var + LN_EPS) * g_ref[...] + b_ref[...]
        outs[0][...] = xn
        k = 1
        if with_ffn:
            h = xn * (1.0 + scf_ref[...]) + shf_ref[...]
            outs[k][...] = _pack_bf16_pairs(h)
            outs[k + 1][...] = jnp.dot(h, wr_ref[...], preferred_element_type=F32,
                                       precision=lax.Precision.HIGHEST)
            k += 2
        if with_next:
            outs[k][...] = (xn * (1.0 + scn_ref[...]) + shn_ref[...]).astype(BF16)
    return body


def post_norm_fused(x, y, y_row0, gate, ln_g, ln_b, rows_per_set, ffn=None, nxt=None):
    n = x.shape[0]
    tps = rows_per_set // NORM_ROWS
    off = y_row0 // NORM_ROWS
    row = pl.BlockSpec((NORM_ROWS, D_MODEL), lambda i: (i, 0))
    tab = pl.BlockSpec((None, 1, D_MODEL), lambda i: (i // tps, 0, 0))
    one = pl.BlockSpec((None, 1, D_MODEL), lambda i: (0, 0, 0))
    args = [x, y, gate, ln_g.reshape(1, 1, D_MODEL), ln_b.reshape(1, 1, D_MODEL)]
    in_specs = [row, pl.BlockSpec((NORM_ROWS, D_MODEL), lambda i: (i + off, 0)), tab, one, one]
    out_shape = [jax.ShapeDtypeStruct((n, D_MODEL), F32)]
    out_specs = [row]
    if ffn is not None:
        args += [ffn[0], ffn[1], ffn[2]]
        in_specs += [tab, tab, pl.BlockSpec((D_MODEL, N_EXPERTS), lambda i: (0, 0))]
        out_shape += [jax.ShapeDtypeStruct((n, HALF_D), jnp.uint32), jax.ShapeDtypeStruct((n, N_EXPERTS), F32)]
        out_specs += [pl.BlockSpec((NORM_ROWS, HALF_D), lambda i: (i, 0)),
                      pl.BlockSpec((NORM_ROWS, N_EXPERTS), lambda i: (i, 0))]
    if nxt is not None:
        args += [nxt[0], nxt[1]]
        in_specs += [tab, tab]
        out_shape += [jax.ShapeDtypeStruct((n, D_MODEL), BF16)]
        out_specs += [row]
    return pl.pallas_call(
        _make_post_norm_kernel(ffn is not None, nxt is not None),
        grid=(n // NORM_ROWS,),
        in_specs=in_specs, out_specs=out_specs, out_shape=out_shape,
        compiler_params=_params(1),
        name="post_norm",
    )(*args)


def depthwise_conv_centred(u, w, b):
    up = jnp.pad(u, ((0, 0), (M_CONV // 2, M_CONV // 2), (0, 0)))
    t = u.shape[1]
    out = sum(up[:, j:j + t] * w[j] for j in range(M_CONV))
    return out + b


def ab_project(h, w_in, conv_w, conv_b, gate_b):
    p, pg, u = dense(h, w_in, splits=(4 * M_DIM, 4 * M_HEADS, P_DIM))
    qk = jax.nn.silu(depthwise_conv_centred(p[..., :2 * M_DIM], conv_w, conv_b))
    kscale = jnp.concatenate([jnp.ones((M_DIM,), F32), jnp.full((M_DIM,), M_HEAD_DIM ** -0.5, F32)])
    qk = (qk * kscale).astype(BF16)
    v = p[..., 2 * M_DIM:3 * M_DIM].astype(BF16)
    gates = (pg + gate_b).reshape(p.shape[:2] + (2, 2, M_HEADS))
    li = gates[..., 0, :]
    lf = jax.nn.log_sigmoid(gates[..., 1, :])
    grow = jnp.stack([li, lf], 3).reshape(p.shape[:2] + (4 * M_HEADS,))
    return qk, v, p, grow, u


def _mlstm_chain(q, k, v, li_row, lf_row, c_ref, n_ref, m_ref, reverse):
    L = M_CHUNK
    ti = lax.broadcasted_iota(jnp.int32, (L, L), 0)
    si = lax.broadcasted_iota(jnp.int32, (L, L), 1)
    mask = (si >= ti) if reverse else (si <= ti)
    mask_t = (ti >= si) if reverse else (ti <= si)
    lf_b = jnp.broadcast_to(lf_row, (L, L))
    lf_bt = lf_b.T
    li_col = jnp.broadcast_to(li_row, (L, L)).T[:, 0:1]
    b_col = jnp.sum(jnp.where(mask, lf_b, 0.0), axis=1, keepdims=True)
    b_row = jnp.sum(jnp.where(mask_t, lf_bt, 0.0), axis=0, keepdims=True)
    a_row = li_row - b_row
    dmat = jnp.where(mask, b_col + a_row, -jnp.inf)
    m_prev = m_ref[...][:, 0:1]
    inter = b_col + m_prev
    m_t = jnp.maximum(inter, jnp.max(dmat, axis=1, keepdims=True))
    w_intra = jnp.exp(dmat - m_t)
    w_inter = jnp.exp(inter - m_t)
    s = lax.dot_general(q, k, (((1,), (1,)), ((), ())), preferred_element_type=F32) * w_intra
    c_prev = c_ref[...]
    n_prev = n_ref[...]
    num = (w_inter * jnp.dot(q, c_prev.astype(BF16), preferred_element_type=F32)
           + jnp.dot(s.astype(BF16), v, preferred_element_type=F32))
    den = (w_inter * jnp.sum(q.astype(F32) * n_prev, axis=1, keepdims=True)
           + jnp.sum(s, axis=1, keepdims=True))
    h = num / jnp.maximum(jnp.abs(den), jnp.exp(-m_t))
    bl = jnp.sum(lf_row, axis=1, keepdims=True)
    m_new = jnp.maximum(bl + m_prev, jnp.max(bl + a_row, axis=1, keepdims=True))
    decay = jnp.exp(bl + m_prev - m_new)
    wg_col = jnp.exp(bl - b_col + li_col - m_new)
    kw = k.astype(F32) * wg_col
    c_ref[...] = decay * c_prev + jnp.dot(kw.T.astype(BF16), v, preferred_element_type=F32)
    n_ref[...] = decay * n_prev + jnp.sum(kw, axis=0, keepdims=True)
    m_ref[...] = jnp.broadcast_to(m_new, m_ref.shape)
    return h


def _mlstm_kernel(qf_ref, kf_ref, vf_ref, gf_ref, qb_ref, kb_ref, vb_ref, gb_ref,
                  hf_ref, hb_ref, c_scr, n_scr, m_scr):
    @pl.when(pl.program_id(1) == 0)
    def _():
        c_scr[...] = jnp.zeros_like(c_scr)
        n_scr[...] = jnp.zeros_like(n_scr)
        m_scr[...] = jnp.zeros_like(m_scr)

    dirs = ((qf_ref, kf_ref, vf_ref, gf_ref, hf_ref), (qb_ref, kb_ref, vb_ref, gb_ref, hb_ref))
    for d, (q_ref, k_ref, v_ref, g_ref, h_ref) in enumerate(dirs):
        for hd in range(M_HEADS):
            cols = slice(hd * M_HEAD_DIM, (hd + 1) * M_HEAD_DIM)
            row_i = d * 2 * M_HEADS + hd
            row_f = row_i + M_HEADS
            h = _mlstm_chain(q_ref[:, cols], k_ref[:, cols], v_ref[:, cols],
                             g_ref[row_i:row_i + 1, :], g_ref[row_f:row_f + 1, :],
                             c_scr.at[d, hd], n_scr.at[d, hd], m_scr.at[d, hd], reverse=(d == 1))
            h_ref[:, cols] = h.astype(h_ref.dtype)


def mlstm_scan(qk, v, grow, n_ctx_chunks):
    nb, tt, _ = v.shape
    nc = tt // M_CHUNK
    g = grow.reshape(nb, nc, M_CHUNK, 4 * M_HEADS).transpose(0, 1, 3, 2)

    def fwd(c):
        return c

    def bwd(c):
        return jnp.where(c < n_ctx_chunks, n_ctx_chunks - 1 - c, nc + n_ctx_chunks - 1 - c)

    def specs(cm):
        return [pl.BlockSpec((None, M_CHUNK, M_DIM), lambda b, c: (b, cm(c), 0)),
                pl.BlockSpec((None, M_CHUNK, M_DIM), lambda b, c: (b, cm(c), 1)),
                pl.BlockSpec((None, M_CHUNK, M_DIM), lambda b, c: (b, cm(c), 0)),
                pl.BlockSpec((None, None, 4 * M_HEADS, M_CHUNK), lambda b, c: (b, cm(c), 0, 0))]

    out_sds = jax.ShapeDtypeStruct((nb, tt, M_DIM), F32)
    return pl.pallas_call(
        _mlstm_kernel,
        grid=(nb, nc),
        in_specs=specs(fwd) + specs(bwd),
        out_specs=[pl.BlockSpec((None, M_CHUNK, M_DIM), lambda b, c: (b, fwd(c), 0)),
                   pl.BlockSpec((None, M_CHUNK, M_DIM), lambda b, c: (b, bwd(c), 0))],
        out_shape=[out_sds, out_sds],
        scratch_shapes=[pltpu.VMEM((2, M_HEADS, M_HEAD_DIM, M_HEAD_DIM), F32),
                        pltpu.VMEM((2, M_HEADS, 1, M_HEAD_DIM), F32),
                        pltpu.VMEM((2, M_HEADS, 1, 128), F32)],
        compiler_params=_params(2),
        name="mlstm_scan",
    )(qk, qk, v, g, qk, qk, v, g)


POST_ROWS = 256
POOL_HALO = 8


def _mixer_a_post_kernel(hf_ref, hb_ref, o_ref, u_ref, uprev_ref, unext_ref, hg_ref, pw_ref, ps_ref, out_ref):
    i = pl.program_id(1)
    n_tiles = pl.num_programs(1)
    t_seq = n_tiles * POST_ROWS
    hs = hf_ref[...] + hb_ref[...]
    gate = jax.nn.sigmoid(o_ref[...])
    for hd in range(M_HEADS):
        cols = slice(hd * M_HEAD_DIM, (hd + 1) * M_HEAD_DIM)
        x = hs[:, cols]
        xc = x - jnp.mean(x, axis=-1, keepdims=True)
        var = jnp.mean(xc * xc, axis=-1, keepdims=True)
        out_ref[:, cols] = (xc * lax.rsqrt(var + LN_EPS) * hg_ref[:, cols] * gate[:, cols]).astype(out_ref.dtype)
    u = u_ref[...]
    prev = jnp.where(i == 0, 0.0, uprev_ref[...])
    nxt = jnp.where(i == n_tiles - 1, 0.0, unext_ref[...])
    ext = jnp.concatenate([prev, u, nxt], axis=0)
    n_ext = POST_ROWS + 2 * POOL_HALO
    t = i * POST_ROWS + lax.broadcasted_iota(jnp.int32, (POST_ROWS, 1), 0)
    for g, w in enumerate(POOL_WINDOWS):
        cols = slice(g * P_GROUP_DIM, (g + 1) * P_GROUP_DIM)
        a = ext[:, cols]
        a = a + pltpu.roll(a, 1, axis=0)
        step = 1
        while 2 * step < w:
            a = pltpu.roll(a, step, axis=0) + pltpu.roll(a, n_ext - step, axis=0)
            step *= 2
        count = (jnp.minimum(t + w // 2, t_seq) - jnp.maximum(t - w // 2, 0)).astype(F32)
        pooled = a[POOL_HALO:POOL_HALO + POST_ROWS] / count - u[:, cols]
        y = jnp.dot(pooled.astype(BF16), pw_ref[g].astype(BF16), preferred_element_type=F32)
        out_ref[:, M_DIM + g * P_GROUP_DIM:M_DIM + (g + 1) * P_GROUP_DIM] = (y * ps_ref[:, cols]).astype(out_ref.dtype)


def mixer_a_post(h_f, h_b, h_row0, p, u, head_g, pool_w, pool_s):
    nb, t, _ = u.shape
    nt = t // POST_ROWS
    off = h_row0 // POST_ROWS
    per = POST_ROWS // POOL_HALO
    return pl.pallas_call(
        _mixer_a_post_kernel,
        grid=(nb, nt),
        in_specs=[pl.BlockSpec((None, POST_ROWS, M_DIM), lambda b, i: (b, i + off, 0)),
                  pl.BlockSpec((None, POST_ROWS, M_DIM), lambda b, i: (b, i + off, 0)),
                  pl.BlockSpec((None, POST_ROWS, M_DIM), lambda b, i: (b, i, 3)),
                  pl.BlockSpec((None, POST_ROWS, P_DIM), lambda b, i: (b, i, 0)),
                  pl.BlockSpec((None, POOL_HALO, P_DIM), lambda b, i: (b, jnp.maximum(i * per - 1, 0), 0)),
                  pl.BlockSpec((None, POOL_HALO, P_DIM), lambda b, i: (b, jnp.minimum((i + 1) * per, nt * per - 1), 0)),
                  pl.BlockSpec((1, M_DIM), lambda b, i: (0, 0)),
                  pl.BlockSpec((P_GROUPS, P_GROUP_DIM, P_GROUP_DIM), lambda b, i: (0, 0, 0)),
                  pl.BlockSpec((1, P_DIM), lambda b, i: (0, 0))],
        out_specs=pl.BlockSpec((None, POST_ROWS, D_MODEL), lambda b, i: (b, i, 0)),
        out_shape=jax.ShapeDtypeStruct((nb, t, D_MODEL), BF16),
        compiler_params=_params(2),
        name="mixer_a_post",
    )(h_f, h_b, p, u, u, u, head_g.reshape(1, M_DIM), pool_w, pool_s.reshape(1, P_DIM))


def mixer_mlstm_pool(h_lat, h_ctx, w_in, conv_w, conv_b, gate_b, head_g, pool_w, pool_s, w_out, with_ctx_out):
    t_ctx = h_ctx.shape[1]
    qk_c, v_c, pc, g_c, uc = ab_project(h_ctx, w_in, conv_w, conv_b, gate_b)
    qk_l, v_l, pl_, g_l, ul = ab_project(h_lat, w_in, conv_w, conv_b, gate_b)
    h_f, h_b = mlstm_scan(jnp.concatenate([qk_c, qk_l], 1), jnp.concatenate([v_c, v_l], 1),
                          jnp.concatenate([g_c, g_l], 1), t_ctx // M_CHUNK)
    y_lat = dense(mixer_a_post(h_f, h_b, t_ctx, pl_, ul, head_g, pool_w, pool_s), w_out)
    if not with_ctx_out:
        return y_lat, None
    y_ctx = dense(mixer_a_post(h_f, h_b, 0, pc, uc, head_g, pool_w, pool_s), w_out)
    return y_lat, y_ctx


def axial_rope_tables(n_tokens):
    rows = n_tokens // GRID_W
    row = jnp.repeat(jnp.arange(rows), GRID_W)
    col = jnp.tile(jnp.arange(GRID_W), rows)
    half = ROPE // 2
    inv = ROPE_THETA ** (-jnp.arange(0, half, 2, dtype=F32) / half)
    ang = jnp.stack([row[:, None] * inv, col[:, None] * inv], 1)
    return jnp.cos(ang), jnp.sin(ang)


def apply_axial_rope(x, cos, sin):
    xs = x.reshape(x.shape[:-1] + (2, 2, ROPE // 4))
    x1, x2 = xs[..., 0, :], xs[..., 1, :]
    out = jnp.stack([x1 * cos - x2 * sin, x2 * cos + x1 * sin], -2)
    return out.reshape(x.shape)


LOG2_E = 1.4426950408889634
Q_SCALE = (NOPE + ROPE) ** -0.5 * LOG2_E
ATTN_KEY_SPLITS = 4


def _rms_norm_rows(x, g):
    return x * lax.rsqrt(jnp.mean(jnp.square(x), axis=-1, keepdims=True) + RMS_EPS) * g


def _q_proj_kernel(x_ref, g_ref, w_ref, tab_ref, o_ref, xn_scr):
    @pl.when(pl.program_id(1) == 0)
    def _():
        xn_scr[...] = _rms_norm_rows(x_ref[...], g_ref[...]).astype(BF16)

    o = jnp.dot(xn_scr[...], w_ref[...], preferred_element_type=F32)
    r = o[:, NOPE:] * tab_ref[...]
    rot = r + pltpu.roll(r, ROPE, axis=1)
    o_ref[...] = (jnp.concatenate([o[:, :NOPE], rot[:, :ROPE]], axis=1) * Q_SCALE).astype(o_ref.dtype)


def q_proj(pq, q_norm_g, w_uq, tab):
    nb, t, _ = pq.shape
    w = w_uq.reshape(Q_LORA, A_HEADS, NOPE + ROPE)
    wpe = w[..., NOPE:].reshape(Q_LORA, A_HEADS, 2, 2, ROPE // 4)
    wsw = jnp.stack([-wpe[..., 1, :], wpe[..., 0, :]], -2).reshape(Q_LORA, A_HEADS, ROPE)
    wh = jnp.concatenate([w, wsw], -1).transpose(1, 0, 2).astype(BF16)
    tm = _pick_tile(t, (1024, 512, 256, 128))
    nt = t // tm
    return pl.pallas_call(
        _q_proj_kernel,
        grid=(nb * nt, A_HEADS),
        in_specs=[pl.BlockSpec((tm, Q_LORA), lambda i, h: (i, 0)),
                  pl.BlockSpec((1, Q_LORA), lambda i, h: (0, 0)),
                  pl.BlockSpec((None, Q_LORA, NOPE + 2 * ROPE), lambda i, h: (h, 0, 0)),
                  pl.BlockSpec((tm, 2 * ROPE), lambda i, h: (i % nt, 0))],
        out_specs=pl.BlockSpec((None, None, tm, NOPE + ROPE), lambda i, h: (i // nt, h, i % nt, 0)),
        out_shape=jax.ShapeDtypeStruct((nb, A_HEADS, t, NOPE + ROPE), BF16),
        scratch_shapes=[pltpu.VMEM((tm, Q_LORA), BF16)],
        compiler_params=_params(2),
        name="mla_q_proj",
    )(pq.reshape(nb * t, Q_LORA), q_norm_g.reshape(1, Q_LORA), wh, tab)


def _kv_proj_kernel(x_ref, g_ref, w_ref, kpe_ref, k_ref, v_ref, xn_scr):
    @pl.when(pl.program_id(1) == 0)
    def _():
        xn_scr[...] = _rms_norm_rows(x_ref[...], g_ref[...]).astype(BF16)

    o = jnp.dot(xn_scr[...], w_ref[...], preferred_element_type=F32)
    k_ref[...] = jnp.concatenate([o[:, :NOPE], kpe_ref[...]], axis=1).astype(k_ref.dtype)
    lane = lax.broadcasted_iota(jnp.int32, (o.shape[0], V_DIM), 1)
    ones_col = jnp.where(lane == 0, 1.0, 0.0)
    v_ref[...] = jnp.concatenate([o[:, NOPE:], ones_col], axis=1).astype(v_ref.dtype)


def kv_proj(pkv, kpe, kv_norm_g, w_ukv):
    nb, tk, _ = pkv.shape
    wh = w_ukv.reshape(KV_LORA, A_HEADS, NOPE + V_DIM).transpose(1, 0, 2).astype(BF16)
    tm = _pick_tile(tk, (2176, 1024, 512, 256, 128))
    nt = tk // tm
    return pl.pallas_call(
        _kv_proj_kernel,
        grid=(nb * nt, A_HEADS),
        in_specs=[pl.BlockSpec((tm, KV_LORA), lambda i, h: (i, 0)),
                  pl.BlockSpec((1, KV_LORA), lambda i, h: (0, 0)),
                  pl.BlockSpec((None, KV_LORA, NOPE + V_DIM), lambda i, h: (h, 0, 0)),
                  pl.BlockSpec((tm, ROPE), lambda i, h: (i, 0))],
        out_specs=[pl.BlockSpec((None, None, tm, NOPE + ROPE), lambda i, h: (i // nt, h, i % nt, 0)),
                   pl.BlockSpec((None, None, tm, 2 * V_DIM), lambda i, h: (i // nt, h, i % nt, 0))],
        out_shape=[jax.ShapeDtypeStruct((nb, A_HEADS, tk, NOPE + ROPE), BF16),
                   jax.ShapeDtypeStruct((nb, A_HEADS, tk, 2 * V_DIM), BF16)],
        scratch_shapes=[pltpu.VMEM((tm, KV_LORA), BF16)],
        compiler_params=_params(2),
        name="mla_kv_proj",
    )(pkv.reshape(nb * tk, KV_LORA), kv_norm_g.reshape(1, KV_LORA), wh, kpe.reshape(nb * tk, ROPE))


def _attn_kernel(q_ref, k_ref, v_ref, o_ref):
    q = q_ref[...]
    n_lanes = k_ref.shape[0] // 128
    sizes = [(n_lanes // ATTN_KEY_SPLITS + (c < n_lanes % ATTN_KEY_SPLITS)) * 128 for c in range(ATTN_KEY_SPLITS)]
    m = o = None
    lo = 0
    for size in filter(None, sizes):
        ks = slice(lo, lo + size)
        lo += size
        s = lax.dot_general(q, k_ref[ks, :], (((1,), (1,)), ((), ())), preferred_element_type=F32)
        mc = jnp.max(s, axis=-1, keepdims=True)
        m_new = mc if m is None else jnp.maximum(m, mc)
        p = jnp.exp2((s - m_new).astype(BF16))
        oc = jnp.dot(p, v_ref[ks, :], preferred_element_type=F32)
        o = oc if o is None else o * jnp.exp2(m - m_new) + oc
        m = m_new
    o_ref[...] = (o[:, :V_DIM] / o[:, V_DIM:V_DIM + 1]).astype(o_ref.dtype)


def attend(q, k, vh):
    nb, _, tq, _ = q.shape
    tk = k.shape[2]
    bq = _pick_tile(tq, (1024, 512, 256, 128))
    return pl.pallas_call(
        _attn_kernel,
        grid=(nb, A_HEADS, tq // bq),
        in_specs=[pl.BlockSpec((None, None, bq, NOPE + ROPE), lambda b, h, i: (b, h, i, 0)),
                  pl.BlockSpec((None, None, tk, NOPE + ROPE), lambda b, h, i: (b, h, 0, 0)),
                  pl.BlockSpec((None, None, tk, 2 * V_DIM), lambda b, h, i: (b, h, 0, 0))],
        out_specs=pl.BlockSpec((None, bq, V_DIM), lambda b, h, i: (b, i, h)),
        out_shape=jax.ShapeDtypeStruct((nb, tq, A_HEADS * V_DIM), BF16),
        compiler_params=_params(3),
        name="mla_attention",
    )(q, k, vh)


def mixer_mla(h_lat, h_ctx, w_in, q_norm_g, kv_norm_g, w_uq, w_ukv, w_o, with_ctx_out):
    t_lat, t_ctx = h_lat.shape[1], h_ctx.shape[1]
    cos, sin = axial_rope_tables(t_lat)
    cos_full = jnp.repeat(cos, 2, axis=1).reshape(t_lat, ROPE)
    sin_full = jnp.repeat(sin, 2, axis=1).reshape(t_lat, ROPE)
    pq_l, pkv_l, pr_l = dense(h_lat, w_in, splits=(Q_LORA, KV_LORA, ROPE))
    if with_ctx_out:
        pq_c, pkv_c, pr_c = dense(h_ctx, w_in, splits=(Q_LORA, KV_LORA, ROPE))
    else:
        pkv_c, pr_c = dense(h_ctx, w_in[:, Q_LORA:], splits=(KV_LORA, ROPE))
    kpe = jnp.concatenate([pr_c, apply_axial_rope(pr_l, cos, sin)], 1)
    k, vh = kv_proj(jnp.concatenate([pkv_c, pkv_l], 1), kpe, kv_norm_g, w_ukv)
    q_l = q_proj(pq_l, q_norm_g, w_uq, jnp.concatenate([cos_full, sin_full], -1))
    y_lat = dense(attend(q_l, k, vh), w_o)
    if not with_ctx_out:
        return y_lat, None
    no_rot = jnp.concatenate([jnp.ones((t_ctx, ROPE), F32), jnp.zeros((t_ctx, ROPE), F32)], -1)
    q_c = q_proj(pq_c, q_norm_g, w_uq, no_rot)
    y_ctx = dense(attend(q_c, k[:, :, :t_ctx], vh[:, :, :t_ctx]), w_o)
    return y_lat, y_ctx


MOE_TF = 256
MOE_TN = 256
MOE_NF = D_MODEL // MOE_TF
MOE_NN = D_MODEL // MOE_TN


def _moe_kernel(rows_ref, h_hbm, gate_ref, w1_ref, w3_ref, w2_ref, f_in_hbm, f_hbm,
                xg, xb, hmid, ybuf, stage, sem_g, sem_r, sem_w):
    del f_in_hbm
    e = pl.program_id(0)
    j = pl.program_id(1)
    n_e = pl.num_programs(0)
    m = xg.shape[0]
    per_step = m // MOE_NF
    last_step = MOE_NF + MOE_NN - 1

    prev_e = jnp.maximum(e - 1, 0)
    next_e = jnp.minimum(e + 1, n_e - 1)

    def issue(body, n, inline):
        if inline:
            for i in range(n):
                body(i, 0)
        else:
            lax.fori_loop(0, n, body, 0, unroll=8)

    def gather(expert, lo, n, inline=False):
        def body(i, carry):
            r = rows_ref[expert * m + lo + i]
            pltpu.make_async_copy(h_hbm.at[pl.ds(r, 1), :], xg.at[pl.ds(lo + i, 1), :], sem_g.at[0]).start()
            return carry
        issue(body, n, inline)

    def f_rows(expert, lo, to_vmem, inline=False):
        def body(i, carry):
            r = rows_ref[expert * m + lo + i]
            if to_vmem:
                pltpu.make_async_copy(f_hbm.at[pl.ds(r, 1), :], stage.at[pl.ds(i, 1), :], sem_r.at[0]).start()
            else:
                pltpu.make_async_copy(stage.at[pl.ds(i, 1), :], f_hbm.at[pl.ds(r, 1), :], sem_w.at[0]).start()
            return carry
        issue(body, per_step, inline)

    def wait_all(sem):
        pltpu.make_async_copy(stage, stage, sem.at[0]).wait()

    def add_piece(expert, lo, inline=False):
        wait_all(sem_r)
        if not isinstance(lo, int):
            lo = pl.multiple_of(lo, 16)
        for n in range(MOE_NN):
            cols = slice(n * MOE_TN, (n + 1) * MOE_TN)
            stage[:, cols] += ybuf[n, pl.ds(lo, per_step), :].astype(F32)
        f_rows(expert, lo, to_vmem=False, inline=inline)

    @pl.when((e == 0) & (j == 0))
    def _():
        gather(0, 0, m)
        ybuf[...] = jnp.zeros_like(ybuf)

    @pl.when((j >= 1) & (j <= MOE_NF))
    def _():
        wait_all(sem_w)

    @pl.when(j == 0)
    def _():
        pltpu.make_async_copy(xg, xg, sem_g.at[0]).wait()
        w = xg[...]
        xb[:, :HALF_D] = pltpu.bitcast(w << 16, F32).astype(BF16)
        xb[:, HALF_D:] = pltpu.bitcast(w & jnp.uint32(0xFFFF0000), F32).astype(BF16)

    @pl.when(j < MOE_NF)
    def _():
        f_rows(prev_e, j * per_step, to_vmem=True, inline=True)
        a = jnp.dot(xb[...], w1_ref[...].astype(BF16), preferred_element_type=F32)
        a = jax.nn.silu(a)
        add_piece(prev_e, j * per_step, inline=True)
        g = jnp.dot(xb[...], w3_ref[...].astype(BF16), preferred_element_type=F32)
        hmid[j] = (a * g).astype(BF16)

    @pl.when(j >= MOE_NF)
    def _():
        gather(next_e, (j - MOE_NF) * per_step, per_step, inline=True)
        acc = jnp.dot(hmid[0], w2_ref[0:MOE_TF, :].astype(BF16), preferred_element_type=F32)
        for f in range(1, MOE_NF):
            acc += jnp.dot(hmid[f], w2_ref[f * MOE_TF:(f + 1) * MOE_TF, :].astype(BF16),
                           preferred_element_type=F32)
        ybuf[j - MOE_NF] = (acc * gate_ref[...]).astype(BF16)

    @pl.when((e == n_e - 1) & (j == last_step))
    def _():
        pltpu.make_async_copy(xg, xg, sem_g.at[0]).wait()
        for piece in range(MOE_NF):
            f_rows(e, piece * per_step, to_vmem=True)
            add_piece(e, piece * per_step)
            wait_all(sem_w)


def expert_ffn_rows(h_packed, rows, gates, w1, w3, w2, layer):
    n_e, m = rows.shape
    n_rows = h_packed.shape[0]
    any_spec = pl.BlockSpec(memory_space=pl.ANY)
    return pl.pallas_call(
        _moe_kernel,
        grid_spec=pltpu.PrefetchScalarGridSpec(
            num_scalar_prefetch=1,
            grid=(n_e, MOE_NF + MOE_NN),
            in_specs=[any_spec,
                      pl.BlockSpec((None, m, 1), lambda e, j, r: (e, 0, 0)),
                      pl.BlockSpec((None, None, D_MODEL, MOE_TF),
                                   lambda e, j, r: (layer, e, 0, jnp.minimum(j, MOE_NF - 1))),
                      pl.BlockSpec((None, None, D_MODEL, MOE_TF),
                                   lambda e, j, r: (layer, e, 0, jnp.minimum(j, MOE_NF - 1))),
                      pl.BlockSpec((None, None, D_MODEL, MOE_TN),
                                   lambda e, j, r: (layer, e, 0, jnp.maximum(j - MOE_NF, 0))),
                      any_spec],
            out_specs=any_spec,
            scratch_shapes=[pltpu.VMEM((m, HALF_D), jnp.uint32),
                            pltpu.VMEM((m, D_MODEL), BF16),
                            pltpu.VMEM((MOE_NF, m, MOE_TF), BF16),
                            pltpu.VMEM((MOE_NN, m, MOE_TN), BF16),
                            pltpu.VMEM((m // MOE_NF, D_MODEL), F32),
                            pltpu.SemaphoreType.DMA((1,)),
                            pltpu.SemaphoreType.DMA((1,)),
                            pltpu.SemaphoreType.DMA((1,))]),
        out_shape=jax.ShapeDtypeStruct((n_rows, D_MODEL), F32),
        input_output_aliases={6: 0},
        compiler_params=_params(2),
        name="expert_ffn",
    )(rows.reshape(-1), h_packed, gates.reshape(n_e, m, 1), w1, w3, w2, jnp.zeros((n_rows, D_MODEL), F32))


def route(logits, nb):
    t = logits.shape[0] // nb
    cap = EC_FACTOR * t // N_EXPERTS
    aff = jax.nn.softmax(logits.reshape(nb, t, N_EXPERTS), axis=-1)
    return lax.top_k(jnp.swapaxes(aff, 1, 2), cap)


def expert_choice_ffn(sets, nb, w1, w3, w2, layer):
    rows, gates, base = [], [], 0
    for _, logits in sets:
        gate, idx = route(logits, nb)
        t = logits.shape[0] // nb
        rid = base + jnp.arange(nb, dtype=jnp.int32)[:, None, None] * t + idx
        rows.append(rid.transpose(1, 0, 2).reshape(N_EXPERTS, -1))
        gates.append(gate.transpose(1, 0, 2).reshape(N_EXPERTS, -1))
        base += nb * t
    rows = jnp.concatenate(rows, 1)
    h_packed = jnp.concatenate([hp for hp, _ in sets], 0) if len(sets) > 1 else sets[0][0]
    return expert_ffn_rows(h_packed, rows, jnp.concatenate(gates, 1), w1, w3, w2, layer)


def kernel(x, c, ctx, c_ctx, ada_w, ada_b, ln_g, ln_b, ab_w_in, ab_conv_w, ab_conv_b, ab_gate_b,
           ab_head_g, ab_pool_w, ab_pool_s, ab_w_out, mla_w_in, mla_q_norm_g, mla_kv_norm_g,
           mla_w_uq, mla_w_ukv, mla_w_o, moe_router, moe_w1, moe_w3, moe_w2):
    nb, t_lat, _ = x.shape
    t_ctx = ctx.shape[1]
    n_lat, n_ctx = nb * t_lat, nb * t_ctx
    cond = jnp.concatenate([c, c_ctx[None]], 0)
    mods = [adaln(cond, ada_w[i], ada_b[i]) for i in range(DEPTH)]
    mods_l = [[m[:nb] for m in ms] for ms in mods]
    mods_c = [[m[nb:] for m in ms] for ms in mods]
    x_lat, x_ctx = x.reshape(n_lat, D_MODEL), ctx.reshape(n_ctx, D_MODEL)
    h_lat = modulate(x, mods_l[0][0], mods_l[0][1]).astype(BF16)
    h_ctx = modulate(ctx, mods_c[0][0], mods_c[0][1]).astype(BF16)
    for i in range(DEPTH):
        with_ctx = i < DEPTH - 1
        j = i // 2
        _, _, ga_l, shf_l, scf_l, gaf_l = mods_l[i]
        _, _, ga_c, shf_c, scf_c, gaf_c = mods_c[i]
        if i % 2 == 0:
            y_lat, y_ctx = mixer_mlstm_pool(h_lat, h_ctx, ab_w_in[j], ab_conv_w[j], ab_conv_b[j], ab_gate_b[j],
                                            ab_head_g[j], ab_pool_w[j], ab_pool_s[j], ab_w_out[j], with_ctx)
        else:
            y_lat, y_ctx = mixer_mla(h_lat, h_ctx, mla_w_in[j], mla_q_norm_g[j], mla_kv_norm_g[j],
                                     mla_w_uq[j], mla_w_ukv[j], mla_w_o[j], with_ctx)
        x_lat, hp_l, lg_l = post_norm_fused(x_lat, y_lat.reshape(n_lat, D_MODEL), 0, ga_l, ln_g[i, 0], ln_b[i, 0],
                                            t_lat, ffn=(shf_l, scf_l, moe_router[i]))
        sets = [(hp_l, lg_l)]
        if with_ctx:
            x_ctx, hp_c, lg_c = post_norm_fused(x_ctx, y_ctx.reshape(n_ctx, D_MODEL), 0, ga_c, ln_g[i, 0],
                                                ln_b[i, 0], n_ctx, ffn=(shf_c, scf_c, moe_router[i]))
            sets.append((hp_c, lg_c))
        f_all = expert_choice_ffn(sets, nb, moe_w1, moe_w3, moe_w2, i)
        nxt_l = (mods_l[i + 1][0], mods_l[i + 1][1]) if with_ctx else None
        res = post_norm_fused(x_lat, f_all, 0, gaf_l, ln_g[i, 1], ln_b[i, 1], t_lat, nxt=nxt_l)
        x_lat = res[0]
        if with_ctx:
            h_lat = res[1].reshape(nb, t_lat, D_MODEL)
            x_ctx, h_ctx = post_norm_fused(x_ctx, f_all, n_lat, gaf_c, ln_g[i, 1], ln_b[i, 1], n_ctx,
                                           nxt=(mods_c[i + 1][0], mods_c[i + 1][1]))
            h_ctx = h_ctx.reshape(nb, t_ctx, D_MODEL)
    return x_lat.reshape(x.shape)
```

```python
import jax
import jax.numpy as jnp
from jax import lax
from jax.experimental import pallas as pl
from jax.experimental.pallas import tpu as pltpu

D_MODEL = 2048
DEPTH = 2
F32 = jnp.float32
BF16 = jnp.bfloat16
GRID_W = 64

M_HEADS = 4
M_DIM = D_MODEL // 2
M_HEAD_DIM = M_DIM // M_HEADS
M_CHUNK = 128
M_CONV = 3
P_DIM = D_MODEL - M_DIM
P_GROUPS = 4
P_GROUP_DIM = P_DIM // P_GROUPS
POOL_WINDOWS = (2, 4, 8, 16)
A_HEADS = 16
Q_LORA = 1536
KV_LORA = 512
NOPE = 128
ROPE = 64
V_DIM = 128
ROPE_THETA = 10000.0
N_EXPERTS = 16
EC_FACTOR = 2
LN_EPS = 1e-5
RMS_EPS = 1e-6
ALPHA = (2 * DEPTH) ** 0.25

V7X_VMEM_BYTES = 64 * 1024 * 1024
VMEM_LIMIT_BYTES = V7X_VMEM_BYTES - 6 * 1024 * 1024


def _params(n_axes):
    return pltpu.CompilerParams(dimension_semantics=("arbitrary",) * n_axes,
                                vmem_limit_bytes=VMEM_LIMIT_BYTES)


def _mm_kernel(x_ref, w_ref, o_ref):
    o_ref[...] = jnp.dot(x_ref[...].astype(BF16), w_ref[...].astype(BF16),
                         preferred_element_type=F32).astype(o_ref.dtype)


def _pick_tile(n, prefs):
    for t in prefs:
        if n % t == 0:
            return t
    return n


def matmul(x, w, out_dtype=F32):
    if x.ndim == 2:
        return _matmul_grouped(x[None], w[None], out_dtype)[0]
    return _matmul_grouped(x, w, out_dtype)


def _matmul_grouped(x, w, out_dtype):
    g, m, k = x.shape
    n = w.shape[-1]
    tm = _pick_tile(m, (1024, 512, 256, 128))
    tn = _pick_tile(n, (512, 384, 256, 128))
    return pl.pallas_call(
        _mm_kernel,
        grid=(g, n // tn, m // tm),
        in_specs=[pl.BlockSpec((None, tm, k), lambda e, j, i: (e, i, 0)),
                  pl.BlockSpec((None, k, tn), lambda e, j, i: (e, 0, j))],
        out_specs=pl.BlockSpec((None, tm, tn), lambda e, j, i: (e, i, j)),
        out_shape=jax.ShapeDtypeStruct((g, m, n), out_dtype),
        compiler_params=_params(3),
        name="matmul",
    )(x, w)


def dense(x, w, splits=None):
    b, t, k = x.shape
    x2 = x.reshape(b * t, k)
    if splits is None:
        return matmul(x2, w).reshape(b, t, w.shape[-1])
    outs, lo = [], 0
    for width in splits:
        outs.append(matmul(x2, w[:, lo:lo + width]).reshape(b, t, width))
        lo += width
    return outs


def adaln(cond, w, b):
    m = jnp.dot(jax.nn.silu(cond), w, precision=lax.Precision.HIGHEST) + b
    m = m.reshape(cond.shape[0], 6, 1, D_MODEL)
    return [m[:, k] for k in range(6)]


def modulate(x, shift, scale):
    return x * (1.0 + scale) + shift


NORM_ROWS = 256
HALF_D = D_MODEL // 2


def _pack_bf16_pairs(h):
    bits = pltpu.bitcast(h.astype(BF16).astype(F32), jnp.uint32)
    return (bits[:, :HALF_D] >> 16) | (bits[:, HALF_D:] & jnp.uint32(0xFFFF0000))


def _make_post_norm_kernel(with_ffn, with_next):
    def body(*refs):
        x_ref, y_ref, gate_ref, g_ref, b_ref = refs[:5]
        pos = 5
        if with_ffn:
            shf_ref, scf_ref, wr_ref = refs[pos:pos + 3]
            pos += 3
        if with_next:
            shn_ref, scn_ref = refs[pos:pos + 2]
            pos += 2
        outs = refs[pos:]
        z = ALPHA * x_ref[...] + (1.0 + gate_ref[...]) * y_ref[...]
        mu = jnp.mean(z, axis=-1, keepdims=True)
        zc = z - mu
        var = jnp.mean(zc * zc, axis=-1, keepdims=True)
        xn = zc * lax.rsqrt(var + LN_EPS) * g_ref[...] + b_ref[...]
        outs[0][...] = xn
        k = 1
        if with_ffn:
            h = xn * (1.0 + scf_ref[...]) + shf_ref[...]
            outs[k][...] = _pack_bf16_pairs(h)
            outs[k + 1][...] = jnp.dot(h, wr_ref[...], preferred_element_type=F32,
                                       precision=lax.Precision.HIGHEST)
            k += 2
        if with_next:
            outs[k][...] = (xn * (1.0 + scn_ref[...]) + shn_ref[...]).astype(BF16)
    return body


def post_norm_fused(x, y, y_row0, gate, ln_g, ln_b, rows_per_set, ffn=None, nxt=None):
    n = x.shape[0]
    tps = rows_per_set // NORM_ROWS
    off = y_row0 // NORM_ROWS
    row = pl.BlockSpec((NORM_ROWS, D_MODEL), lambda i: (i, 0))
    tab = pl.BlockSpec((None, 1, D_MODEL), lambda i: (i // tps, 0, 0))
    one = pl.BlockSpec((None, 1, D_MODEL), lambda i: (0, 0, 0))
    args = [x, y, gate, ln_g.reshape(1, 1, D_MODEL), ln_b.reshape(1, 1, D_MODEL)]
    in_specs = [row, pl.BlockSpec((NORM_ROWS, D_MODEL), lambda i: (i + off, 0)), tab, one, one]
    out_shape = [jax.ShapeDtypeStruct((n, D_MODEL), F32)]
    out_specs = [row]
    if ffn is not None:
        args += [ffn[0], ffn[1], ffn[2]]
        in_specs += [tab, tab, pl.BlockSpec((D_MODEL, N_EXPERTS), lambda i: (0, 0))]
        out_shape += [jax.ShapeDtypeStruct((n, HALF_D), jnp.uint32), jax.ShapeDtypeStruct((n, N_EXPERTS), F32)]
        out_specs += [pl.BlockSpec((NORM_ROWS, HALF_D), lambda i: (i, 0)),
                      pl.BlockSpec((NORM_ROWS, N_EXPERTS), lambda i: (i, 0))]
    if nxt is not None:
        args += [nxt[0], nxt[1]]
        in_specs += [tab, tab]
        out_shape += [jax.ShapeDtypeStruct((n, D_MODEL), BF16)]
        out_specs += [row]
    return pl.pallas_call(
        _make_post_norm_kernel(ffn is not None, nxt is not None),
        grid=(n // NORM_ROWS,),
        in_specs=in_specs, out_specs=out_specs, out_shape=out_shape,
        compiler_params=_params(1),
        name="post_norm",
    )(*args)


def depthwise_conv_centred(u, w, b):
    up = jnp.pad(u, ((0, 0), (M_CONV // 2, M_CONV // 2), (0, 0)))
    t = u.shape[1]
    out = sum(up[:, j:j + t] * w[j] for j in range(M_CONV))
    return out + b


def ab_project(h, w_in, conv_w, conv_b, gate_b):
    p, pg, u = dense(h, w_in, splits=(4 * M_DIM, 4 * M_HEADS, P_DIM))
    qk = jax.nn.silu(depthwise_conv_centred(p[..., :2 * M_DIM], conv_w, conv_b))
    kscale = jnp.concatenate([jnp.ones((M_DIM,), F32), jnp.full((M_DIM,), M_HEAD_DIM ** -0.5, F32)])
    qk = (qk * kscale).astype(BF16)
    v = p[..., 2 * M_DIM:3 * M_DIM].astype(BF16)
    gates = (pg + gate_b).reshape(p.shape[:2] + (2, 2, M_HEADS))
    li = gates[..., 0, :]
    lf = jax.nn.log_sigmoid(gates[..., 1, :])
    grow = jnp.stack([li, lf], 3).reshape(p.shape[:2] + (4 * M_HEADS,))
    return qk, v, p, grow, u


def _mlstm_chain(q, k, v, li_row, lf_row, c_ref, n_ref, m_ref, reverse):
    L = M_CHUNK
    ti = lax.broadcasted_iota(jnp.int32, (L, L), 0)
    si = lax.broadcasted_iota(jnp.int32, (L, L), 1)
    mask = (si >= ti) if reverse else (si <= ti)
    mask_t = (ti >= si) if reverse else (ti <= si)
    lf_b = jnp.broadcast_to(lf_row, (L, L))
    lf_bt = lf_b.T
    li_col = jnp.broadcast_to(li_row, (L, L)).T[:, 0:1]
    b_col = jnp.sum(jnp.where(mask, lf_b, 0.0), axis=1, keepdims=True)
    b_row = jnp.sum(jnp.where(mask_t, lf_bt, 0.0), axis=0, keepdims=True)
    a_row = li_row - b_row
    dmat = jnp.where(mask, b_col + a_row, -jnp.inf)
    m_prev = m_ref[...][:, 0:1]
    inter = b_col + m_prev
    m_t = jnp.maximum(inter, jnp.max(dmat, axis=1, keepdims=True))
    w_intra = jnp.exp(dmat - m_t)
    w_inter = jnp.exp(inter - m_t)
    s = lax.dot_general(q, k, (((1,), (1,)), ((), ())), preferred_element_type=F32) * w_intra
    c_prev = c_ref[...]
    n_prev = n_ref[...]
    num = (w_inter * jnp.dot(q, c_prev.astype(BF16), preferred_element_type=F32)
           + jnp.dot(s.astype(BF16), v, preferred_element_type=F32))
    den = (w_inter * jnp.sum(q.astype(F32) * n_prev, axis=1, keepdims=True)
           + jnp.sum(s, axis=1, keepdims=True))
    h = num / jnp.maximum(jnp.abs(den), jnp.exp(-m_t))
    bl = jnp.sum(lf_row, axis=1, keepdims=True)
    m_new = jnp.maximum(bl + m_prev, jnp.max(bl + a_row, axis=1, keepdims=True))
    decay = jnp.exp(bl + m_prev - m_new)
    wg_col = jnp.exp(bl - b_col + li_col - m_new)
    kw = k.astype(F32) * wg_col
    c_ref[...] = decay * c_prev + jnp.dot(kw.T.astype(BF16), v, preferred_element_type=F32)
    n_ref[...] = decay * n_prev + jnp.sum(kw, axis=0, keepdims=True)
    m_ref[...] = jnp.broadcast_to(m_new, m_ref.shape)
    return h


def _mlstm_kernel(qf_ref, kf_ref, vf_ref, gf_ref, qb_ref, kb_ref, vb_ref, gb_ref,
                  hf_ref, hb_ref, c_scr, n_scr, m_scr):
    @pl.when(pl.program_id(1) == 0)
    def _():
        c_scr[...] = jnp.zeros_like(c_scr)
        n_scr[...] = jnp.zeros_like(n_scr)
        m_scr[...] = jnp.zeros_like(m_scr)

    dirs = ((qf_ref, kf_ref, vf_ref, gf_ref, hf_ref), (qb_ref, kb_ref, vb_ref, gb_ref, hb_ref))
    for d, (q_ref, k_ref, v_ref, g_ref, h_ref) in enumerate(dirs):
        for hd in range(M_HEADS):
            cols = slice(hd * M_HEAD_DIM, (hd + 1) * M_HEAD_DIM)
            row_i = d * 2 * M_HEADS + hd
            row_f = row_i + M_HEADS
            h = _mlstm_chain(q_ref[:, cols], k_ref[:, cols], v_ref[:, cols],
                             g_ref[row_i:row_i + 1, :], g_ref[row_f:row_f + 1, :],
                             c_scr.at[d, hd], n_scr.at[d, hd], m_scr.at[d, hd], reverse=(d == 1))
            h_ref[:, cols] = h.astype(h_ref.dtype)


def mlstm_scan(qk, v, grow, n_ctx_chunks):
    nb, tt, _ = v.shape
    nc = tt // M_CHUNK
    g = grow.reshape(nb, nc, M_CHUNK, 4 * M_HEADS).transpose(0, 1, 3, 2)

    def fwd(c):
        return c

    def bwd(c):
        return jnp.where(c < n_ctx_chunks, n_ctx_chunks - 1 - c, nc + n_ctx_chunks - 1 - c)

    def specs(cm):
        return [pl.BlockSpec((None, M_CHUNK, M_DIM), lambda b, c: (b, cm(c), 0)),
                pl.BlockSpec((None, M_CHUNK, M_DIM), lambda b, c: (b, cm(c), 1)),
                pl.BlockSpec((None, M_CHUNK, M_DIM), lambda b, c: (b, cm(c), 0)),
                pl.BlockSpec((None, None, 4 * M_HEADS, M_CHUNK), lambda b, c: (b, cm(c), 0, 0))]

    out_sds = jax.ShapeDtypeStruct((nb, tt, M_DIM), F32)
    return pl.pallas_call(
        _mlstm_kernel,
        grid=(nb, nc),
        in_specs=specs(fwd) + specs(bwd),
        out_specs=[pl.BlockSpec((None, M_CHUNK, M_DIM), lambda b, c: (b, fwd(c), 0)),
                   pl.BlockSpec((None, M_CHUNK, M_DIM), lambda b, c: (b, bwd(c), 0))],
        out_shape=[out_sds, out_sds],
        scratch_shapes=[pltpu.VMEM((2, M_HEADS, M_HEAD_DIM, M_HEAD_DIM), F32),
                        pltpu.VMEM((2, M_HEADS, 1, M_HEAD_DIM), F32),
                        pltpu.VMEM((2, M_HEADS, 1, 128), F32)],
        compiler_params=_params(2),
        name="mlstm_scan",
    )(qk, qk, v, g, qk, qk, v, g)


POST_ROWS = 256
POOL_HALO = 8


def _mixer_a_post_kernel(hf_ref, hb_ref, o_ref, u_ref, uprev_ref, unext_ref, hg_ref, pw_ref, ps_ref, out_ref):
    i = pl.program_id(1)
    n_tiles = pl.num_programs(1)
    t_seq = n_tiles * POST_ROWS
    hs = hf_ref[...] + hb_ref[...]
    gate = jax.nn.sigmoid(o_ref[...])
    for hd in range(M_HEADS):
        cols = slice(hd * M_HEAD_DIM, (hd + 1) * M_HEAD_DIM)
        x = hs[:, cols]
        xc = x - jnp.mean(x, axis=-1, keepdims=True)
        var = jnp.mean(xc * xc, axis=-1, keepdims=True)
        out_ref[:, cols] = (xc * lax.rsqrt(var + LN_EPS) * hg_ref[:, cols] * gate[:, cols]).astype(out_ref.dtype)
    u = u_ref[...]
    prev = jnp.where(i == 0, 0.0, uprev_ref[...])
    nxt = jnp.where(i == n_tiles - 1, 0.0, unext_ref[...])
    ext = jnp.concatenate([prev, u, nxt], axis=0)
    n_ext = POST_ROWS + 2 * POOL_HALO
    t = i * POST_ROWS + lax.broadcasted_iota(jnp.int32, (POST_ROWS, 1), 0)
    for g, w in enumerate(POOL_WINDOWS):
        cols = slice(g * P_GROUP_DIM, (g + 1) * P_GROUP_DIM)
        a = ext[:, cols]
        a = a + pltpu.roll(a, 1, axis=0)
        step = 1
        while 2 * step < w:
            a = pltpu.roll(a, step, axis=0) + pltpu.roll(a, n_ext - step, axis=0)
            step *= 2
        count = (jnp.minimum(t + w // 2, t_seq) - jnp.maximum(t - w // 2, 0)).astype(F32)
        pooled = a[POOL_HALO:POOL_HALO + POST_ROWS] / count - u[:, cols]
        y = jnp.dot(pooled.astype(BF16), pw_ref[g].astype(BF16), preferred_element_type=F32)
        out_ref[:, M_DIM + g * P_GROUP_DIM:M_DIM + (g + 1) * P_GROUP_DIM] = (y * ps_ref[:, cols]).astype(out_ref.dtype)


def mixer_a_post(h_f, h_b, h_row0, p, u, head_g, pool_w, pool_s):
    nb, t, _ = u.shape
    nt = t // POST_ROWS
    off = h_row0 // POST_ROWS
    per = POST_ROWS // POOL_HALO
    return pl.pallas_call(
        _mixer_a_post_kernel,
        grid=(nb, nt),
        in_specs=[pl.BlockSpec((None, POST_ROWS, M_DIM), lambda b, i: (b, i + off, 0)),
                  pl.BlockSpec((None, POST_ROWS, M_DIM), lambda b, i: (b, i + off, 0)),
                  pl.BlockSpec((None, POST_ROWS, M_DIM), lambda b, i: (b, i, 3)),
                  pl.BlockSpec((None, POST_ROWS, P_DIM), lambda b, i: (b, i, 0)),
                  pl.BlockSpec((None, POOL_HALO, P_DIM), lambda b, i: (b, jnp.maximum(i * per - 1, 0), 0)),
                  pl.BlockSpec((None, POOL_HALO, P_DIM), lambda b, i: (b, jnp.minimum((i + 1) * per, nt * per - 1), 0)),
                  pl.BlockSpec((1, M_DIM), lambda b, i: (0, 0)),
                  pl.BlockSpec((P_GROUPS, P_GROUP_DIM, P_GROUP_DIM), lambda b, i: (0, 0, 0)),
                  pl.BlockSpec((1, P_DIM), lambda b, i: (0, 0))],
        out_specs=pl.BlockSpec((None, POST_ROWS, D_MODEL), lambda b, i: (b, i, 0)),
        out_shape=jax.ShapeDtypeStruct((nb, t, D_MODEL), BF16),
        compiler_params=_params(2),
        name="mixer_a_post",
    )(h_f, h_b, p, u, u, u, head_g.reshape(1, M_DIM), pool_w, pool_s.reshape(1, P_DIM))


def mixer_mlstm_pool(h_lat, h_ctx, w_in, conv_w, conv_b, gate_b, head_g, pool_w, pool_s, w_out, with_ctx_out):
    t_ctx = h_ctx.shape[1]
    qk_c, v_c, pc, g_c, uc = ab_project(h_ctx, w_in, conv_w, conv_b, gate_b)
    qk_l, v_l, pl_, g_l, ul = ab_project(h_lat, w_in, conv_w, conv_b, gate_b)
    h_f, h_b = mlstm_scan(jnp.concatenate([qk_c, qk_l], 1), jnp.concatenate([v_c, v_l], 1),
                          jnp.concatenate([g_c, g_l], 1), t_ctx // M_CHUNK)
    y_lat = dense(mixer_a_post(h_f, h_b, t_ctx, pl_, ul, head_g, pool_w, pool_s), w_out)
    if not with_ctx_out:
        return y_lat, None
    y_ctx = dense(mixer_a_post(h_f, h_b, 0, pc, uc, head_g, pool_w, pool_s), w_out)
    return y_lat, y_ctx


def axial_rope_tables(n_tokens):
    rows = n_tokens // GRID_W
    row = jnp.repeat(jnp.arange(rows), GRID_W)
    col = jnp.tile(jnp.arange(GRID_W), rows)
    half = ROPE // 2
    inv = ROPE_THETA ** (-jnp.arange(0, half, 2, dtype=F32) / half)
    ang = jnp.stack([row[:, None] * inv, col[:, None] * inv], 1)
    return jnp.cos(ang), jnp.sin(ang)


def apply_axial_rope(x, cos, sin):
    xs = x.reshape(x.shape[:-1] + (2, 2, ROPE // 4))
    x1, x2 = xs[..., 0, :], xs[..., 1, :]
    out = jnp.stack([x1 * cos - x2 * sin, x2 * cos + x1 * sin], -2)
    return out.reshape(x.shape)


LOG2_E = 1.4426950408889634
Q_SCALE = (NOPE + ROPE) ** -0.5 * LOG2_E
ATTN_KEY_SPLITS = 4


def _rms_norm_rows(x, g):
    return x * lax.rsqrt(jnp.mean(jnp.square(x), axis=-1, keepdims=True) + RMS_EPS) * g


def _q_proj_kernel(x_ref, g_ref, w_ref, tab_ref, o_ref, xn_scr):
    @pl.when(pl.program_id(1) == 0)
    def _():
        xn_scr[...] = _rms_norm_rows(x_ref[...], g_ref[...]).astype(BF16)

    o = jnp.dot(xn_scr[...], w_ref[...], preferred_element_type=F32)
    r = o[:, NOPE:] * tab_ref[...]
    rot = r + pltpu.roll(r, ROPE, axis=1)
    o_ref[...] = (jnp.concatenate([o[:, :NOPE], rot[:, :ROPE]], axis=1) * Q_SCALE).astype(o_ref.dtype)


def q_proj(pq, q_norm_g, w_uq, tab):
    nb, t, _ = pq.shape
    w = w_uq.reshape(Q_LORA, A_HEADS, NOPE + ROPE)
    wpe = w[..., NOPE:].reshape(Q_LORA, A_HEADS, 2, 2, ROPE // 4)
    wsw = jnp.stack([-wpe[..., 1, :], wpe[..., 0, :]], -2).reshape(Q_LORA, A_HEADS, ROPE)
    wh = jnp.concatenate([w, wsw], -1).transpose(1, 0, 2).astype(BF16)
    tm = _pick_tile(t, (1024, 512, 256, 128))
    nt = t // tm
    return pl.pallas_call(
        _q_proj_kernel,
        grid=(nb * nt, A_HEADS),
        in_specs=[pl.BlockSpec((tm, Q_LORA), lambda i, h: (i, 0)),
                  pl.BlockSpec((1, Q_LORA), lambda i, h: (0, 0)),
                  pl.BlockSpec((None, Q_LORA, NOPE + 2 * ROPE), lambda i, h: (h, 0, 0)),
                  pl.BlockSpec((tm, 2 * ROPE), lambda i, h: (i % nt, 0))],
        out_specs=pl.BlockSpec((None, None, tm, NOPE + ROPE), lambda i, h: (i // nt, h, i % nt, 0)),
        out_shape=jax.ShapeDtypeStruct((nb, A_HEADS, t, NOPE + ROPE), BF16),
        scratch_shapes=[pltpu.VMEM((tm, Q_LORA), BF16)],
        compiler_params=_params(2),
        name="mla_q_proj",
    )(pq.reshape(nb * t, Q_LORA), q_norm_g.reshape(1, Q_LORA), wh, tab)


def _kv_proj_kernel(x_ref, g_ref, w_ref, kpe_ref, k_ref, v_ref, xn_scr):
    @pl.when(pl.program_id(1) == 0)
    def _():
        xn_scr[...] = _rms_norm_rows(x_ref[...], g_ref[...]).astype(BF16)

    o = jnp.dot(xn_scr[...], w_ref[...], preferred_element_type=F32)
    k_ref[...] = jnp.concatenate([o[:, :NOPE], kpe_ref[...]], axis=1).astype(k_ref.dtype)
    lane = lax.broadcasted_iota(jnp.int32, (o.shape[0], V_DIM), 1)
    ones_col = jnp.where(lane == 0, 1.0, 0.0)
    v_ref[...] = jnp.concatenate([o[:, NOPE:], ones_col], axis=1).astype(v_ref.dtype)


def kv_proj(pkv, kpe, kv_norm_g, w_ukv):
    nb, tk, _ = pkv.shape
    wh = w_ukv.reshape(KV_LORA, A_HEADS, NOPE + V_DIM).transpose(1, 0, 2).astype(BF16)
    tm = _pick_tile(tk, (2176, 1024, 512, 256, 128))
    nt = tk // tm
    return pl.pallas_call(
        _kv_proj_kernel,
        grid=(nb * nt, A_HEADS),
        in_specs=[pl.BlockSpec((tm, KV_LORA), lambda i, h: (i, 0)),
                  pl.BlockSpec((1, KV_LORA), lambda i, h: (0, 0)),
                  pl.BlockSpec((None, KV_LORA, NOPE + V_DIM), lambda i, h: (h, 0, 0)),
                  pl.BlockSpec((tm, ROPE), lambda i, h: (i, 0))],
        out_specs=[pl.BlockSpec((None, None, tm, NOPE + ROPE), lambda i, h: (i // nt, h, i % nt, 0)),
                   pl.BlockSpec((None, None, tm, 2 * V_DIM), lambda i, h: (i // nt, h, i % nt, 0))],
        out_shape=[jax.ShapeDtypeStruct((nb, A_HEADS, tk, NOPE + ROPE), BF16),
                   jax.ShapeDtypeStruct((nb, A_HEADS, tk, 2 * V_DIM), BF16)],
        scratch_shapes=[pltpu.VMEM((tm, KV_LORA), BF16)],
        compiler_params=_params(2),
        name="mla_kv_proj",
    )(pkv.reshape(nb * tk, KV_LORA), kv_norm_g.reshape(1, KV_LORA), wh, kpe.reshape(nb * tk, ROPE))


def _attn_kernel(q_ref, k_ref, v_ref, o_ref):
    q = q_ref[...]
    n_lanes = k_ref.shape[0] // 128
    sizes = [(n_lanes // ATTN_KEY_SPLITS + (c < n_lanes % ATTN_KEY_SPLITS)) * 128 for c in range(ATTN_KEY_SPLITS)]
    m = o = None
    lo = 0
    for size in filter(None, sizes):
        ks = slice(lo, lo + size)
        lo += size
        s = lax.dot_general(q, k_ref[ks, :], (((1,), (1,)), ((), ())), preferred_element_type=F32)
        mc = jnp.max(s, axis=-1, keepdims=True)
        m_new = mc if m is None else jnp.maximum(m, mc)
        p = jnp.exp2((s - m_new).astype(BF16))
        oc = jnp.dot(p, v_ref[ks, :], preferred_element_type=F32)
        o = oc if o is None else o * jnp.exp2(m - m_new) + oc
        m = m_new
    o_ref[...] = (o[:, :V_DIM] / o[:, V_DIM:V_DIM + 1]).astype(o_ref.dtype)


def attend(q, k, vh):
    nb, _, tq, _ = q.shape
    tk = k.shape[2]
    bq = _pick_tile(tq, (1024, 512, 256, 128))
    return pl.pallas_call(
        _attn_kernel,
        grid=(nb, A_HEADS, tq // bq),
        in_specs=[pl.BlockSpec((None, None, bq, NOPE + ROPE), lambda b, h, i: (b, h, i, 0)),
                  pl.BlockSpec((None, None, tk, NOPE + ROPE), lambda b, h, i: (b, h, 0, 0)),
                  pl.BlockSpec((None, None, tk, 2 * V_DIM), lambda b, h, i: (b, h, 0, 0))],
        out_specs=pl.BlockSpec((None, bq, V_DIM), lambda b, h, i: (b, i, h)),
        out_shape=jax.ShapeDtypeStruct((nb, tq, A_HEADS * V_DIM), BF16),
        compiler_params=_params(3),
        name="mla_attention",
    )(q, k, vh)


def mixer_mla(h_lat, h_ctx, w_in, q_norm_g, kv_norm_g, w_uq, w_ukv, w_o, with_ctx_out):
    t_lat, t_ctx = h_lat.shape[1], h_ctx.shape[1]
    cos, sin = axial_rope_tables(t_lat)
    cos_full = jnp.repeat(cos, 2, axis=1).reshape(t_lat, ROPE)
    sin_full = jnp.repeat(sin, 2, axis=1).reshape(t_lat, ROPE)
    pq_l, pkv_l, pr_l = dense(h_lat, w_in, splits=(Q_LORA, KV_LORA, ROPE))
    if with_ctx_out:
        pq_c, pkv_c, pr_c = dense(h_ctx, w_in, splits=(Q_LORA, KV_LORA, ROPE))
    else:
        pkv_c, pr_c = dense(h_ctx, w_in[:, Q_LORA:], splits=(KV_LORA, ROPE))
    kpe = jnp.concatenate([pr_c, apply_axial_rope(pr_l, cos, sin)], 1)
    k, vh = kv_proj(jnp.concatenate([pkv_c, pkv_l], 1), kpe, kv_norm_g, w_ukv)
    q_l = q_proj(pq_l, q_norm_g, w_uq, jnp.concatenate([cos_full, sin_full], -1))
    y_lat = dense(attend(q_l, k, vh), w_o)
    if not with_ctx_out:
        return y_lat, None
    no_rot = jnp.concatenate([jnp.ones((t_ctx, ROPE), F32), jnp.zeros((t_ctx, ROPE), F32)], -1)
    q_c = q_proj(pq_c, q_norm_g, w_uq, no_rot)
    y_ctx = dense(attend(q_c, k[:, :, :t_ctx], vh[:, :, :t_ctx]), w_o)
    return y_lat, y_ctx


MOE_TF = 256
MOE_TN = 256
MOE_NF = D_MODEL // MOE_TF
MOE_NN = D_MODEL // MOE_TN


def _moe_kernel(rows_ref, h_hbm, gate_ref, w1_ref, w3_ref, w2_ref, f_in_hbm, f_hbm,
                xg, xb, hmid, ybuf, stage, sem_g, sem_r, sem_w):
    del f_in_hbm
    e = pl.program_id(0)
    j = pl.program_id(1)
    n_e = pl.num_programs(0)
    m = xg.shape[0]
    per_step = m // MOE_NF
    last_step = MOE_NF + MOE_NN - 1

    prev_e = jnp.maximum(e - 1, 0)
    next_e = jnp.minimum(e + 1, n_e - 1)

    def issue(body, n, inline):
        if inline:
            for i in range(n):
                body(i, 0)
        else:
            lax.fori_loop(0, n, body, 0, unroll=8)

    def gather(expert, lo, n, inline=False):
        def body(i, carry):
            r = rows_ref[expert * m + lo + i]
            pltpu.make_async_copy(h_hbm.at[pl.ds(r, 1), :], xg.at[pl.ds(lo + i, 1), :], sem_g.at[0]).start()
            return carry
        issue(body, n, inline)

    def f_rows(expert, piece, slot, to_vmem, inline=False):
        def body(i, carry):
            r = rows_ref[expert * m + piece * per_step + i]
            if to_vmem:
                pltpu.make_async_copy(f_hbm.at[pl.ds(r, 1), :], stage.at[slot, pl.ds(i, 1), :],
                                      sem_r.at[slot]).start()
            else:
                pltpu.make_async_copy(stage.at[slot, pl.ds(i, 1), :], f_hbm.at[pl.ds(r, 1), :],
                                      sem_w.at[slot]).start()
            return carry
        issue(body, per_step, inline)

    def wait_all(sem, slot):
        pltpu.make_async_copy(stage.at[slot], stage.at[slot], sem.at[slot]).wait()

    def add_and_write(expert, piece, slot, inline=False):
        lo = piece * per_step
        if not isinstance(lo, int):
            lo = pl.multiple_of(lo, 16)
        for n in range(MOE_NN):
            cols = slice(n * MOE_TN, (n + 1) * MOE_TN)
            stage[slot, :, cols] += ybuf[n, pl.ds(lo, per_step), :].astype(F32)
        f_rows(expert, piece, slot, to_vmem=False, inline=inline)

    def up_projection():
        x = xb[...]
        a = jnp.dot(x, w1_ref[...].astype(BF16), preferred_element_type=F32)
        g = jnp.dot(x, w3_ref[...].astype(BF16), preferred_element_type=F32)
        hmid[j] = (jax.nn.silu(a) * g).astype(BF16)

    @pl.when((e == 0) & (j == 0))
    def _():
        gather(0, 0, m)
        ybuf[...] = jnp.zeros_like(ybuf)

    @pl.when((j >= 2) & (j <= MOE_NF + 1))
    def _():
        wait_all(sem_w, j % 2)

    @pl.when((j >= 1) & (j <= MOE_NF))
    def _():
        wait_all(sem_r, (j - 1) % 2)

    @pl.when(j == 0)
    def _():
        pltpu.make_async_copy(xg, xg, sem_g.at[0]).wait()
        w = xg[...]
        xb[:, :HALF_D] = pltpu.bitcast(w << 16, F32).astype(BF16)
        xb[:, HALF_D:] = pltpu.bitcast(w & jnp.uint32(0xFFFF0000), F32).astype(BF16)

    @pl.when(j == 0)
    def _():
        f_rows(prev_e, 0, 0, to_vmem=True, inline=True)
        up_projection()

    @pl.when((j >= 1) & (j < MOE_NF))
    def _():
        add_and_write(prev_e, j - 1, (j - 1) % 2, inline=True)
        f_rows(prev_e, j, j % 2, to_vmem=True, inline=True)
        up_projection()

    @pl.when(j == MOE_NF)
    def _():
        add_and_write(prev_e, MOE_NF - 1, (MOE_NF - 1) % 2)

    @pl.when(j >= MOE_NF)
    def _():
        gather(next_e, (j - MOE_NF) * per_step, per_step, inline=True)
        acc = jnp.dot(hmid[0], w2_ref[0:MOE_TF, :].astype(BF16), preferred_element_type=F32)
        for f in range(1, MOE_NF):
            acc += jnp.dot(hmid[f], w2_ref[f * MOE_TF:(f + 1) * MOE_TF, :].astype(BF16),
                           preferred_element_type=F32)
        g_rows = gate_ref[...]
        gated = []
        for b in range(m // 128):
            g_col = jnp.broadcast_to(g_rows[b:b + 1, :], (128, 128)).T
            gated.append(acc[b * 128:(b + 1) * 128, :] * jnp.concatenate([g_col] * (MOE_TN // 128), axis=1))
        ybuf[j - MOE_NF] = jnp.concatenate(gated, axis=0).astype(BF16)

    @pl.when((e == n_e - 1) & (j == last_step))
    def _():
        pltpu.make_async_copy(xg, xg, sem_g.at[0]).wait()
        for piece in range(MOE_NF):
            f_rows(e, piece, 0, to_vmem=True)
            wait_all(sem_r, 0)
            add_and_write(e, piece, 0)
            wait_all(sem_w, 0)


def expert_ffn_rows(h_packed, rows, gates, w1, w3, w2, layer):
    n_e, m = rows.shape
    n_rows = h_packed.shape[0]
    assert m % 128 == 0 and (m // MOE_NF) % 16 == 0, m
    any_spec = pl.BlockSpec(memory_space=pl.ANY)
    return pl.pallas_call(
        _moe_kernel,
        grid_spec=pltpu.PrefetchScalarGridSpec(
            num_scalar_prefetch=1,
            grid=(n_e, MOE_NF + MOE_NN),
            in_specs=[any_spec,
                      pl.BlockSpec((None, m // 128, 128), lambda e, j, r: (e, 0, 0)),
                      pl.BlockSpec((None, None, D_MODEL, MOE_TF),
                                   lambda e, j, r: (layer, e, 0, jnp.minimum(j, MOE_NF - 1))),
                      pl.BlockSpec((None, None, D_MODEL, MOE_TF),
                                   lambda e, j, r: (layer, e, 0, jnp.minimum(j, MOE_NF - 1))),
                      pl.BlockSpec((None, None, D_MODEL, MOE_TN),
                                   lambda e, j, r: (layer, e, 0, jnp.maximum(j - MOE_NF, 0))),
                      any_spec],
            out_specs=any_spec,
            scratch_shapes=[pltpu.VMEM((m, HALF_D), jnp.uint32),
                            pltpu.VMEM((m, D_MODEL), BF16),
                            pltpu.VMEM((MOE_NF, m, MOE_TF), BF16),
                            pltpu.VMEM((MOE_NN, m, MOE_TN), BF16),
                            pltpu.VMEM((2, m // MOE_NF, D_MODEL), F32),
                            pltpu.SemaphoreType.DMA((1,)),
                            pltpu.SemaphoreType.DMA((2,)),
                            pltpu.SemaphoreType.DMA((2,))]),
        out_shape=jax.ShapeDtypeStruct((n_rows, D_MODEL), F32),
        input_output_aliases={6: 0},
        compiler_params=_params(2),
        name="expert_ffn",
    )(rows.reshape(-1), h_packed, gates.reshape(n_e, m // 128, 128), w1, w3, w2,
      jnp.zeros((n_rows, D_MODEL), F32))


def route(logits, nb):
    t = logits.shape[0] // nb
    cap = EC_FACTOR * t // N_EXPERTS
    aff = jax.nn.softmax(logits.reshape(nb, t, N_EXPERTS), axis=-1)
    return lax.top_k(jnp.swapaxes(aff, 1, 2), cap)


def expert_choice_ffn(sets, nb, w1, w3, w2, layer):
    rows, gates, base = [], [], 0
    for _, logits in sets:
        gate, idx = route(logits, nb)
        t = logits.shape[0] // nb
        rid = base + jnp.arange(nb, dtype=jnp.int32)[:, None, None] * t + idx
        rows.append(rid.transpose(1, 0, 2).reshape(N_EXPERTS, -1))
        gates.append(gate.transpose(1, 0, 2).reshape(N_EXPERTS, -1))
        base += nb * t
    rows = jnp.concatenate(rows, 1)
    h_packed = jnp.concatenate([hp for hp, _ in sets], 0) if len(sets) > 1 else sets[0][0]
    return expert_ffn_rows(h_packed, rows, jnp.concatenate(gates, 1), w1, w3, w2, layer)


def kernel(x, c, ctx, c_ctx, ada_w, ada_b, ln_g, ln_b, ab_w_in, ab_conv_w, ab_conv_b, ab_gate_b,
           ab_head_g, ab_pool_w, ab_pool_s, ab_w_out, mla_w_in, mla_q_norm_g, mla_kv_norm_g,
           mla_w_uq, mla_w_ukv, mla_w_o, moe_router, moe_w1, moe_w3, moe_w2):
    nb, t_lat, _ = x.shape
    t_ctx = ctx.shape[1]
    n_lat, n_ctx = nb * t_lat, nb * t_ctx
    cond = jnp.concatenate([c, c_ctx[None]], 0)
    mods = [adaln(cond, ada_w[i], ada_b[i]) for i in range(DEPTH)]
    mods_l = [[m[:nb] for m in ms] for ms in mods]
    mods_c = [[m[nb:] for m in ms] for ms in mods]
    x_lat, x_ctx = x.reshape(n_lat, D_MODEL), ctx.reshape(n_ctx, D_MODEL)
    h_lat = modulate(x, mods_l[0][0], mods_l[0][1]).astype(BF16)
    h_ctx = modulate(ctx, mods_c[0][0], mods_c[0][1]).astype(BF16)
    for i in range(DEPTH):
        with_ctx = i < DEPTH - 1
        j = i // 2
        _, _, ga_l, shf_l, scf_l, gaf_l = mods_l[i]
        _, _, ga_c, shf_c, scf_c, gaf_c = mods_c[i]
        if i % 2 == 0:
            y_lat, y_ctx = mixer_mlstm_pool(h_lat, h_ctx, ab_w_in[j], ab_conv_w[j], ab_conv_b[j], ab_gate_b[j],
                                            ab_head_g[j], ab_pool_w[j], ab_pool_s[j], ab_w_out[j], with_ctx)
        else:
            y_lat, y_ctx = mixer_mla(h_lat, h_ctx, mla_w_in[j], mla_q_norm_g[j], mla_kv_norm_g[j],
                                     mla_w_uq[j], mla_w_ukv[j], mla_w_o[j], with_ctx)
        x_lat, hp_l, lg_l = post_norm_fused(x_lat, y_lat.reshape(n_lat, D_MODEL), 0, ga_l, ln_g[i, 0], ln_b[i, 0],
                                            t_lat, ffn=(shf_l, scf_l, moe_router[i]))
        sets = [(hp_l, lg_l)]
        if with_ctx:
            x_ctx, hp_c, lg_c = post_norm_fused(x_ctx, y_ctx.reshape(n_ctx, D_MODEL), 0, ga_c, ln_g[i, 0],
                                                ln_b[i, 0], n_ctx, ffn=(shf_c, scf_c, moe_router[i]))
            sets.append((hp_c, lg_c))
        f_all = expert_choice_ffn(sets, nb, moe_w1, moe_w3, moe_w2, i)
        nxt_l = (mods_l[i + 1][0], mods_l[i + 1][1]) if with_ctx else None
        res = post_norm_fused(x_lat, f_all, 0, gaf_l, ln_g[i, 1], ln_b[i, 1], t_lat, nxt=nxt_l)
        x_lat = res[0]
        if with_ctx:
            h_lat = res[1].reshape(nb, t_lat, D_MODEL)
            x_ctx, h_ctx = post_norm_fused(x_ctx, f_all, n_lat, gaf_c, ln_g[i, 1], ln_b[i, 1], n_ctx,
                                           nxt=(mods_c[i + 1][0], mods_c[i + 1][1]))
            h_ctx = h_ctx.reshape(nb, t_ctx, D_MODEL)
    return x_lat.reshape(x.shape)
```

```python
import jax
import jax.numpy as jnp
from jax import lax
from jax.experimental import pallas as pl
from jax.experimental.pallas import tpu as pltpu

D_MODEL = 2048
DEPTH = 2
F32 = jnp.float32
BF16 = jnp.bfloat16
GRID_W = 64

M_HEADS = 4
M_DIM = D_MODEL // 2
M_HEAD_DIM = M_DIM // M_HEADS
M_CHUNK = 128
M_CONV = 3
P_DIM = D_MODEL - M_DIM
P_GROUPS = 4
P_GROUP_DIM = P_DIM // P_GROUPS
POOL_WINDOWS = (2, 4, 8, 16)
A_HEADS = 16
Q_LORA = 1536
KV_LORA = 512
NOPE = 128
ROPE = 64
V_DIM = 128
ROPE_THETA = 10000.0
N_EXPERTS = 16
EC_FACTOR = 2
LN_EPS = 1e-5
RMS_EPS = 1e-6
ALPHA = (2 * DEPTH) ** 0.25

V7X_VMEM_BYTES = 64 * 1024 * 1024
VMEM_LIMIT_BYTES = V7X_VMEM_BYTES - 6 * 1024 * 1024


def _params(n_axes):
    return pltpu.CompilerParams(dimension_semantics=("arbitrary",) * n_axes,
                                vmem_limit_bytes=VMEM_LIMIT_BYTES)


def _mm_kernel(x_ref, w_ref, o_ref):
    o_ref[...] = jnp.dot(x_ref[...].astype(BF16), w_ref[...].astype(BF16),
                         preferred_element_type=F32).astype(o_ref.dtype)


def _pick_tile(n, prefs):
    for t in prefs:
        if n % t == 0:
            return t
    return n


def matmul(x, w, out_dtype=F32):
    if x.ndim == 2:
        return _matmul_grouped(x[None], w[None], out_dtype)[0]
    return _matmul_grouped(x, w, out_dtype)


def _matmul_grouped(x, w, out_dtype):
    g, m, k = x.shape
    n = w.shape[-1]
    tm = _pick_tile(m, (1024, 512, 256, 128))
    tn = _pick_tile(n, (512, 384, 256, 128))
    return pl.pallas_call(
        _mm_kernel,
        grid=(g, n // tn, m // tm),
        in_specs=[pl.BlockSpec((None, tm, k), lambda e, j, i: (e, i, 0)),
                  pl.BlockSpec((None, k, tn), lambda e, j, i: (e, 0, j))],
        out_specs=pl.BlockSpec((None, tm, tn), lambda e, j, i: (e, i, j)),
        out_shape=jax.ShapeDtypeStruct((g, m, n), out_dtype),
        compiler_params=_params(3),
        name="matmul",
    )(x, w)


def dense(x, w, splits=None):
    b, t, k = x.shape
    x2 = x.reshape(b * t, k)
    if splits is None:
        return matmul(x2, w).reshape(b, t, w.shape[-1])
    outs, lo = [], 0
    for width in splits:
        outs.append(matmul(x2, w[:, lo:lo + width]).reshape(b, t, width))
        lo += width
    return outs


def adaln(cond, w, b):
    m = jnp.dot(jax.nn.silu(cond), w, precision=lax.Precision.HIGHEST) + b
    m = m.reshape(cond.shape[0], 6, 1, D_MODEL)
    return [m[:, k] for k in range(6)]


def modulate(x, shift, scale):
    return x * (1.0 + scale) + shift


NORM_ROWS = 256
HALF_D = D_MODEL // 2


def _pack_bf16_pairs(h):
    bits = pltpu.bitcast(h.astype(BF16).astype(F32), jnp.uint32)
    return (bits[:, :HALF_D] >> 16) | (bits[:, HALF_D:] & jnp.uint32(0xFFFF0000))


def _make_post_norm_kernel(with_ffn, with_next):
    def body(*refs):
        x_ref, y_ref, gate_ref, g_ref, b_ref = refs[:5]
        pos = 5
        if with_ffn:
            shf_ref, scf_ref, wr_ref = refs[pos:pos + 3]
            pos += 3
        if with_next:
            shn_ref, scn_ref = refs[pos:pos + 2]
            pos += 2
        outs = refs[pos:]
        z = ALPHA * x_ref[...] + (1.0 + gate_ref[...]) * y_ref[...]
        mu = jnp.mean(z, axis=-1, keepdims=True)
        zc = z - mu
        var = jnp.mean(zc * zc, axis=-1, keepdims=True)
        xn = zc * lax.rsqrt(var + LN_EPS) * g_ref[...] + b_ref[...]
        outs[0][...] = xn
        k = 1
        if with_ffn:
            h = xn * (1.0 + scf_ref[...]) + shf_ref[...]
            outs[k][...] = _pack_bf16_pairs(h)
            outs[k + 1][...] = jnp.dot(h, wr_ref[...], preferred_element_type=F32,
                                       precision=lax.Precision.HIGHEST)
            k += 2
        if with_next:
            outs[k][...] = (xn * (1.0 + scn_ref[...]) + shn_ref[...]).astype(BF16)
    return body


def post_norm_fused(x, y, y_row0, gate, ln_g, ln_b, rows_per_set, ffn=None, nxt=None):
    n = x.shape[0]
    tps = rows_per_set // NORM_ROWS
    off = y_row0 // NORM_ROWS
    row = pl.BlockSpec((NORM_ROWS, D_MODEL), lambda i: (i, 0))
    tab = pl.BlockSpec((None, 1, D_MODEL), lambda i: (i // tps, 0, 0))
    one = pl.BlockSpec((None, 1, D_MODEL), lambda i: (0, 0, 0))
    args = [x, y, gate, ln_g.reshape(1, 1, D_MODEL), ln_b.reshape(1, 1, D_MODEL)]
    in_specs = [row, pl.BlockSpec((NORM_ROWS, D_MODEL), lambda i: (i + off, 0)), tab, one, one]
    out_shape = [jax.ShapeDtypeStruct((n, D_MODEL), F32)]
    out_specs = [row]
    if ffn is not None:
        args += [ffn[0], ffn[1], ffn[2]]
        in_specs += [tab, tab, pl.BlockSpec((D_MODEL, N_EXPERTS), lambda i: (0, 0))]
        out_shape += [jax.ShapeDtypeStruct((n, HALF_D), jnp.uint32), jax.ShapeDtypeStruct((n, N_EXPERTS), F32)]
        out_specs += [pl.BlockSpec((NORM_ROWS, HALF_D), lambda i: (i, 0)),
                      pl.BlockSpec((NORM_ROWS, N_EXPERTS), lambda i: (i, 0))]
    if nxt is not None:
        args += [nxt[0], nxt[1]]
        in_specs += [tab, tab]
        out_shape += [jax.ShapeDtypeStruct((n, D_MODEL), BF16)]
        out_specs += [row]
    return pl.pallas_call(
        _make_post_norm_kernel(ffn is not None, nxt is not None),
        grid=(n // NORM_ROWS,),
        in_specs=in_specs, out_specs=out_specs, out_shape=out_shape,
        compiler_params=_params(1),
        name="post_norm",
    )(*args)


PREP_ROWS = 256
PREP_HALO = 8


def _qkv_prep_kernel(x_ref, xprev_ref, xnext_ref, w_ref, b_ref, o_ref):
    i = pl.program_id(1)
    c = pl.program_id(2)
    n_tiles = pl.num_programs(1)

    @pl.when(c == 2)
    def _():
        o_ref[...] = x_ref[...].astype(o_ref.dtype)

    @pl.when(c < 2)
    def _():
        x = x_ref[...]
        prev = jnp.where(i == 0, 0.0, xprev_ref[...])
        nxt = jnp.where(i == n_tiles - 1, 0.0, xnext_ref[...])
        ext = jnp.concatenate([prev, x, nxt], axis=0)
        n_ext = PREP_ROWS + 2 * PREP_HALO
        x_m1 = pltpu.roll(ext, 1, axis=0)[PREP_HALO:PREP_HALO + PREP_ROWS]
        x_p1 = pltpu.roll(ext, n_ext - 1, axis=0)[PREP_HALO:PREP_HALO + PREP_ROWS]
        y = x_m1 * w_ref[0:1, :] + x * w_ref[1:2, :] + x_p1 * w_ref[2:3, :] + b_ref[...]
        scale = jnp.where(c == 1, M_HEAD_DIM ** -0.5, 1.0)
        o_ref[...] = (jax.nn.silu(y) * scale).astype(o_ref.dtype)


def qkv_prep(p, conv_w, conv_b):
    nb, t, _ = p.shape
    nt = t // PREP_ROWS
    per = PREP_ROWS // PREP_HALO
    return pl.pallas_call(
        _qkv_prep_kernel,
        grid=(nb, nt, 3),
        in_specs=[pl.BlockSpec((None, PREP_ROWS, M_DIM), lambda b, i, c: (b, i, c)),
                  pl.BlockSpec((None, PREP_HALO, M_DIM), lambda b, i, c: (b, jnp.maximum(i * per - 1, 0), c)),
                  pl.BlockSpec((None, PREP_HALO, M_DIM),
                               lambda b, i, c: (b, jnp.minimum((i + 1) * per, nt * per - 1), c)),
                  pl.BlockSpec((M_CONV, M_DIM), lambda b, i, c: (0, jnp.minimum(c, 1))),
                  pl.BlockSpec((1, M_DIM), lambda b, i, c: (0, jnp.minimum(c, 1)))],
        out_specs=pl.BlockSpec((None, PREP_ROWS, M_DIM), lambda b, i, c: (b, i, c)),
        out_shape=jax.ShapeDtypeStruct((nb, t, 3 * M_DIM), BF16),
        compiler_params=_params(3),
        name="qkv_prep",
    )(p, p, p, conv_w, conv_b.reshape(1, 2 * M_DIM))


def ab_project(h, w_in, conv_w, conv_b, gate_b):
    p, pg, u = dense(h, w_in, splits=(4 * M_DIM, 4 * M_HEADS, P_DIM))
    qkv = qkv_prep(p, conv_w, conv_b)
    gates = (pg + gate_b).reshape(p.shape[:2] + (2, 2, M_HEADS))
    li = gates[..., 0, :]
    lf = jax.nn.log_sigmoid(gates[..., 1, :])
    grow = jnp.stack([li, lf], 3).reshape(p.shape[:2] + (4 * M_HEADS,))
    return qkv, p, grow, u


def _mlstm_chain(q, k, v, li_row, lf_row, c_ref, n_ref, m_ref, reverse):
    L = M_CHUNK
    ti = lax.broadcasted_iota(jnp.int32, (L, L), 0)
    si = lax.broadcasted_iota(jnp.int32, (L, L), 1)
    mask = (si >= ti) if reverse else (si <= ti)
    mask_t = (ti >= si) if reverse else (ti <= si)
    lf_b = jnp.broadcast_to(lf_row, (L, L))
    lf_bt = lf_b.T
    li_col = jnp.broadcast_to(li_row, (L, L)).T[:, 0:1]
    b_col = jnp.sum(jnp.where(mask, lf_b, 0.0), axis=1, keepdims=True)
    b_row = jnp.sum(jnp.where(mask_t, lf_bt, 0.0), axis=0, keepdims=True)
    a_row = li_row - b_row
    dmat = jnp.where(mask, b_col + a_row, -jnp.inf)
    m_prev = m_ref[...][:, 0:1]
    inter = b_col + m_prev
    m_t = jnp.maximum(inter, jnp.max(dmat, axis=1, keepdims=True))
    w_intra = jnp.exp(dmat - m_t)
    w_inter = jnp.exp(inter - m_t)
    s = lax.dot_general(q, k, (((1,), (1,)), ((), ())), preferred_element_type=F32) * w_intra
    c_prev = c_ref[...]
    n_prev = n_ref[...]
    num = (w_inter * jnp.dot(q, c_prev.astype(BF16), preferred_element_type=F32)
           + jnp.dot(s.astype(BF16), v, preferred_element_type=F32))
    den = (w_inter * jnp.sum(q.astype(F32) * n_prev, axis=1, keepdims=True)
           + jnp.sum(s, axis=1, keepdims=True))
    h = num / jnp.maximum(jnp.abs(den), jnp.exp(-m_t))
    bl = jnp.sum(lf_row, axis=1, keepdims=True)
    m_new = jnp.maximum(bl + m_prev, jnp.max(bl + a_row, axis=1, keepdims=True))
    decay = jnp.exp(bl + m_prev - m_new)
    wg_col = jnp.exp(bl - b_col + li_col - m_new)
    kw = k.astype(F32) * wg_col
    c_ref[...] = decay * c_prev + jnp.dot(kw.T.astype(BF16), v, preferred_element_type=F32)
    n_ref[...] = decay * n_prev + jnp.sum(kw, axis=0, keepdims=True)
    m_ref[...] = jnp.broadcast_to(m_new, m_ref.shape)
    return h


def _mlstm_kernel(qf_ref, kf_ref, vf_ref, gf_ref, qb_ref, kb_ref, vb_ref, gb_ref,
                  hf_ref, hb_ref, c_scr, n_scr, m_scr):
    @pl.when(pl.program_id(1) == 0)
    def _():
        c_scr[...] = jnp.zeros_like(c_scr)
        n_scr[...] = jnp.zeros_like(n_scr)
        m_scr[...] = jnp.zeros_like(m_scr)

    dirs = ((qf_ref, kf_ref, vf_ref, gf_ref, hf_ref), (qb_ref, kb_ref, vb_ref, gb_ref, hb_ref))
    for d, (q_ref, k_ref, v_ref, g_ref, h_ref) in enumerate(dirs):
        for hd in range(M_HEADS):
            cols = slice(hd * M_HEAD_DIM, (hd + 1) * M_HEAD_DIM)
            row_i = d * 2 * M_HEADS + hd
            row_f = row_i + M_HEADS
            h = _mlstm_chain(q_ref[:, cols], k_ref[:, cols], v_ref[:, cols],
                             g_ref[row_i:row_i + 1, :], g_ref[row_f:row_f + 1, :],
                             c_scr.at[d, hd], n_scr.at[d, hd], m_scr.at[d, hd], reverse=(d == 1))
            h_ref[:, cols] = h.astype(h_ref.dtype)


def mlstm_scan(qkv, grow, n_ctx_chunks):
    nb, tt, _ = qkv.shape
    nc = tt // M_CHUNK
    g = grow.reshape(nb, nc, M_CHUNK, 4 * M_HEADS).transpose(0, 1, 3, 2)

    def fwd(c):
        return c

    def bwd(c):
        return jnp.where(c < n_ctx_chunks, n_ctx_chunks - 1 - c, nc + n_ctx_chunks - 1 - c)

    def specs(cm):
        return [pl.BlockSpec((None, M_CHUNK, M_DIM), lambda b, c: (b, cm(c), 0)),
                pl.BlockSpec((None, M_CHUNK, M_DIM), lambda b, c: (b, cm(c), 1)),
                pl.BlockSpec((None, M_CHUNK, M_DIM), lambda b, c: (b, cm(c), 2)),
                pl.BlockSpec((None, None, 4 * M_HEADS, M_CHUNK), lambda b, c: (b, cm(c), 0, 0))]

    out_sds = jax.ShapeDtypeStruct((nb, tt, M_DIM), F32)
    return pl.pallas_call(
        _mlstm_kernel,
        grid=(nb, nc),
        in_specs=specs(fwd) + specs(bwd),
        out_specs=[pl.BlockSpec((None, M_CHUNK, M_DIM), lambda b, c: (b, fwd(c), 0)),
                   pl.BlockSpec((None, M_CHUNK, M_DIM), lambda b, c: (b, bwd(c), 0))],
        out_shape=[out_sds, out_sds],
        scratch_shapes=[pltpu.VMEM((2, M_HEADS, M_HEAD_DIM, M_HEAD_DIM), F32),
                        pltpu.VMEM((2, M_HEADS, 1, M_HEAD_DIM), F32),
                        pltpu.VMEM((2, M_HEADS, 1, 128), F32)],
        compiler_params=_params(2),
        name="mlstm_scan",
    )(qkv, qkv, qkv, g, qkv, qkv, qkv, g)


POST_ROWS = 256
POOL_HALO = 8


def _mixer_a_post_kernel(hf_ref, hb_ref, o_ref, u_ref, uprev_ref, unext_ref, hg_ref, pw_ref, ps_ref, out_ref):
    i = pl.program_id(1)
    n_tiles = pl.num_programs(1)
    t_seq = n_tiles * POST_ROWS
    hs = hf_ref[...] + hb_ref[...]
    gate = jax.nn.sigmoid(o_ref[...])
    for hd in range(M_HEADS):
        cols = slice(hd * M_HEAD_DIM, (hd + 1) * M_HEAD_DIM)
        x = hs[:, cols]
        xc = x - jnp.mean(x, axis=-1, keepdims=True)
        var = jnp.mean(xc * xc, axis=-1, keepdims=True)
        out_ref[:, cols] = (xc * lax.rsqrt(var + LN_EPS) * hg_ref[:, cols] * gate[:, cols]).astype(out_ref.dtype)
    u = u_ref[...]
    prev = jnp.where(i == 0, 0.0, uprev_ref[...])
    nxt = jnp.where(i == n_tiles - 1, 0.0, unext_ref[...])
    ext = jnp.concatenate([prev, u, nxt], axis=0)
    n_ext = POST_ROWS + 2 * POOL_HALO
    t = i * POST_ROWS + lax.broadcasted_iota(jnp.int32, (POST_ROWS, 1), 0)
    for g, w in enumerate(POOL_WINDOWS):
        cols = slice(g * P_GROUP_DIM, (g + 1) * P_GROUP_DIM)
        a = ext[:, cols]
        a = a + pltpu.roll(a, 1, axis=0)
        step = 1
        while 2 * step < w:
            a = pltpu.roll(a, step, axis=0) + pltpu.roll(a, n_ext - step, axis=0)
            step *= 2
        count = (jnp.minimum(t + w // 2, t_seq) - jnp.maximum(t - w // 2, 0)).astype(F32)
        pooled = a[POOL_HALO:POOL_HALO + POST_ROWS] / count - u[:, cols]
        y = jnp.dot(pooled.astype(BF16), pw_ref[g].astype(BF16), preferred_element_type=F32)
        out_ref[:, M_DIM + g * P_GROUP_DIM:M_DIM + (g + 1) * P_GROUP_DIM] = (y * ps_ref[:, cols]).astype(out_ref.dtype)


def mixer_a_post(h_f, h_b, h_row0, p, u, head_g, pool_w, pool_s):
    nb, t, _ = u.shape
    nt = t // POST_ROWS
    off = h_row0 // POST_ROWS
    per = POST_ROWS // POOL_HALO
    return pl.pallas_call(
        _mixer_a_post_kernel,
        grid=(nb, nt),
        in_specs=[pl.BlockSpec((None, POST_ROWS, M_DIM), lambda b, i: (b, i + off, 0)),
                  pl.BlockSpec((None, POST_ROWS, M_DIM), lambda b, i: (b, i + off, 0)),
                  pl.BlockSpec((None, POST_ROWS, M_DIM), lambda b, i: (b, i, 3)),
                  pl.BlockSpec((None, POST_ROWS, P_DIM), lambda b, i: (b, i, 0)),
                  pl.BlockSpec((None, POOL_HALO, P_DIM), lambda b, i: (b, jnp.maximum(i * per - 1, 0), 0)),
                  pl.BlockSpec((None, POOL_HALO, P_DIM), lambda b, i: (b, jnp.minimum((i + 1) * per, nt * per - 1), 0)),
                  pl.BlockSpec((1, M_DIM), lambda b, i: (0, 0)),
                  pl.BlockSpec((P_GROUPS, P_GROUP_DIM, P_GROUP_DIM), lambda b, i: (0, 0, 0)),
                  pl.BlockSpec((1, P_DIM), lambda b, i: (0, 0))],
        out_specs=pl.BlockSpec((None, POST_ROWS, D_MODEL), lambda b, i: (b, i, 0)),
        out_shape=jax.ShapeDtypeStruct((nb, t, D_MODEL), BF16),
        compiler_params=_params(2),
        name="mixer_a_post",
    )(h_f, h_b, p, u, u, u, head_g.reshape(1, M_DIM), pool_w, pool_s.reshape(1, P_DIM))


def mixer_mlstm_pool(h_lat, h_ctx, w_in, conv_w, conv_b, gate_b, head_g, pool_w, pool_s, w_out, with_ctx_out):
    t_ctx = h_ctx.shape[1]
    qkv_c, pc, g_c, uc = ab_project(h_ctx, w_in, conv_w, conv_b, gate_b)
    qkv_l, pl_, g_l, ul = ab_project(h_lat, w_in, conv_w, conv_b, gate_b)
    h_f, h_b = mlstm_scan(jnp.concatenate([qkv_c, qkv_l], 1), jnp.concatenate([g_c, g_l], 1),
                          t_ctx // M_CHUNK)
    y_lat = dense(mixer_a_post(h_f, h_b, t_ctx, pl_, ul, head_g, pool_w, pool_s), w_out)
    if not with_ctx_out:
        return y_lat, None
    y_ctx = dense(mixer_a_post(h_f, h_b, 0, pc, uc, head_g, pool_w, pool_s), w_out)
    return y_lat, y_ctx


def axial_rope_tables(n_tokens):
    rows = n_tokens // GRID_W
    row = jnp.repeat(jnp.arange(rows), GRID_W)
    col = jnp.tile(jnp.arange(GRID_W), rows)
    half = ROPE // 2
    inv = ROPE_THETA ** (-jnp.arange(0, half, 2, dtype=F32) / half)
    ang = jnp.stack([row[:, None] * inv, col[:, None] * inv], 1)
    return jnp.cos(ang), jnp.sin(ang)


def apply_axial_rope(x, cos, sin):
    xs = x.reshape(x.shape[:-1] + (2, 2, ROPE // 4))
    x1, x2 = xs[..., 0, :], xs[..., 1, :]
    out = jnp.stack([x1 * cos - x2 * sin, x2 * cos + x1 * sin], -2)
    return out.reshape(x.shape)


LOG2_E = 1.4426950408889634
Q_SCALE = (NOPE + ROPE) ** -0.5 * LOG2_E
ATTN_KEY_SPLITS = 4


def _rms_norm_rows(x, g):
    return x * lax.rsqrt(jnp.mean(jnp.square(x), axis=-1, keepdims=True) + RMS_EPS) * g


def _q_proj_kernel(x_ref, g_ref, w_ref, tab_ref, o_ref, xn_scr):
    @pl.when(pl.program_id(1) == 0)
    def _():
        xn_scr[...] = _rms_norm_rows(x_ref[...], g_ref[...]).astype(BF16)

    o = jnp.dot(xn_scr[...], w_ref[...], preferred_element_type=F32)
    r = o[:, NOPE:] * tab_ref[...]
    rot = r + pltpu.roll(r, ROPE, axis=1)
    o_ref[...] = (jnp.concatenate([o[:, :NOPE], rot[:, :ROPE]], axis=1) * Q_SCALE).astype(o_ref.dtype)


def q_proj(pq, q_norm_g, w_uq, tab):
    nb, t, _ = pq.shape
    w = w_uq.reshape(Q_LORA, A_HEADS, NOPE + ROPE)
    wpe = w[..., NOPE:].reshape(Q_LORA, A_HEADS, 2, 2, ROPE // 4)
    wsw = jnp.stack([-wpe[..., 1, :], wpe[..., 0, :]], -2).reshape(Q_LORA, A_HEADS, ROPE)
    wh = jnp.concatenate([w, wsw], -1).transpose(1, 0, 2).astype(BF16)
    tm = _pick_tile(t, (1024, 512, 256, 128))
    nt = t // tm
    return pl.pallas_call(
        _q_proj_kernel,
        grid=(nb * nt, A_HEADS),
        in_specs=[pl.BlockSpec((tm, Q_LORA), lambda i, h: (i, 0)),
                  pl.BlockSpec((1, Q_LORA), lambda i, h: (0, 0)),
                  pl.BlockSpec((None, Q_LORA, NOPE + 2 * ROPE), lambda i, h: (h, 0, 0)),
                  pl.BlockSpec((tm, 2 * ROPE), lambda i, h: (i % nt, 0))],
        out_specs=pl.BlockSpec((None, None, tm, NOPE + ROPE), lambda i, h: (i // nt, h, i % nt, 0)),
        out_shape=jax.ShapeDtypeStruct((nb, A_HEADS, t, NOPE + ROPE), BF16),
        scratch_shapes=[pltpu.VMEM((tm, Q_LORA), BF16)],
        compiler_params=_params(2),
        name="mla_q_proj",
    )(pq.reshape(nb * t, Q_LORA), q_norm_g.reshape(1, Q_LORA), wh, tab)


def _kv_proj_kernel(x_ref, g_ref, w_ref, kpe_ref, k_ref, v_ref, xn_scr):
    @pl.when(pl.program_id(1) == 0)
    def _():
        xn_scr[...] = _rms_norm_rows(x_ref[...], g_ref[...]).astype(BF16)

    o = jnp.dot(xn_scr[...], w_ref[...], preferred_element_type=F32)
    k_ref[...] = jnp.concatenate([o[:, :NOPE], kpe_ref[...]], axis=1).astype(k_ref.dtype)
    lane = lax.broadcasted_iota(jnp.int32, (o.shape[0], V_DIM), 1)
    ones_col = jnp.where(lane == 0, 1.0, 0.0)
    v_ref[...] = jnp.concatenate([o[:, NOPE:], ones_col], axis=1).astype(v_ref.dtype)


def kv_proj(pkv, kpe, kv_norm_g, w_ukv):
    nb, tk, _ = pkv.shape
    wh = w_ukv.reshape(KV_LORA, A_HEADS, NOPE + V_DIM).transpose(1, 0, 2).astype(BF16)
    tm = _pick_tile(tk, (2176, 1024, 512, 256, 128))
    nt = tk // tm
    return pl.pallas_call(
        _kv_proj_kernel,
        grid=(nb * nt, A_HEADS),
        in_specs=[pl.BlockSpec((tm, KV_LORA), lambda i, h: (i, 0)),
                  pl.BlockSpec((1, KV_LORA), lambda i, h: (0, 0)),
                  pl.BlockSpec((None, KV_LORA, NOPE + V_DIM), lambda i, h: (h, 0, 0)),
                  pl.BlockSpec((tm, ROPE), lambda i, h: (i, 0))],
        out_specs=[pl.BlockSpec((None, None, tm, NOPE + ROPE), lambda i, h: (i // nt, h, i % nt, 0)),
                   pl.BlockSpec((None, None, tm, 2 * V_DIM), lambda i, h: (i // nt, h, i % nt, 0))],
        out_shape=[jax.ShapeDtypeStruct((nb, A_HEADS, tk, NOPE + ROPE), BF16),
                   jax.ShapeDtypeStruct((nb, A_HEADS, tk, 2 * V_DIM), BF16)],
        scratch_shapes=[pltpu.VMEM((tm, KV_LORA), BF16)],
        compiler_params=_params(2),
        name="mla_kv_proj",
    )(pkv.reshape(nb * tk, KV_LORA), kv_norm_g.reshape(1, KV_LORA), wh, kpe.reshape(nb * tk, ROPE))


def _attn_kernel(q_ref, k_ref, v_ref, o_ref):
    q = q_ref[...]
    n_lanes = k_ref.shape[0] // 128
    sizes = [(n_lanes // ATTN_KEY_SPLITS + (c < n_lanes % ATTN_KEY_SPLITS)) * 128 for c in range(ATTN_KEY_SPLITS)]
    m = o = None
    lo = 0
    for size in filter(None, sizes):
        ks = slice(lo, lo + size)
        lo += size
        s = lax.dot_general(q, k_ref[ks, :], (((1,), (1,)), ((), ())), preferred_element_type=F32)
        mc = jnp.max(s, axis=-1, keepdims=True)
        m_new = mc if m is None else jnp.maximum(m, mc)
        p = jnp.exp2((s - m_new).astype(BF16))
        oc = jnp.dot(p, v_ref[ks, :], preferred_element_type=F32)
        o = oc if o is None else o * jnp.exp2(m - m_new) + oc
        m = m_new
    o_ref[...] = (o[:, :V_DIM] / o[:, V_DIM:V_DIM + 1]).astype(o_ref.dtype)


def attend(q, k, vh):
    nb, _, tq, _ = q.shape
    tk = k.shape[2]
    bq = _pick_tile(tq, (1024, 512, 256, 128))
    return pl.pallas_call(
        _attn_kernel,
        grid=(nb, A_HEADS, tq // bq),
        in_specs=[pl.BlockSpec((None, None, bq, NOPE + ROPE), lambda b, h, i: (b, h, i, 0)),
                  pl.BlockSpec((None, None, tk, NOPE + ROPE), lambda b, h, i: (b, h, 0, 0)),
                  pl.BlockSpec((None, None, tk, 2 * V_DIM), lambda b, h, i: (b, h, 0, 0))],
        out_specs=pl.BlockSpec((None, bq, V_DIM), lambda b, h, i: (b, i, h)),
        out_shape=jax.ShapeDtypeStruct((nb, tq, A_HEADS * V_DIM), BF16),
        compiler_params=_params(3),
        name="mla_attention",
    )(q, k, vh)


def mixer_mla(h_lat, h_ctx, w_in, q_norm_g, kv_norm_g, w_uq, w_ukv, w_o, with_ctx_out):
    t_lat, t_ctx = h_lat.shape[1], h_ctx.shape[1]
    cos, sin = axial_rope_tables(t_lat)
    cos_full = jnp.repeat(cos, 2, axis=1).reshape(t_lat, ROPE)
    sin_full = jnp.repeat(sin, 2, axis=1).reshape(t_lat, ROPE)
    pq_l, pkv_l, pr_l = dense(h_lat, w_in, splits=(Q_LORA, KV_LORA, ROPE))
    if with_ctx_out:
        pq_c, pkv_c, pr_c = dense(h_ctx, w_in, splits=(Q_LORA, KV_LORA, ROPE))
    else:
        pkv_c, pr_c = dense(h_ctx, w_in[:, Q_LORA:], splits=(KV_LORA, ROPE))
    kpe = jnp.concatenate([pr_c, apply_axial_rope(pr_l, cos, sin)], 1)
    k, vh = kv_proj(jnp.concatenate([pkv_c, pkv_l], 1), kpe, kv_norm_g, w_ukv)
    q_l = q_proj(pq_l, q_norm_g, w_uq, jnp.concatenate([cos_full, sin_full], -1))
    y_lat = dense(attend(q_l, k, vh), w_o)
    if not with_ctx_out:
        return y_lat, None
    no_rot = jnp.concatenate([jnp.ones((t_ctx, ROPE), F32), jnp.zeros((t_ctx, ROPE), F32)], -1)
    q_c = q_proj(pq_c, q_norm_g, w_uq, no_rot)
    y_ctx = dense(attend(q_c, k[:, :, :t_ctx], vh[:, :, :t_ctx]), w_o)
    return y_lat, y_ctx


MOE_TF = 256
MOE_TN = 256
MOE_NF = D_MODEL // MOE_TF
MOE_NN = D_MODEL // MOE_TN


def _moe_kernel(rows_ref, h_hbm, gate_ref, w1_ref, w3_ref, w2_ref, f_in_hbm, f_hbm,
                xg, xb, hmid, ybuf, stage, sem_g, sem_r, sem_w):
    del f_in_hbm
    e = pl.program_id(0)
    j = pl.program_id(1)
    n_e = pl.num_programs(0)
    m = xg.shape[0]
    per_step = m // MOE_NF
    last_step = MOE_NF + MOE_NN - 1

    prev_e = jnp.maximum(e - 1, 0)
    next_e = jnp.minimum(e + 1, n_e - 1)

    def issue(body, n, inline):
        if inline:
            for i in range(n):
                body(i, 0)
        else:
            lax.fori_loop(0, n, body, 0, unroll=8)

    def queue(i):
        return i % 2 if isinstance(i, int) else 0

    def gather(expert, lo, n, inline=False):
        def body(i, carry):
            r = rows_ref[expert * m + lo + i]
            pltpu.make_async_copy(h_hbm.at[pl.ds(r, 1), :], xg.at[pl.ds(lo + i, 1), :],
                                  sem_g.at[0]).start(priority=queue(i))
            return carry
        issue(body, n, inline)

    def f_rows(expert, piece, slot, to_vmem, inline=False):
        def body(i, carry):
            r = rows_ref[expert * m + piece * per_step + i]
            if to_vmem:
                pltpu.make_async_copy(f_hbm.at[pl.ds(r, 1), :], stage.at[slot, pl.ds(i, 1), :],
                                      sem_r.at[slot]).start(priority=queue(i))
            else:
                pltpu.make_async_copy(stage.at[slot, pl.ds(i, 1), :], f_hbm.at[pl.ds(r, 1), :],
                                      sem_w.at[slot]).start(priority=queue(i))
            return carry
        issue(body, per_step, inline)

    def wait_all(sem, slot):
        pltpu.make_async_copy(stage.at[slot], stage.at[slot], sem.at[slot]).wait()

    def add_and_write(expert, piece, slot, inline=False):
        lo = piece * per_step
        if not isinstance(lo, int):
            lo = pl.multiple_of(lo, 16)
        for n in range(MOE_NN):
            cols = slice(n * MOE_TN, (n + 1) * MOE_TN)
            stage[slot, :, cols] += ybuf[n, pl.ds(lo, per_step), :].astype(F32)
        f_rows(expert, piece, slot, to_vmem=False, inline=inline)

    def up_projection():
        x = xb[...]
        a = jnp.dot(x, w1_ref[...].astype(BF16), preferred_element_type=F32)
        g = jnp.dot(x, w3_ref[...].astype(BF16), preferred_element_type=F32)
        hmid[j] = (jax.nn.silu(a) * g).astype(BF16)

    @pl.when((e == 0) & (j == 0))
    def _():
        gather(0, 0, m)
        ybuf[...] = jnp.zeros_like(ybuf)

    @pl.when((j >= 2) & (j <= MOE_NF + 1))
    def _():
        wait_all(sem_w, j % 2)

    @pl.when((j >= 1) & (j <= MOE_NF))
    def _():
        wait_all(sem_r, (j - 1) % 2)

    @pl.when(j == 0)
    def _():
        pltpu.make_async_copy(xg, xg, sem_g.at[0]).wait()
        w = xg[...]
        xb[:, :HALF_D] = pltpu.bitcast(w << 16, F32).astype(BF16)
        xb[:, HALF_D:] = pltpu.bitcast(w & jnp.uint32(0xFFFF0000), F32).astype(BF16)

    @pl.when(j == 0)
    def _():
        f_rows(prev_e, 0, 0, to_vmem=True, inline=True)
        up_projection()

    @pl.when((j >= 1) & (j < MOE_NF))
    def _():
        add_and_write(prev_e, j - 1, (j - 1) % 2, inline=True)
        f_rows(prev_e, j, j % 2, to_vmem=True, inline=True)
        up_projection()

    @pl.when(j == MOE_NF)
    def _():
        add_and_write(prev_e, MOE_NF - 1, (MOE_NF - 1) % 2)

    @pl.when(j >= MOE_NF)
    def _():
        gather(next_e, (j - MOE_NF) * per_step, per_step, inline=True)
        acc = jnp.dot(hmid[0], w2_ref[0:MOE_TF, :].astype(BF16), preferred_element_type=F32)
        for f in range(1, MOE_NF):
            acc += jnp.dot(hmid[f], w2_ref[f * MOE_TF:(f + 1) * MOE_TF, :].astype(BF16),
                           preferred_element_type=F32)
        g_rows = gate_ref[...]
        gated = []
        for b in range(m // 128):
            g_col = jnp.broadcast_to(g_rows[b:b + 1, :], (128, 128)).T
            gated.append(acc[b * 128:(b + 1) * 128, :] * jnp.concatenate([g_col] * (MOE_TN // 128), axis=1))
        ybuf[j - MOE_NF] = jnp.concatenate(gated, axis=0).astype(BF16)

    @pl.when((e == n_e - 1) & (j == last_step))
    def _():
        pltpu.make_async_copy(xg, xg, sem_g.at[0]).wait()
        for piece in range(MOE_NF):
            f_rows(e, piece, 0, to_vmem=True)
            wait_all(sem_r, 0)
            add_and_write(e, piece, 0)
            wait_all(sem_w, 0)


def expert_ffn_rows(h_packed, rows, gates, w1, w3, w2, layer):
    n_e, m = rows.shape
    n_rows = h_packed.shape[0]
    assert m % 128 == 0 and (m // MOE_NF) % 16 == 0, m
    any_spec = pl.BlockSpec(memory_space=pl.ANY)
    return pl.pallas_call(
        _moe_kernel,
        grid_spec=pltpu.PrefetchScalarGridSpec(
            num_scalar_prefetch=1,
            grid=(n_e, MOE_NF + MOE_NN),
            in_specs=[any_spec,
                      pl.BlockSpec((None, m // 128, 128), lambda e, j, r: (e, 0, 0)),
                      pl.BlockSpec((None, None, D_MODEL, MOE_TF),
                                   lambda e, j, r: (layer, e, 0, jnp.minimum(j, MOE_NF - 1))),
                      pl.BlockSpec((None, None, D_MODEL, MOE_TF),
                                   lambda e, j, r: (layer, e, 0, jnp.minimum(j, MOE_NF - 1))),
                      pl.BlockSpec((None, None, D_MODEL, MOE_TN),
                                   lambda e, j, r: (layer, e, 0, jnp.maximum(j - MOE_NF, 0))),
                      any_spec],
            out_specs=any_spec,
            scratch_shapes=[pltpu.VMEM((m, HALF_D), jnp.uint32),
                            pltpu.VMEM((m, D_MODEL), BF16),
                            pltpu.VMEM((MOE_NF, m, MOE_TF), BF16),
                            pltpu.VMEM((MOE_NN, m, MOE_TN), BF16),
                            pltpu.VMEM((2, m // MOE_NF, D_MODEL), F32),
                            pltpu.SemaphoreType.DMA((1,)),
                            pltpu.SemaphoreType.DMA((2,)),
                            pltpu.SemaphoreType.DMA((2,))]),
        out_shape=jax.ShapeDtypeStruct((n_rows, D_MODEL), F32),
        input_output_aliases={6: 0},
        compiler_params=_params(2),
        name="expert_ffn",
    )(rows.reshape(-1), h_packed, gates.reshape(n_e, m // 128, 128), w1, w3, w2,
      jnp.zeros((n_rows, D_MODEL), F32))


def route(logits, nb):
    t = logits.shape[0] // nb
    cap = EC_FACTOR * t // N_EXPERTS
    aff = jax.nn.softmax(logits.reshape(nb, t, N_EXPERTS), axis=-1)
    return lax.top_k(jnp.swapaxes(aff, 1, 2), cap)


def expert_choice_ffn(sets, nb, w1, w3, w2, layer):
    rows, gates, base = [], [], 0
    for _, logits in sets:
        gate, idx = route(logits, nb)
        t = logits.shape[0] // nb
        rid = base + jnp.arange(nb, dtype=jnp.int32)[:, None, None] * t + idx
        rows.append(rid.transpose(1, 0, 2).reshape(N_EXPERTS, -1))
        gates.append(gate.transpose(1, 0, 2).reshape(N_EXPERTS, -1))
        base += nb * t
    rows = jnp.concatenate(rows, 1)
    h_packed = jnp.concatenate([hp for hp, _ in sets], 0) if len(sets) > 1 else sets[0][0]
    return expert_ffn_rows(h_packed, rows, jnp.concatenate(gates, 1), w1, w3, w2, layer)


def kernel(x, c, ctx, c_ctx, ada_w, ada_b, ln_g, ln_b, ab_w_in, ab_conv_w, ab_conv_b, ab_gate_b,
           ab_head_g, ab_pool_w, ab_pool_s, ab_w_out, mla_w_in, mla_q_norm_g, mla_kv_norm_g,
           mla_w_uq, mla_w_ukv, mla_w_o, moe_router, moe_w1, moe_w3, moe_w2):
    nb, t_lat, _ = x.shape
    t_ctx = ctx.shape[1]
    n_lat, n_ctx = nb * t_lat, nb * t_ctx
    cond = jnp.concatenate([c, c_ctx[None]], 0)
    mods = [adaln(cond, ada_w[i], ada_b[i]) for i in range(DEPTH)]
    mods_l = [[m[:nb] for m in ms] for ms in mods]
    mods_c = [[m[nb:] for m in ms] for ms in mods]
    x_lat, x_ctx = x.reshape(n_lat, D_MODEL), ctx.reshape(n_ctx, D_MODEL)
    h_lat = modulate(x, mods_l[0][0], mods_l[0][1]).astype(BF16)
    h_ctx = modulate(ctx, mods_c[0][0], mods_c[0][1]).astype(BF16)
    for i in range(DEPTH):
        with_ctx = i < DEPTH - 1
        j = i // 2
        _, _, ga_l, shf_l, scf_l, gaf_l = mods_l[i]
        _, _, ga_c, shf_c, scf_c, gaf_c = mods_c[i]
        if i % 2 == 0:
            y_lat, y_ctx = mixer_mlstm_pool(h_lat, h_ctx, ab_w_in[j], ab_conv_w[j], ab_conv_b[j], ab_gate_b[j],
                                            ab_head_g[j], ab_pool_w[j], ab_pool_s[j], ab_w_out[j], with_ctx)
        else:
            y_lat, y_ctx = mixer_mla(h_lat, h_ctx, mla_w_in[j], mla_q_norm_g[j], mla_kv_norm_g[j],
                                     mla_w_uq[j], mla_w_ukv[j], mla_w_o[j], with_ctx)
        x_lat, hp_l, lg_l = post_norm_fused(x_lat, y_lat.reshape(n_lat, D_MODEL), 0, ga_l, ln_g[i, 0], ln_b[i, 0],
                                            t_lat, ffn=(shf_l, scf_l, moe_router[i]))
        sets = [(hp_l, lg_l)]
        if with_ctx:
            x_ctx, hp_c, lg_c = post_norm_fused(x_ctx, y_ctx.reshape(n_ctx, D_MODEL), 0, ga_c, ln_g[i, 0],
                                                ln_b[i, 0], n_ctx, ffn=(shf_c, scf_c, moe_router[i]))
            sets.append((hp_c, lg_c))
        f_all = expert_choice_ffn(sets, nb, moe_w1, moe_w3, moe_w2, i)
        nxt_l = (mods_l[i + 1][0], mods_l[i + 1][1]) if with_ctx else None
        res = post_norm_fused(x_lat, f_all, 0, gaf_l, ln_g[i, 1], ln_b[i, 1], t_lat, nxt=nxt_l)
        x_lat = res[0]
        if with_ctx:
            h_lat = res[1].reshape(nb, t_lat, D_MODEL)
            x_ctx, h_ctx = post_norm_fused(x_ctx, f_all, n_lat, gaf_c, ln_g[i, 1], ln_b[i, 1], n_ctx,
                                           nxt=(mods_c[i + 1][0], mods_c[i + 1][1]))
            h_ctx = h_ctx.reshape(nb, t_ctx, D_MODEL)
    return x_lat.reshape(x.shape)
```

```python
import jax
import jax.numpy as jnp
from jax import lax
from jax.experimental import pallas as pl
from jax.experimental.pallas import tpu as pltpu

D_MODEL = 2048
DEPTH = 2
F32 = jnp.float32
BF16 = jnp.bfloat16
GRID_W = 64

M_HEADS = 4
M_DIM = D_MODEL // 2
M_HEAD_DIM = M_DIM // M_HEADS
M_CHUNK = 128
M_CONV = 3
P_DIM = D_MODEL - M_DIM
P_GROUPS = 4
P_GROUP_DIM = P_DIM // P_GROUPS
POOL_WINDOWS = (2, 4, 8, 16)
A_HEADS = 16
Q_LORA = 1536
KV_LORA = 512
NOPE = 128
ROPE = 64
V_DIM = 128
ROPE_THETA = 10000.0
N_EXPERTS = 16
EC_FACTOR = 2
LN_EPS = 1e-5
RMS_EPS = 1e-6
ALPHA = (2 * DEPTH) ** 0.25

V7X_VMEM_BYTES = 64 * 1024 * 1024
VMEM_LIMIT_BYTES = V7X_VMEM_BYTES - 6 * 1024 * 1024


def _params(n_axes):
    return pltpu.CompilerParams(dimension_semantics=("arbitrary",) * n_axes,
                                vmem_limit_bytes=VMEM_LIMIT_BYTES)


def _mm_kernel(x_ref, w_ref, o_ref):
    o_ref[...] = jnp.dot(x_ref[...].astype(BF16), w_ref[...].astype(BF16),
                         preferred_element_type=F32).astype(o_ref.dtype)


def _pick_tile(n, prefs):
    for t in prefs:
        if n % t == 0:
            return t
    return n


def matmul(x, w, out_dtype=F32):
    if x.ndim == 2:
        return _matmul_grouped(x[None], w[None], out_dtype)[0]
    return _matmul_grouped(x, w, out_dtype)


def _matmul_grouped(x, w, out_dtype):
    g, m, k = x.shape
    n = w.shape[-1]
    tm = _pick_tile(m, (1024, 512, 256, 128))
    tn = _pick_tile(n, (512, 384, 256, 128))
    return pl.pallas_call(
        _mm_kernel,
        grid=(g, n // tn, m // tm),
        in_specs=[pl.BlockSpec((None, tm, k), lambda e, j, i: (e, i, 0)),
                  pl.BlockSpec((None, k, tn), lambda e, j, i: (e, 0, j))],
        out_specs=pl.BlockSpec((None, tm, tn), lambda e, j, i: (e, i, j)),
        out_shape=jax.ShapeDtypeStruct((g, m, n), out_dtype),
        compiler_params=_params(3),
        name="matmul",
    )(x, w)


def dense(x, w, splits=None):
    b, t, k = x.shape
    x2 = x.reshape(b * t, k)
    if splits is None:
        return matmul(x2, w).reshape(b, t, w.shape[-1])
    outs, lo = [], 0
    for width in splits:
        outs.append(matmul(x2, w[:, lo:lo + width]).reshape(b, t, width))
        lo += width
    return outs


def adaln(cond, w, b):
    m = jnp.dot(jax.nn.silu(cond), w, precision=lax.Precision.HIGHEST) + b
    m = m.reshape(cond.shape[0], 6, 1, D_MODEL)
    return [m[:, k] for k in range(6)]


def modulate(x, shift, scale):
    return x * (1.0 + scale) + shift


NORM_ROWS = 256
HALF_D = D_MODEL // 2


def _pack_bf16_pairs(h):
    bits = pltpu.bitcast(h.astype(BF16).astype(F32), jnp.uint32)
    return (bits[:, :HALF_D] >> 16) | (bits[:, HALF_D:] & jnp.uint32(0xFFFF0000))


def _make_post_norm_kernel(with_ffn, with_next):
    def body(*refs):
        x_ref, y_ref, gate_ref, g_ref, b_ref = refs[:5]
        pos = 5
        if with_ffn:
            shf_ref, scf_ref, wr_ref = refs[pos:pos + 3]
            pos += 3
        if with_next:
            shn_ref, scn_ref = refs[pos:pos + 2]
            pos += 2
        outs = refs[pos:]
        z = ALPHA * x_ref[...] + (1.0 + gate_ref[...]) * y_ref[...]
        mu = jnp.mean(z, axis=-1, keepdims=True)
        zc = z - mu
        var = jnp.mean(zc * zc, axis=-1, keepdims=True)
        xn = zc * lax.rsqrt(var + LN_EPS) * g_ref[...] + b_ref[...]
        outs[0][...] = xn
        k = 1
        if with_ffn:
            h = xn * (1.0 + scf_ref[...]) + shf_ref[...]
            outs[k][...] = _pack_bf16_pairs(h)
            outs[k + 1][...] = jnp.dot(h, wr_ref[...], preferred_element_type=F32,
                                       precision=lax.Precision.HIGHEST)
            k += 2
        if with_next:
            outs[k][...] = (xn * (1.0 + scn_ref[...]) + shn_ref[...]).astype(BF16)
    return body


def post_norm_fused(x, y, y_row0, gate, ln_g, ln_b, rows_per_set, ffn=None, nxt=None):
    n = x.shape[0]
    tps = rows_per_set // NORM_ROWS
    off = y_row0 // NORM_ROWS
    row = pl.BlockSpec((NORM_ROWS, D_MODEL), lambda i: (i, 0))
    tab = pl.BlockSpec((None, 1, D_MODEL), lambda i: (i // tps, 0, 0))
    one = pl.BlockSpec((None, 1, D_MODEL), lambda i: (0, 0, 0))
    args = [x, y, gate, ln_g.reshape(1, 1, D_MODEL), ln_b.reshape(1, 1, D_MODEL)]
    in_specs = [row, pl.BlockSpec((NORM_ROWS, D_MODEL), lambda i: (i + off, 0)), tab, one, one]
    out_shape = [jax.ShapeDtypeStruct((n, D_MODEL), F32)]
    out_specs = [row]
    if ffn is not None:
        args += [ffn[0], ffn[1], ffn[2]]
        in_specs += [tab, tab, pl.BlockSpec((D_MODEL, N_EXPERTS), lambda i: (0, 0))]
        out_shape += [jax.ShapeDtypeStruct((n, HALF_D), jnp.uint32), jax.ShapeDtypeStruct((n, N_EXPERTS), F32)]
        out_specs += [pl.BlockSpec((NORM_ROWS, HALF_D), lambda i: (i, 0)),
                      pl.BlockSpec((NORM_ROWS, N_EXPERTS), lambda i: (i, 0))]
    if nxt is not None:
        args += [nxt[0], nxt[1]]
        in_specs += [tab, tab]
        out_shape += [jax.ShapeDtypeStruct((n, D_MODEL), BF16)]
        out_specs += [row]
    return pl.pallas_call(
        _make_post_norm_kernel(ffn is not None, nxt is not None),
        grid=(n // NORM_ROWS,),
        in_specs=in_specs, out_specs=out_specs, out_shape=out_shape,
        compiler_params=_params(1),
        name="post_norm",
    )(*args)


PREP_ROWS = (1024, 512, 256)
PREP_HALO = 8


def _qkv_prep_kernel(x_ref, xprev_ref, xnext_ref, w_ref, b_ref, o_ref):
    i = pl.program_id(1)
    c = pl.program_id(2)
    n_tiles = pl.num_programs(1)

    @pl.when(c == 2)
    def _():
        o_ref[...] = x_ref[...].astype(o_ref.dtype)

    @pl.when(c < 2)
    def _():
        x = x_ref[...]
        prev = jnp.where(i == 0, 0.0, xprev_ref[...])
        nxt = jnp.where(i == n_tiles - 1, 0.0, xnext_ref[...])
        ext = jnp.concatenate([prev, x, nxt], axis=0)
        rows = x.shape[0]
        n_ext = rows + 2 * PREP_HALO
        x_m1 = pltpu.roll(ext, 1, axis=0)[PREP_HALO:PREP_HALO + rows]
        x_p1 = pltpu.roll(ext, n_ext - 1, axis=0)[PREP_HALO:PREP_HALO + rows]
        y = x_m1 * w_ref[0:1, :] + x * w_ref[1:2, :] + x_p1 * w_ref[2:3, :] + b_ref[...]
        scale = jnp.where(c == 1, M_HEAD_DIM ** -0.5, 1.0)
        o_ref[...] = (jax.nn.silu(y) * scale).astype(o_ref.dtype)


def qkv_prep(p, conv_w, conv_b):
    nb, t, _ = p.shape
    rows = _pick_tile(t, PREP_ROWS)
    nt = t // rows
    per = rows // PREP_HALO
    return pl.pallas_call(
        _qkv_prep_kernel,
        grid=(nb, nt, 3),
        in_specs=[pl.BlockSpec((None, rows, M_DIM), lambda b, i, c: (b, i, c)),
                  pl.BlockSpec((None, PREP_HALO, M_DIM), lambda b, i, c: (b, jnp.maximum(i * per - 1, 0), c)),
                  pl.BlockSpec((None, PREP_HALO, M_DIM),
                               lambda b, i, c: (b, jnp.minimum((i + 1) * per, nt * per - 1), c)),
                  pl.BlockSpec((M_CONV, M_DIM), lambda b, i, c: (0, jnp.minimum(c, 1))),
                  pl.BlockSpec((1, M_DIM), lambda b, i, c: (0, jnp.minimum(c, 1)))],
        out_specs=pl.BlockSpec((None, rows, M_DIM), lambda b, i, c: (b, i, c)),
        out_shape=jax.ShapeDtypeStruct((nb, t, 3 * M_DIM), BF16),
        compiler_params=_params(3),
        name="qkv_prep",
    )(p, p, p, conv_w, conv_b.reshape(1, 2 * M_DIM))


def ab_project(h, w_in, conv_w, conv_b, gate_b):
    p, pg, u = dense(h, w_in, splits=(4 * M_DIM, 4 * M_HEADS, P_DIM))
    qkv = qkv_prep(p, conv_w, conv_b)
    gates = (pg + gate_b).reshape(p.shape[:2] + (2, 2, M_HEADS))
    li = gates[..., 0, :]
    lf = jax.nn.log_sigmoid(gates[..., 1, :])
    grow = jnp.stack([li, lf], 3).reshape(p.shape[:2] + (4 * M_HEADS,))
    return qkv, p, grow, u


def _mlstm_chain(q, k, v, li_row, lf_row, c_ref, n_ref, m_ref, reverse):
    L = M_CHUNK
    ti = lax.broadcasted_iota(jnp.int32, (L, L), 0)
    si = lax.broadcasted_iota(jnp.int32, (L, L), 1)
    mask = (si >= ti) if reverse else (si <= ti)
    mask_t = (ti >= si) if reverse else (ti <= si)
    lf_b = jnp.broadcast_to(lf_row, (L, L))
    lf_bt = lf_b.T
    li_col = jnp.broadcast_to(li_row, (L, L)).T[:, 0:1]
    b_col = jnp.sum(jnp.where(mask, lf_b, 0.0), axis=1, keepdims=True)
    b_row = jnp.sum(jnp.where(mask_t, lf_bt, 0.0), axis=0, keepdims=True)
    a_row = li_row - b_row
    dmat = jnp.where(mask, b_col + a_row, -jnp.inf)
    m_prev = m_ref[...][:, 0:1]
    inter = b_col + m_prev
    m_t = jnp.maximum(inter, jnp.max(dmat, axis=1, keepdims=True))
    w_intra = jnp.exp(dmat - m_t)
    w_inter = jnp.exp(inter - m_t)
    s = lax.dot_general(q, k, (((1,), (1,)), ((), ())), preferred_element_type=F32) * w_intra
    c_prev = c_ref[...]
    n_prev = n_ref[...]
    num = (w_inter * jnp.dot(q, c_prev.astype(BF16), preferred_element_type=F32)
           + jnp.dot(s.astype(BF16), v, preferred_element_type=F32))
    den = (w_inter * jnp.sum(q.astype(F32) * n_prev, axis=1, keepdims=True)
           + jnp.sum(s, axis=1, keepdims=True))
    h = num / jnp.maximum(jnp.abs(den), jnp.exp(-m_t))
    bl = jnp.sum(lf_row, axis=1, keepdims=True)
    m_new = jnp.maximum(bl + m_prev, jnp.max(bl + a_row, axis=1, keepdims=True))
    decay = jnp.exp(bl + m_prev - m_new)
    wg_col = jnp.exp(bl - b_col + li_col - m_new)
    kw = k.astype(F32) * wg_col
    c_ref[...] = decay * c_prev + jnp.dot(kw.T.astype(BF16), v, preferred_element_type=F32)
    n_ref[...] = decay * n_prev + jnp.sum(kw, axis=0, keepdims=True)
    m_ref[...] = jnp.broadcast_to(m_new, m_ref.shape)
    return h


def _mlstm_kernel(qf_ref, kf_ref, vf_ref, gf_ref, qb_ref, kb_ref, vb_ref, gb_ref,
                  hf_ref, hb_ref, c_scr, n_scr, m_scr):
    @pl.when(pl.program_id(1) == 0)
    def _():
        c_scr[...] = jnp.zeros_like(c_scr)
        n_scr[...] = jnp.zeros_like(n_scr)
        m_scr[...] = jnp.zeros_like(m_scr)

    dirs = ((qf_ref, kf_ref, vf_ref, gf_ref, hf_ref), (qb_ref, kb_ref, vb_ref, gb_ref, hb_ref))
    for d, (q_ref, k_ref, v_ref, g_ref, h_ref) in enumerate(dirs):
        for hd in range(M_HEADS):
            cols = slice(hd * M_HEAD_DIM, (hd + 1) * M_HEAD_DIM)
            row_i = d * 2 * M_HEADS + hd
            row_f = row_i + M_HEADS
            h = _mlstm_chain(q_ref[:, cols], k_ref[:, cols], v_ref[:, cols],
                             g_ref[row_i:row_i + 1, :], g_ref[row_f:row_f + 1, :],
                             c_scr.at[d, hd], n_scr.at[d, hd], m_scr.at[d, hd], reverse=(d == 1))
            h_ref[:, cols] = h.astype(h_ref.dtype)


def mlstm_scan(qkv, grow, n_ctx_chunks):
    nb, tt, _ = qkv.shape
    nc = tt // M_CHUNK
    g = grow.reshape(nb, nc, M_CHUNK, 4 * M_HEADS).transpose(0, 1, 3, 2)

    def fwd(c):
        return c

    def bwd(c):
        return jnp.where(c < n_ctx_chunks, n_ctx_chunks - 1 - c, nc + n_ctx_chunks - 1 - c)

    def specs(cm):
        return [pl.BlockSpec((None, M_CHUNK, M_DIM), lambda b, c: (b, cm(c), 0)),
                pl.BlockSpec((None, M_CHUNK, M_DIM), lambda b, c: (b, cm(c), 1)),
                pl.BlockSpec((None, M_CHUNK, M_DIM), lambda b, c: (b, cm(c), 2)),
                pl.BlockSpec((None, None, 4 * M_HEADS, M_CHUNK), lambda b, c: (b, cm(c), 0, 0))]

    out_sds = jax.ShapeDtypeStruct((nb, tt, M_DIM), F32)
    return pl.pallas_call(
        _mlstm_kernel,
        grid=(nb, nc),
        in_specs=specs(fwd) + specs(bwd),
        out_specs=[pl.BlockSpec((None, M_CHUNK, M_DIM), lambda b, c: (b, fwd(c), 0)),
                   pl.BlockSpec((None, M_CHUNK, M_DIM), lambda b, c: (b, bwd(c), 0))],
        out_shape=[out_sds, out_sds],
        scratch_shapes=[pltpu.VMEM((2, M_HEADS, M_HEAD_DIM, M_HEAD_DIM), F32),
                        pltpu.VMEM((2, M_HEADS, 1, M_HEAD_DIM), F32),
                        pltpu.VMEM((2, M_HEADS, 1, 128), F32)],
        compiler_params=_params(2),
        name="mlstm_scan",
    )(qkv, qkv, qkv, g, qkv, qkv, qkv, g)


POST_ROWS = 256
POOL_HALO = 8


def _mixer_a_post_kernel(hf_ref, hb_ref, o_ref, u_ref, uprev_ref, unext_ref, hg_ref, pw_ref, ps_ref, out_ref):
    i = pl.program_id(1)
    n_tiles = pl.num_programs(1)
    t_seq = n_tiles * POST_ROWS
    hs = hf_ref[...] + hb_ref[...]
    gate = jax.nn.sigmoid(o_ref[...])
    for hd in range(M_HEADS):
        cols = slice(hd * M_HEAD_DIM, (hd + 1) * M_HEAD_DIM)
        x = hs[:, cols]
        xc = x - jnp.mean(x, axis=-1, keepdims=True)
        var = jnp.mean(xc * xc, axis=-1, keepdims=True)
        out_ref[:, cols] = (xc * lax.rsqrt(var + LN_EPS) * hg_ref[:, cols] * gate[:, cols]).astype(out_ref.dtype)
    u = u_ref[...]
    prev = jnp.where(i == 0, 0.0, uprev_ref[...])
    nxt = jnp.where(i == n_tiles - 1, 0.0, unext_ref[...])
    ext = jnp.concatenate([prev, u, nxt], axis=0)
    n_ext = POST_ROWS + 2 * POOL_HALO
    t = i * POST_ROWS + lax.broadcasted_iota(jnp.int32, (POST_ROWS, 1), 0)
    for g, w in enumerate(POOL_WINDOWS):
        cols = slice(g * P_GROUP_DIM, (g + 1) * P_GROUP_DIM)
        a = ext[:, cols]
        a = a + pltpu.roll(a, 1, axis=0)
        step = 1
        while 2 * step < w:
            a = pltpu.roll(a, step, axis=0) + pltpu.roll(a, n_ext - step, axis=0)
            step *= 2
        count = (jnp.minimum(t + w // 2, t_seq) - jnp.maximum(t - w // 2, 0)).astype(F32)
        pooled = a[POOL_HALO:POOL_HALO + POST_ROWS] / count - u[:, cols]
        y = jnp.dot(pooled.astype(BF16), pw_ref[g].astype(BF16), preferred_element_type=F32)
        out_ref[:, M_DIM + g * P_GROUP_DIM:M_DIM + (g + 1) * P_GROUP_DIM] = (y * ps_ref[:, cols]).astype(out_ref.dtype)


def mixer_a_post(h_f, h_b, h_row0, p, u, head_g, pool_w, pool_s):
    nb, t, _ = u.shape
    nt = t // POST_ROWS
    off = h_row0 // POST_ROWS
    per = POST_ROWS // POOL_HALO
    return pl.pallas_call(
        _mixer_a_post_kernel,
        grid=(nb, nt),
        in_specs=[pl.BlockSpec((None, POST_ROWS, M_DIM), lambda b, i: (b, i + off, 0)),
                  pl.BlockSpec((None, POST_ROWS, M_DIM), lambda b, i: (b, i + off, 0)),
                  pl.BlockSpec((None, POST_ROWS, M_DIM), lambda b, i: (b, i, 3)),
                  pl.BlockSpec((None, POST_ROWS, P_DIM), lambda b, i: (b, i, 0)),
                  pl.BlockSpec((None, POOL_HALO, P_DIM), lambda b, i: (b, jnp.maximum(i * per - 1, 0), 0)),
                  pl.BlockSpec((None, POOL_HALO, P_DIM), lambda b, i: (b, jnp.minimum((i + 1) * per, nt * per - 1), 0)),
                  pl.BlockSpec((1, M_DIM), lambda b, i: (0, 0)),
                  pl.BlockSpec((P_GROUPS, P_GROUP_DIM, P_GROUP_DIM), lambda b, i: (0, 0, 0)),
                  pl.BlockSpec((1, P_DIM), lambda b, i: (0, 0))],
        out_specs=pl.BlockSpec((None, POST_ROWS, D_MODEL), lambda b, i: (b, i, 0)),
        out_shape=jax.ShapeDtypeStruct((nb, t, D_MODEL), BF16),
        compiler_params=_params(2),
        name="mixer_a_post",
    )(h_f, h_b, p, u, u, u, head_g.reshape(1, M_DIM), pool_w, pool_s.reshape(1, P_DIM))


def mixer_mlstm_pool(h_lat, h_ctx, w_in, conv_w, conv_b, gate_b, head_g, pool_w, pool_s, w_out, with_ctx_out):
    t_ctx = h_ctx.shape[1]
    qkv_c, pc, g_c, uc = ab_project(h_ctx, w_in, conv_w, conv_b, gate_b)
    qkv_l, pl_, g_l, ul = ab_project(h_lat, w_in, conv_w, conv_b, gate_b)
    h_f, h_b = mlstm_scan(jnp.concatenate([qkv_c, qkv_l], 1), jnp.concatenate([g_c, g_l], 1),
                          t_ctx // M_CHUNK)
    y_lat = dense(mixer_a_post(h_f, h_b, t_ctx, pl_, ul, head_g, pool_w, pool_s), w_out)
    if not with_ctx_out:
        return y_lat, None
    y_ctx = dense(mixer_a_post(h_f, h_b, 0, pc, uc, head_g, pool_w, pool_s), w_out)
    return y_lat, y_ctx


def axial_rope_tables(n_tokens):
    rows = n_tokens // GRID_W
    row = jnp.repeat(jnp.arange(rows), GRID_W)
    col = jnp.tile(jnp.arange(GRID_W), rows)
    half = ROPE // 2
    inv = ROPE_THETA ** (-jnp.arange(0, half, 2, dtype=F32) / half)
    ang = jnp.stack([row[:, None] * inv, col[:, None] * inv], 1)
    return jnp.cos(ang), jnp.sin(ang)


def apply_axial_rope(x, cos, sin):
    xs = x.reshape(x.shape[:-1] + (2, 2, ROPE // 4))
    x1, x2 = xs[..., 0, :], xs[..., 1, :]
    out = jnp.stack([x1 * cos - x2 * sin, x2 * cos + x1 * sin], -2)
    return out.reshape(x.shape)


LOG2_E = 1.4426950408889634
Q_SCALE = (NOPE + ROPE) ** -0.5 * LOG2_E
ATTN_KEY_SPLITS = 8


def _rms_norm_rows(x, g):
    return x * lax.rsqrt(jnp.mean(jnp.square(x), axis=-1, keepdims=True) + RMS_EPS) * g


def _q_proj_kernel(x_ref, g_ref, w_ref, tab_ref, o_ref, xn_scr):
    @pl.when(pl.program_id(1) == 0)
    def _():
        xn_scr[...] = _rms_norm_rows(x_ref[...], g_ref[...]).astype(BF16)

    o = jnp.dot(xn_scr[...], w_ref[...], preferred_element_type=F32)
    r = o[:, NOPE:] * tab_ref[...]
    rot = r + pltpu.roll(r, ROPE, axis=1)
    o_ref[...] = (jnp.concatenate([o[:, :NOPE], rot[:, :ROPE]], axis=1) * Q_SCALE).astype(o_ref.dtype)


def q_proj(pq, q_norm_g, w_uq, tab):
    nb, t, _ = pq.shape
    w = w_uq.reshape(Q_LORA, A_HEADS, NOPE + ROPE)
    wpe = w[..., NOPE:].reshape(Q_LORA, A_HEADS, 2, 2, ROPE // 4)
    wsw = jnp.stack([-wpe[..., 1, :], wpe[..., 0, :]], -2).reshape(Q_LORA, A_HEADS, ROPE)
    wh = jnp.concatenate([w, wsw], -1).transpose(1, 0, 2).astype(BF16)
    tm = _pick_tile(t, (1024, 512, 256, 128))
    nt = t // tm
    return pl.pallas_call(
        _q_proj_kernel,
        grid=(nb * nt, A_HEADS),
        in_specs=[pl.BlockSpec((tm, Q_LORA), lambda i, h: (i, 0)),
                  pl.BlockSpec((1, Q_LORA), lambda i, h: (0, 0)),
                  pl.BlockSpec((None, Q_LORA, NOPE + 2 * ROPE), lambda i, h: (h, 0, 0)),
                  pl.BlockSpec((tm, 2 * ROPE), lambda i, h: (i % nt, 0))],
        out_specs=pl.BlockSpec((None, None, tm, NOPE + ROPE), lambda i, h: (i // nt, h, i % nt, 0)),
        out_shape=jax.ShapeDtypeStruct((nb, A_HEADS, t, NOPE + ROPE), BF16),
        scratch_shapes=[pltpu.VMEM((tm, Q_LORA), BF16)],
        compiler_params=_params(2),
        name="mla_q_proj",
    )(pq.reshape(nb * t, Q_LORA), q_norm_g.reshape(1, Q_LORA), wh, tab)


def _kv_proj_kernel(x_ref, g_ref, w_ref, kpe_ref, k_ref, v_ref, xn_scr):
    @pl.when(pl.program_id(1) == 0)
    def _():
        xn_scr[...] = _rms_norm_rows(x_ref[...], g_ref[...]).astype(BF16)

    o = jnp.dot(xn_scr[...], w_ref[...], preferred_element_type=F32)
    k_ref[...] = jnp.concatenate([o[:, :NOPE], kpe_ref[...]], axis=1).astype(k_ref.dtype)
    lane = lax.broadcasted_iota(jnp.int32, (o.shape[0], V_DIM), 1)
    ones_col = jnp.where(lane == 0, 1.0, 0.0)
    v_ref[...] = jnp.concatenate([o[:, NOPE:], ones_col], axis=1).astype(v_ref.dtype)


def kv_proj(pkv, kpe, kv_norm_g, w_ukv):
    nb, tk, _ = pkv.shape
    wh = w_ukv.reshape(KV_LORA, A_HEADS, NOPE + V_DIM).transpose(1, 0, 2).astype(BF16)
    tm = _pick_tile(tk, (2176, 1024, 512, 256, 128))
    nt = tk // tm
    return pl.pallas_call(
        _kv_proj_kernel,
        grid=(nb * nt, A_HEADS),
        in_specs=[pl.BlockSpec((tm, KV_LORA), lambda i, h: (i, 0)),
                  pl.BlockSpec((1, KV_LORA), lambda i, h: (0, 0)),
                  pl.BlockSpec((None, KV_LORA, NOPE + V_DIM), lambda i, h: (h, 0, 0)),
                  pl.BlockSpec((tm, ROPE), lambda i, h: (i, 0))],
        out_specs=[pl.BlockSpec((None, None, tm, NOPE + ROPE), lambda i, h: (i // nt, h, i % nt, 0)),
                   pl.BlockSpec((None, None, tm, 2 * V_DIM), lambda i, h: (i // nt, h, i % nt, 0))],
        out_shape=[jax.ShapeDtypeStruct((nb, A_HEADS, tk, NOPE + ROPE), BF16),
                   jax.ShapeDtypeStruct((nb, A_HEADS, tk, 2 * V_DIM), BF16)],
        scratch_shapes=[pltpu.VMEM((tm, KV_LORA), BF16)],
        compiler_params=_params(2),
        name="mla_kv_proj",
    )(pkv.reshape(nb * tk, KV_LORA), kv_norm_g.reshape(1, KV_LORA), wh, kpe.reshape(nb * tk, ROPE))


def _attn_kernel(q_ref, k_ref, v_ref, o_ref):
    q = q_ref[...]
    n_lanes = k_ref.shape[0] // 128
    sizes = [(n_lanes // ATTN_KEY_SPLITS + (c < n_lanes % ATTN_KEY_SPLITS)) * 128 for c in range(ATTN_KEY_SPLITS)]
    m = o = None
    lo = 0
    for size in filter(None, sizes):
        ks = slice(lo, lo + size)
        lo += size
        s = lax.dot_general(q, k_ref[ks, :], (((1,), (1,)), ((), ())), preferred_element_type=F32)
        mc = jnp.max(s, axis=-1, keepdims=True)
        m_new = mc if m is None else jnp.maximum(m, mc)
        p = jnp.exp2((s - m_new).astype(BF16))
        oc = jnp.dot(p, v_ref[ks, :], preferred_element_type=F32)
        o = oc if o is None else o * jnp.exp2(m - m_new) + oc
        m = m_new
    o_ref[...] = (o[:, :V_DIM] / o[:, V_DIM:V_DIM + 1]).astype(o_ref.dtype)


def attend(q, k, vh):
    nb, _, tq, _ = q.shape
    tk = k.shape[2]
    bq = _pick_tile(tq, (1024, 512, 256, 128))
    return pl.pallas_call(
        _attn_kernel,
        grid=(nb, A_HEADS, tq // bq),
        in_specs=[pl.BlockSpec((None, None, bq, NOPE + ROPE), lambda b, h, i: (b, h, i, 0)),
                  pl.BlockSpec((None, None, tk, NOPE + ROPE), lambda b, h, i: (b, h, 0, 0)),
                  pl.BlockSpec((None, None, tk, 2 * V_DIM), lambda b, h, i: (b, h, 0, 0))],
        out_specs=pl.BlockSpec((None, bq, V_DIM), lambda b, h, i: (b, i, h)),
        out_shape=jax.ShapeDtypeStruct((nb, tq, A_HEADS * V_DIM), BF16),
        compiler_params=_params(3),
        name="mla_attention",
    )(q, k, vh)


def mixer_mla(h_lat, h_ctx, w_in, q_norm_g, kv_norm_g, w_uq, w_ukv, w_o, with_ctx_out):
    t_lat, t_ctx = h_lat.shape[1], h_ctx.shape[1]
    cos, sin = axial_rope_tables(t_lat)
    cos_full = jnp.repeat(cos, 2, axis=1).reshape(t_lat, ROPE)
    sin_full = jnp.repeat(sin, 2, axis=1).reshape(t_lat, ROPE)
    pq_l, pkv_l, pr_l = dense(h_lat, w_in, splits=(Q_LORA, KV_LORA, ROPE))
    if with_ctx_out:
        pq_c, pkv_c, pr_c = dense(h_ctx, w_in, splits=(Q_LORA, KV_LORA, ROPE))
    else:
        pkv_c, pr_c = dense(h_ctx, w_in[:, Q_LORA:], splits=(KV_LORA, ROPE))
    kpe = jnp.concatenate([pr_c, apply_axial_rope(pr_l, cos, sin)], 1)
    k, vh = kv_proj(jnp.concatenate([pkv_c, pkv_l], 1), kpe, kv_norm_g, w_ukv)
    q_l = q_proj(pq_l, q_norm_g, w_uq, jnp.concatenate([cos_full, sin_full], -1))
    y_lat = dense(attend(q_l, k, vh), w_o)
    if not with_ctx_out:
        return y_lat, None
    no_rot = jnp.concatenate([jnp.ones((t_ctx, ROPE), F32), jnp.zeros((t_ctx, ROPE), F32)], -1)
    q_c = q_proj(pq_c, q_norm_g, w_uq, no_rot)
    y_ctx = dense(attend(q_c, k[:, :, :t_ctx], vh[:, :, :t_ctx]), w_o)
    return y_lat, y_ctx


MOE_TF = 256
MOE_TN = 256
MOE_NF = D_MODEL // MOE_TF
MOE_NN = D_MODEL // MOE_TN


def _moe_kernel(rows_ref, h_hbm, gate_ref, w1_ref, w3_ref, w2_ref, f_in_hbm, f_hbm,
                xg, xb, hmid, ybuf, stage, sem_g, sem_r, sem_w):
    del f_in_hbm
    e = pl.program_id(0)
    j = pl.program_id(1)
    n_e = pl.num_programs(0)
    m = xg.shape[0]
    per_step = m // MOE_NF
    last_step = MOE_NF + MOE_NN - 1

    prev_e = jnp.maximum(e - 1, 0)
    next_e = jnp.minimum(e + 1, n_e - 1)

    def issue(body, n, inline):
        if inline:
            for i in range(n):
                body(i, 0)
        else:
            lax.fori_loop(0, n, body, 0, unroll=8)

    def gather(expert, lo, n, inline=False):
        def body(i, carry):
            r = rows_ref[expert * m + lo + i]
            pltpu.make_async_copy(h_hbm.at[pl.ds(r, 1), :], xg.at[pl.ds(lo + i, 1), :], sem_g.at[0]).start()
            return carry
        issue(body, n, inline)

    def f_rows(expert, piece, slot, to_vmem, inline=False):
        def body(i, carry):
            r = rows_ref[expert * m + piece * per_step + i]
            if to_vmem:
                pltpu.make_async_copy(f_hbm.at[pl.ds(r, 1), :], stage.at[slot, pl.ds(i, 1), :],
                                      sem_r.at[slot]).start()
            else:
                pltpu.make_async_copy(stage.at[slot, pl.ds(i, 1), :], f_hbm.at[pl.ds(r, 1), :],
                                      sem_w.at[slot]).start()
            return carry
        issue(body, per_step, inline)

    def wait_all(sem, slot):
        pltpu.make_async_copy(stage.at[slot], stage.at[slot], sem.at[slot]).wait()

    def add_and_write(expert, piece, slot, inline=False):
        lo = piece * per_step
        if not isinstance(lo, int):
            lo = pl.multiple_of(lo, 16)
        for n in range(MOE_NN):
            cols = slice(n * MOE_TN, (n + 1) * MOE_TN)
            stage[slot, :, cols] += ybuf[n, pl.ds(lo, per_step), :].astype(F32)
        f_rows(expert, piece, slot, to_vmem=False, inline=inline)

    def up_projection():
        x = xb[...]
        a = jnp.dot(x, w1_ref[...].astype(BF16), preferred_element_type=F32)
        g = jnp.dot(x, w3_ref[...].astype(BF16), preferred_element_type=F32)
        hmid[j] = (jax.nn.silu(a) * g).astype(BF16)

    @pl.when((e == 0) & (j == 0))
    def _():
        gather(0, 0, m)
        ybuf[...] = jnp.zeros_like(ybuf)

    @pl.when((j >= 2) & (j <= MOE_NF + 1))
    def _():
        wait_all(sem_w, j % 2)

    @pl.when((j >= 1) & (j <= MOE_NF))
    def _():
        wait_all(sem_r, (j - 1) % 2)

    @pl.when(j == 0)
    def _():
        pltpu.make_async_copy(xg, xg, sem_g.at[0]).wait()
        w = xg[...]
        xb[:, :HALF_D] = pltpu.bitcast(w << 16, F32).astype(BF16)
        xb[:, HALF_D:] = pltpu.bitcast(w & jnp.uint32(0xFFFF0000), F32).astype(BF16)

    @pl.when(j == 0)
    def _():
        f_rows(prev_e, 0, 0, to_vmem=True, inline=True)
        up_projection()

    @pl.when((j >= 1) & (j < MOE_NF))
    def _():
        add_and_write(prev_e, j - 1, (j - 1) % 2, inline=True)
        f_rows(prev_e, j, j % 2, to_vmem=True, inline=True)
        up_projection()

    @pl.when(j == MOE_NF)
    def _():
        add_and_write(prev_e, MOE_NF - 1, (MOE_NF - 1) % 2)

    @pl.when(j >= MOE_NF)
    def _():
        gather(next_e, (j - MOE_NF) * per_step, per_step, inline=True)
        acc = jnp.dot(hmid[0], w2_ref[0:MOE_TF, :].astype(BF16), preferred_element_type=F32)
        for f in range(1, MOE_NF):
            acc += jnp.dot(hmid[f], w2_ref[f * MOE_TF:(f + 1) * MOE_TF, :].astype(BF16),
                           preferred_element_type=F32)
        g_rows = gate_ref[...]
        gated = []
        for b in range(m // 128):
            g_col = jnp.broadcast_to(g_rows[b:b + 1, :], (128, 128)).T
            gated.append(acc[b * 128:(b + 1) * 128, :] * jnp.concatenate([g_col] * (MOE_TN // 128), axis=1))
        ybuf[j - MOE_NF] = jnp.concatenate(gated, axis=0).astype(BF16)

    @pl.when((e == n_e - 1) & (j == last_step))
    def _():
        pltpu.make_async_copy(xg, xg, sem_g.at[0]).wait()
        for piece in range(MOE_NF):
            f_rows(e, piece, 0, to_vmem=True)
            wait_all(sem_r, 0)
            add_and_write(e, piece, 0)
            wait_all(sem_w, 0)


def expert_ffn_rows(h_packed, rows, gates, w1, w3, w2, layer):
    n_e, m = rows.shape
    n_rows = h_packed.shape[0]
    assert m % 128 == 0 and (m // MOE_NF) % 16 == 0, m
    any_spec = pl.BlockSpec(memory_space=pl.ANY)
    return pl.pallas_call(
        _moe_kernel,
        grid_spec=pltpu.PrefetchScalarGridSpec(
            num_scalar_prefetch=1,
            grid=(n_e, MOE_NF + MOE_NN),
            in_specs=[any_spec,
                      pl.BlockSpec((None, m // 128, 128), lambda e, j, r: (e, 0, 0)),
                      pl.BlockSpec((None, None, D_MODEL, MOE_TF),
                                   lambda e, j, r: (layer, e, 0, jnp.minimum(j, MOE_NF - 1))),
                      pl.BlockSpec((None, None, D_MODEL, MOE_TF),
                                   lambda e, j, r: (layer, e, 0, jnp.minimum(j, MOE_NF - 1))),
                      pl.BlockSpec((None, None, D_MODEL, MOE_TN),
                                   lambda e, j, r: (layer, e, 0, jnp.maximum(j - MOE_NF, 0))),
                      any_spec],
            out_specs=any_spec,
            scratch_shapes=[pltpu.VMEM((m, HALF_D), jnp.uint32),
                            pltpu.VMEM((m, D_MODEL), BF16),
                            pltpu.VMEM((MOE_NF, m, MOE_TF), BF16),
                            pltpu.VMEM((MOE_NN, m, MOE_TN), BF16),
                            pltpu.VMEM((2, m // MOE_NF, D_MODEL), F32),
                            pltpu.SemaphoreType.DMA((1,)),
                            pltpu.SemaphoreType.DMA((2,)),
                            pltpu.SemaphoreType.DMA((2,))]),
        out_shape=jax.ShapeDtypeStruct((n_rows, D_MODEL), F32),
        input_output_aliases={6: 0},
        compiler_params=_params(2),
        name="expert_ffn",
    )(rows.reshape(-1), h_packed, gates.reshape(n_e, m // 128, 128), w1, w3, w2,
      jnp.zeros((n_rows, D_MODEL), F32))


def route(logits, nb):
    t = logits.shape[0] // nb
    cap = EC_FACTOR * t // N_EXPERTS
    aff = jax.nn.softmax(logits.reshape(nb, t, N_EXPERTS), axis=-1)
    return lax.top_k(jnp.swapaxes(aff, 1, 2), cap)


def expert_choice_ffn(sets, nb, w1, w3, w2, layer):
    rows, gates, base = [], [], 0
    for _, logits in sets:
        gate, idx = route(logits, nb)
        t = logits.shape[0] // nb
        rid = base + jnp.arange(nb, dtype=jnp.int32)[:, None, None] * t + idx
        rows.append(rid.transpose(1, 0, 2).reshape(N_EXPERTS, -1))
        gates.append(gate.transpose(1, 0, 2).reshape(N_EXPERTS, -1))
        base += nb * t
    rows = jnp.concatenate(rows, 1)
    h_packed = jnp.concatenate([hp for hp, _ in sets], 0) if len(sets) > 1 else sets[0][0]
    return expert_ffn_rows(h_packed, rows, jnp.concatenate(gates, 1), w1, w3, w2, layer)


def kernel(x, c, ctx, c_ctx, ada_w, ada_b, ln_g, ln_b, ab_w_in, ab_conv_w, ab_conv_b, ab_gate_b,
           ab_head_g, ab_pool_w, ab_pool_s, ab_w_out, mla_w_in, mla_q_norm_g, mla_kv_norm_g,
           mla_w_uq, mla_w_ukv, mla_w_o, moe_router, moe_w1, moe_w3, moe_w2):
    nb, t_lat, _ = x.shape
    t_ctx = ctx.shape[1]
    n_lat, n_ctx = nb * t_lat, nb * t_ctx
    cond = jnp.concatenate([c, c_ctx[None]], 0)
    mods = [adaln(cond, ada_w[i], ada_b[i]) for i in range(DEPTH)]
    mods_l = [[m[:nb] for m in ms] for ms in mods]
    mods_c = [[m[nb:] for m in ms] for ms in mods]
    x_lat, x_ctx = x.reshape(n_lat, D_MODEL), ctx.reshape(n_ctx, D_MODEL)
    h_lat = modulate(x, mods_l[0][0], mods_l[0][1]).astype(BF16)
    h_ctx = modulate(ctx, mods_c[0][0], mods_c[0][1]).astype(BF16)
    for i in range(DEPTH):
        with_ctx = i < DEPTH - 1
        j = i // 2
        _, _, ga_l, shf_l, scf_l, gaf_l = mods_l[i]
        _, _, ga_c, shf_c, scf_c, gaf_c = mods_c[i]
        if i % 2 == 0:
            y_lat, y_ctx = mixer_mlstm_pool(h_lat, h_ctx, ab_w_in[j], ab_conv_w[j], ab_conv_b[j], ab_gate_b[j],
                                            ab_head_g[j], ab_pool_w[j], ab_pool_s[j], ab_w_out[j], with_ctx)
        else:
            y_lat, y_ctx = mixer_mla(h_lat, h_ctx, mla_w_in[j], mla_q_norm_g[j], mla_kv_norm_g[j],
                                     mla_w_uq[j], mla_w_ukv[j], mla_w_o[j], with_ctx)
        x_lat, hp_l, lg_l = post_norm_fused(x_lat, y_lat.reshape(n_lat, D_MODEL), 0, ga_l, ln_g[i, 0], ln_b[i, 0],
                                            t_lat, ffn=(shf_l, scf_l, moe_router[i]))
        sets = [(hp_l, lg_l)]
        if with_ctx:
            x_ctx, hp_c, lg_c = post_norm_fused(x_ctx, y_ctx.reshape(n_ctx, D_MODEL), 0, ga_c, ln_g[i, 0],
                                                ln_b[i, 0], n_ctx, ffn=(shf_c, scf_c, moe_router[i]))
            sets.append((hp_c, lg_c))
        f_all = expert_choice_ffn(sets, nb, moe_w1, moe_w3, moe_w2, i)
        nxt_l = (mods_l[i + 1][0], mods_l[i + 1][1]) if with_ctx else None
        res = post_norm_fused(x_lat, f_all, 0, gaf_l, ln_g[i, 1], ln_b[i, 1], t_lat, nxt=nxt_l)
        x_lat = res[0]
        if with_ctx:
            h_lat = res[1].reshape(nb, t_lat, D_MODEL)
            x_ctx, h_ctx = post_norm_fused(x_ctx, f_all, n_lat, gaf_c, ln_g[i, 1], ln_b[i, 1], n_ctx,
                                           nxt=(mods_c[i + 1][0], mods_c[i + 1][1]))
            h_ctx = h_ctx.reshape(nb, t_ctx, D_MODEL)
    return x_lat.reshape(x.shape)
```
